```python
import jax
import jax.numpy as jnp
from jax import lax
import numpy as np


D_MODEL = 2048
BATCH = 8
SEQ = 2048
DEPTH = 2

RMS_EPS = 1e-6
PLE_DIM = 256
SB_HEADS = 8
SB_HEAD_DIM = 128
SB_BLOCK = 128
SB_WIDTH = SB_HEADS * SB_HEAD_DIM
HG_HEADS = 8
HG_KEY_DIM = 128
HG_VAL_DIM = 128
HG_CHUNK = 64
HG_KEY_WIDTH = HG_HEADS * HG_KEY_DIM
HG_VAL_WIDTH = HG_HEADS * HG_VAL_DIM
EVEN_SPLITS = [SB_WIDTH, SB_WIDTH, SB_WIDTH, HG_KEY_WIDTH, HG_KEY_WIDTH, HG_VAL_WIDTH, HG_VAL_WIDTH]
EVEN_IN_WIDTH = sum(EVEN_SPLITS)
EVEN_OUT_WIDTH = SB_WIDTH + HG_VAL_WIDTH
LRU_WIDTH = ((4 * D_MODEL // 3 + 255) // 256) * 256
RG_BLOCK_WIDTH = 256
RG_BLOCKS = LRU_WIDTH // RG_BLOCK_WIDTH
CONV_WIDTH = 4
RG_C = 8.0
D_FF = ((8 * D_MODEL // 3 + 255) // 256) * 256
N_EVEN = (DEPTH + 1) // 2
N_ODD = DEPTH // 2

kernel_name = "hybrid_stickbreak_hgrn2_rglru_block"


def rmsnorm(x, g):
    xf = x.astype(jnp.float32)
    y = xf * lax.rsqrt(jnp.mean(xf * xf, axis=-1, keepdims=True) + RMS_EPS)
    return (y * g.astype(jnp.float32)).astype(x.dtype)


def split_heads(a, n_heads, head_dim):
    b, s, _ = a.shape
    return a.reshape(b, s, n_heads, head_dim).transpose(0, 2, 1, 3)


def merge_heads(a):
    b, h, s, d = a.shape
    return a.transpose(0, 2, 1, 3).reshape(b, s, h * d)


def stick_breaking_attention(q, k, v):
    seq = q.shape[2]
    qf = q.astype(jnp.float32) * (SB_HEAD_DIM ** -0.5)
    kf = k.astype(jnp.float32)
    vf = v.astype(jnp.float32)
    outs = []
    for blk in range(seq // SB_BLOCK):
        q0 = blk * SB_BLOCK
        q1 = q0 + SB_BLOCK
        z = jnp.einsum('bhtd,bhsd->bhts', qf[:, :, q0:q1], kf[:, :, :q1])
        mask = jnp.arange(q1)[None, :] < (q0 + jnp.arange(SB_BLOCK))[:, None]
        log_beta = jax.nn.log_sigmoid(z)
        log_one_minus = jnp.where(mask, log_beta - z, 0.0)
        log_remain = lax.cumsum(log_one_minus, axis=3, reverse=True) - log_one_minus
        w = jnp.where(mask, jnp.exp(log_beta + log_remain), 0.0)
        outs.append(jnp.einsum('bhts,bhsd->bhtd', w, vf[:, :, :q1]))
    return jnp.concatenate(outs, axis=2).astype(q.dtype)


def hgrn2_chunkwise(q, k, v, log_f):
    b, h, s, dk = q.shape
    dv = v.shape[-1]
    n_chunks = s // HG_CHUNK

    def to_chunks(a):
        return a.reshape(b, h, n_chunks, HG_CHUNK, a.shape[-1]).transpose(2, 0, 1, 3, 4)

    causal = (jnp.arange(HG_CHUNK)[:, None] >= jnp.arange(HG_CHUNK)[None, :])[:, :, None]

    def step(state, inp):
        qi, ki, vi, gi = inp
        cum = jnp.cumsum(gi, axis=2)
        o_inter = jnp.einsum('bhtk,bhkv->bhtv', qi * jnp.exp(cum), state)
        diff = cum[:, :, :, None, :] - cum[:, :, None, :, :]
        decay = jnp.exp(jnp.where(causal, diff, -jnp.inf))
        scores = jnp.einsum('bhtk,bhsk,bhtsk->bhts', qi, ki, decay)
        o_intra = jnp.einsum('bhts,bhsv->bhtv', scores, vi)
        last = cum[:, :, -1:, :]
        k_dec = ki * jnp.exp(last - cum)
        new_state = state * jnp.exp(last[:, :, 0, :, None]) + jnp.einsum('bhsk,bhsv->bhkv', k_dec, vi)
        return new_state, o_inter + o_intra

    state0 = jnp.zeros((b, h, dk, dv), jnp.float32)
    _, o = lax.scan(step, state0, (to_chunks(q), to_chunks(k), to_chunks(v), to_chunks(log_f)))
    return o.transpose(1, 2, 0, 3, 4).reshape(b, h, s, dv)


def hgrn2(hq, hf, hi, hg, lb, norm_g):
    lbf = lb.astype(jnp.float32)
    f = lbf + (1.0 - lbf) * jax.nn.sigmoid(hf.astype(jnp.float32))
    q = jax.nn.silu(hq.astype(jnp.float32))
    o = hgrn2_chunkwise(split_heads(q, HG_HEADS, HG_KEY_DIM),
                        split_heads(1.0 - f, HG_HEADS, HG_KEY_DIM),
                        split_heads(hi.astype(jnp.float32), HG_HEADS, HG_VAL_DIM),
                        split_heads(jnp.log(f), HG_HEADS, HG_KEY_DIM))
    o = rmsnorm(o, norm_g)
    return (merge_heads(o) * jax.nn.silu(hg.astype(jnp.float32))).astype(hq.dtype)


def even_mixer(h, w_in, w_out, lb, hg_norm_g):
    sq, sk, sv, hq, hf, hi, hg = jnp.split(h @ w_in, np.cumsum(EVEN_SPLITS)[:-1].tolist(), axis=-1)
    a_out = merge_heads(stick_breaking_attention(split_heads(sq, SB_HEADS, SB_HEAD_DIM),
                                                 split_heads(sk, SB_HEADS, SB_HEAD_DIM),
                                                 split_heads(sv, SB_HEADS, SB_HEAD_DIM)))
    b_out = hgrn2(hq, hf, hi, hg, lb, hg_norm_g)
    return jnp.concatenate([a_out.astype(h.dtype), b_out.astype(h.dtype)], axis=-1) @ w_out


def causal_depthwise_conv(x, w, b):
    s = x.shape[1]
    xp = jnp.pad(x, ((0, 0), (CONV_WIDTH - 1, 0), (0, 0)))
    y = b
    for tap in range(CONV_WIDTH):
        y = y + xp[:, CONV_WIDTH - 1 - tap:CONV_WIDTH - 1 - tap + s] * w[tap]
    return y


def block_diag_linear(x, w, b):
    bsz, s, _ = x.shape
    xb = x.reshape(bsz, s, RG_BLOCKS, RG_BLOCK_WIDTH)
    return (jnp.einsum('bsni,nio->bsno', xb, w) + b).reshape(bsz, s, LRU_WIDTH)


def rg_lru(x, wa, ba, wx, bx, lam):
    s = x.shape[1]
    r = jax.nn.sigmoid(block_diag_linear(x, wa, ba).astype(jnp.float32))
    i = jax.nn.sigmoid(block_diag_linear(x, wx, bx).astype(jnp.float32))
    log_a = -RG_C * r * jax.nn.softplus(-lam.astype(jnp.float32))
    a = jnp.exp(log_a)
    mult = jnp.sqrt(-jnp.expm1(2.0 * log_a))
    mult = jnp.where((jnp.arange(s) == 0)[None, :, None], 1.0, mult)
    u = x.astype(jnp.float32) * i * mult

    def combine(left, right):
        a1, b1 = left
        a2, b2 = right
        return a1 * a2, a2 * b1 + b2

    _, hs = lax.associative_scan(combine, (a, u), axis=1)
    return hs.astype(x.dtype)


def odd_mixer(h, w_in, conv_w, conv_b, wa, ba, wx, bx, lam, w_out):
    gate_branch, x_branch = jnp.split(h @ w_in, 2, axis=-1)
    y = rg_lru(causal_depthwise_conv(x_branch, conv_w, conv_b), wa, ba, wx, bx, lam)
    return (jax.nn.gelu(gate_branch) * y) @ w_out


def swiglu(h, w_gate_up, w_down):
    g, u = jnp.split(h @ w_gate_up, 2, axis=-1)
    return (jax.nn.silu(g) * u) @ w_down


def per_layer_embedding(h, p_i, w_up, w_gate, g):
    e = p_i @ w_up
    gate = jax.nn.sigmoid(h @ w_gate)
    return rmsnorm(gate * e, g)


def _fwd_setup_inputs(seed: int = 0) -> dict:
    key = jax.random.key(seed)
    ks = jax.random.split(key, 24)

    def nrm(i, shape, scale):
        return scale * jax.random.normal(ks[i], shape, jnp.float32)

    u = jax.random.uniform(ks[18], (N_ODD, LRU_WIDTH), jnp.float32, 0.9, 0.999)
    a0 = u ** (1.0 / RG_C)
    rg_lambda = jnp.log(a0) - jnp.log1p(-a0)
    return {
        'x': nrm(0, (BATCH, SEQ, D_MODEL), 1.0),
        'p': nrm(1, (DEPTH, BATCH, SEQ, PLE_DIM), 1.0),
        'mix_pre_g': 1.0 + nrm(2, (DEPTH, D_MODEL), 0.05),
        'mix_post_g': 1.0 + nrm(3, (DEPTH, D_MODEL), 0.05),
        'ffn_pre_g': 1.0 + nrm(4, (DEPTH, D_MODEL), 0.05),
        'ffn_post_g': 1.0 + nrm(5, (DEPTH, D_MODEL), 0.05),
        'ple_norm_g': 1.0 + nrm(6, (DEPTH, D_MODEL), 0.05),
        'w_in_even': nrm(7, (N_EVEN, D_MODEL, EVEN_IN_WIDTH), D_MODEL ** -0.5),
        'w_out_even': nrm(8, (N_EVEN, EVEN_OUT_WIDTH, D_MODEL), EVEN_OUT_WIDTH ** -0.5),
        'hg_lb_logits': nrm(9, (N_EVEN + 1, HG_KEY_WIDTH), 0.5),
        'hg_norm_g': 1.0 + nrm(10, (N_EVEN, HG_VAL_DIM), 0.05),
        'w_in_odd': nrm(11, (N_ODD, D_MODEL, 2 * LRU_WIDTH), D_MODEL ** -0.5),
        'conv_w': nrm(12, (N_ODD, CONV_WIDTH, LRU_WIDTH), CONV_WIDTH ** -0.5),
        'conv_b': nrm(13, (N_ODD, LRU_WIDTH), 0.02),
        'rg_wa': nrm(14, (N_ODD, RG_BLOCKS, RG_BLOCK_WIDTH, RG_BLOCK_WIDTH), RG_BLOCK_WIDTH ** -0.5),
        'rg_ba': nrm(15, (N_ODD, RG_BLOCKS, RG_BLOCK_WIDTH), 0.02),
        'rg_wx': nrm(16, (N_ODD, RG_BLOCKS, RG_BLOCK_WIDTH, RG_BLOCK_WIDTH), RG_BLOCK_WIDTH ** -0.5),
        'rg_bx': nrm(17, (N_ODD, RG_BLOCKS, RG_BLOCK_WIDTH), 0.02),
        'rg_lambda': rg_lambda,
        'w_out_odd': nrm(19, (N_ODD, LRU_WIDTH, D_MODEL), LRU_WIDTH ** -0.5),
        'w_gate_up': nrm(20, (DEPTH, D_MODEL, 2 * D_FF), D_MODEL ** -0.5),
        'w_down': nrm(21, (DEPTH, D_FF, D_MODEL), D_FF ** -0.5),
        'w_ple_up': nrm(22, (DEPTH, PLE_DIM, D_MODEL), PLE_DIM ** -0.5),
        'w_ple_gate': nrm(23, (DEPTH, D_MODEL, D_MODEL), D_MODEL ** -0.5),
    }


def _fwd_reference(x, p, mix_pre_g, mix_post_g, ffn_pre_g, ffn_post_g, ple_norm_g,
              w_in_even, w_out_even, hg_lb_logits, hg_norm_g,
              w_in_odd, conv_w, conv_b, rg_wa, rg_ba, rg_wx, rg_bx, rg_lambda, w_out_odd,
              w_gate_up, w_down, w_ple_up, w_ple_gate):
    lb_all = jnp.cumsum(jax.nn.softmax(hg_lb_logits.astype(jnp.float32), axis=0), axis=0)
    h = x
    for i in range(DEPTH):
        j = i // 2
        n = rmsnorm(h, mix_pre_g[i])
        if i % 2 == 0:
            m = even_mixer(n, w_in_even[j], w_out_even[j], lb_all[j], hg_norm_g[j])
        else:
            m = odd_mixer(n, w_in_odd[j], conv_w[j], conv_b[j], rg_wa[j], rg_ba[j],
                          rg_wx[j], rg_bx[j], rg_lambda[j], w_out_odd[j])
        h = h + rmsnorm(m, mix_post_g[i])
        h = h + rmsnorm(swiglu(rmsnorm(h, ffn_pre_g[i]), w_gate_up[i], w_down[i]), ffn_post_g[i])
        h = h + per_layer_embedding(h, p[i], w_ple_up[i], w_ple_gate[i], ple_norm_g[i])
    return h


import jax as _jax
import jax.numpy as _jnp

TWIN_FORMAT = 'train_step'
FWD_PARAMS = ['x', 'p', 'mix_pre_g', 'mix_post_g', 'ffn_pre_g', 'ffn_post_g', 'ple_norm_g', 'w_in_even', 'w_out_even', 'hg_lb_logits', 'hg_norm_g', 'w_in_odd', 'conv_w', 'conv_b', 'rg_wa', 'rg_ba', 'rg_wx', 'rg_bx', 'rg_lambda', 'w_out_odd', 'w_gate_up', 'w_down', 'w_ple_up', 'w_ple_gate']
TWIN_WEIGHTS = ['mix_pre_g', 'mix_post_g', 'ffn_pre_g', 'ffn_post_g', 'ple_norm_g', 'w_in_even', 'w_out_even', 'hg_lb_logits', 'hg_norm_g', 'w_in_odd', 'conv_w', 'conv_b', 'rg_wa', 'rg_ba', 'rg_wx', 'rg_bx', 'rg_lambda', 'w_out_odd', 'w_gate_up', 'w_down', 'w_ple_up', 'w_ple_gate']
TWIN_DIFF_INPUT = 'x'
TWIN_INPUTS = ['x', 'p', 'mix_pre_g', 'mix_post_g', 'ffn_pre_g', 'ffn_post_g', 'ple_norm_g', 'w_in_even', 'w_out_even', 'hg_lb_logits', 'hg_norm_g', 'w_in_odd', 'conv_w', 'conv_b', 'rg_wa', 'rg_ba', 'rg_wx', 'rg_bx', 'rg_lambda', 'w_out_odd', 'w_gate_up', 'w_down', 'w_ple_up', 'w_ple_gate', 'loss_target', 'm_mix_pre_g', 'm_mix_post_g', 'm_ffn_pre_g', 'm_ffn_post_g', 'm_ple_norm_g', 'm_w_in_even', 'm_w_out_even', 'm_hg_lb_logits', 'm_hg_norm_g', 'm_w_in_odd', 'm_conv_w', 'm_conv_b', 'm_rg_wa', 'm_rg_ba', 'm_rg_wx', 'm_rg_bx', 'm_rg_lambda', 'm_w_out_odd', 'm_w_gate_up', 'm_w_down', 'm_w_ple_up', 'm_w_ple_gate', 'v_mix_pre_g', 'v_mix_post_g', 'v_ffn_pre_g', 'v_ffn_post_g', 'v_ple_norm_g', 'v_w_in_even', 'v_w_out_even', 'v_hg_lb_logits', 'v_hg_norm_g', 'v_w_in_odd', 'v_conv_w', 'v_conv_b', 'v_rg_wa', 'v_rg_ba', 'v_rg_wx', 'v_rg_bx', 'v_rg_lambda', 'v_w_out_odd', 'v_w_gate_up', 'v_w_down', 'v_w_ple_up', 'v_w_ple_gate']
TWIN_OUTPUTS = ['loss', 'grad_x', 'grad_mix_pre_g', 'grad_mix_post_g', 'grad_ffn_pre_g', 'grad_ffn_post_g', 'grad_ple_norm_g', 'grad_w_in_even', 'grad_w_out_even', 'grad_hg_lb_logits', 'grad_hg_norm_g', 'grad_w_in_odd', 'grad_conv_w', 'grad_conv_b', 'grad_rg_wa', 'grad_rg_ba', 'grad_rg_wx', 'grad_rg_bx', 'grad_rg_lambda', 'grad_w_out_odd', 'grad_w_gate_up', 'grad_w_down', 'grad_w_ple_up', 'grad_w_ple_gate', 'delta_mix_pre_g', 'delta_mix_post_g', 'delta_ffn_pre_g', 'delta_ffn_post_g', 'delta_ple_norm_g', 'delta_w_in_even', 'delta_w_out_even', 'delta_hg_lb_logits', 'delta_hg_norm_g', 'delta_w_in_odd', 'delta_conv_w', 'delta_conv_b', 'delta_rg_wa', 'delta_rg_ba', 'delta_rg_wx', 'delta_rg_bx', 'delta_rg_lambda', 'delta_w_out_odd', 'delta_w_gate_up', 'delta_w_down', 'delta_w_ple_up', 'delta_w_ple_gate', 'new_m_mix_pre_g', 'new_m_mix_post_g', 'new_m_ffn_pre_g', 'new_m_ffn_post_g', 'new_m_ple_norm_g', 'new_m_w_in_even', 'new_m_w_out_even', 'new_m_hg_lb_logits', 'new_m_hg_norm_g', 'new_m_w_in_odd', 'new_m_conv_w', 'new_m_conv_b', 'new_m_rg_wa', 'new_m_rg_ba', 'new_m_rg_wx', 'new_m_rg_bx', 'new_m_rg_lambda', 'new_m_w_out_odd', 'new_m_w_gate_up', 'new_m_w_down', 'new_m_w_ple_up', 'new_m_w_ple_gate', 'new_v_mix_pre_g', 'new_v_mix_post_g', 'new_v_ffn_pre_g', 'new_v_ffn_post_g', 'new_v_ple_norm_g', 'new_v_w_in_even', 'new_v_w_out_even', 'new_v_hg_lb_logits', 'new_v_hg_norm_g', 'new_v_w_in_odd', 'new_v_conv_w', 'new_v_conv_b', 'new_v_rg_wa', 'new_v_rg_ba', 'new_v_rg_wx', 'new_v_rg_bx', 'new_v_rg_lambda', 'new_v_w_out_odd', 'new_v_w_gate_up', 'new_v_w_down', 'new_v_w_ple_up', 'new_v_w_ple_gate']
TWIN_LEAF_KINDS = {'loss': 'loss', 'grad_x': 'grad_x', 'grad_mix_pre_g': 'grad_w', 'grad_mix_post_g': 'grad_w', 'grad_ffn_pre_g': 'grad_w', 'grad_ffn_post_g': 'grad_w', 'grad_ple_norm_g': 'grad_w', 'grad_w_in_even': 'grad_w', 'grad_w_out_even': 'grad_w', 'grad_hg_lb_logits': 'grad_w', 'grad_hg_norm_g': 'grad_w', 'grad_w_in_odd': 'grad_w', 'grad_conv_w': 'grad_w', 'grad_conv_b': 'grad_w', 'grad_rg_wa': 'grad_w', 'grad_rg_ba': 'grad_w', 'grad_rg_wx': 'grad_w', 'grad_rg_bx': 'grad_w', 'grad_rg_lambda': 'grad_w', 'grad_w_out_odd': 'grad_w', 'grad_w_gate_up': 'grad_w', 'grad_w_down': 'grad_w', 'grad_w_ple_up': 'grad_w', 'grad_w_ple_gate': 'grad_w', 'delta_mix_pre_g': 'delta_w', 'delta_mix_post_g': 'delta_w', 'delta_ffn_pre_g': 'delta_w', 'delta_ffn_post_g': 'delta_w', 'delta_ple_norm_g': 'delta_w', 'delta_w_in_even': 'delta_w', 'delta_w_out_even': 'delta_w', 'delta_hg_lb_logits': 'delta_w', 'delta_hg_norm_g': 'delta_w', 'delta_w_in_odd': 'delta_w', 'delta_conv_w': 'delta_w', 'delta_conv_b': 'delta_w', 'delta_rg_wa': 'delta_w', 'delta_rg_ba': 'delta_w', 'delta_rg_wx': 'delta_w', 'delta_rg_bx': 'delta_w', 'delta_rg_lambda': 'delta_w', 'delta_w_out_odd': 'delta_w', 'delta_w_gate_up': 'delta_w', 'delta_w_down': 'delta_w', 'delta_w_ple_up': 'delta_w', 'delta_w_ple_gate': 'delta_w', 'new_m_mix_pre_g': 'new_m', 'new_m_mix_post_g': 'new_m', 'new_m_ffn_pre_g': 'new_m', 'new_m_ffn_post_g': 'new_m', 'new_m_ple_norm_g': 'new_m', 'new_m_w_in_even': 'new_m', 'new_m_w_out_even': 'new_m', 'new_m_hg_lb_logits': 'new_m', 'new_m_hg_norm_g': 'new_m', 'new_m_w_in_odd': 'new_m', 'new_m_conv_w': 'new_m', 'new_m_conv_b': 'new_m', 'new_m_rg_wa': 'new_m', 'new_m_rg_ba': 'new_m', 'new_m_rg_wx': 'new_m', 'new_m_rg_bx': 'new_m', 'new_m_rg_lambda': 'new_m', 'new_m_w_out_odd': 'new_m', 'new_m_w_gate_up': 'new_m', 'new_m_w_down': 'new_m', 'new_m_w_ple_up': 'new_m', 'new_m_w_ple_gate': 'new_m', 'new_v_mix_pre_g': 'new_v', 'new_v_mix_post_g': 'new_v', 'new_v_ffn_pre_g': 'new_v', 'new_v_ffn_post_g': 'new_v', 'new_v_ple_norm_g': 'new_v', 'new_v_w_in_even': 'new_v', 'new_v_w_out_even': 'new_v', 'new_v_hg_lb_logits': 'new_v', 'new_v_hg_norm_g': 'new_v', 'new_v_w_in_odd': 'new_v', 'new_v_conv_w': 'new_v', 'new_v_conv_b': 'new_v', 'new_v_rg_wa': 'new_v', 'new_v_rg_ba': 'new_v', 'new_v_rg_wx': 'new_v', 'new_v_rg_bx': 'new_v', 'new_v_rg_lambda': 'new_v', 'new_v_w_out_odd': 'new_v', 'new_v_w_gate_up': 'new_v', 'new_v_w_down': 'new_v', 'new_v_w_ple_up': 'new_v', 'new_v_w_ple_gate': 'new_v'}


def _forward(args):
    return _fwd_reference(*[args[k] for k in FWD_PARAMS])


def _output_shape():
    out = _jax.eval_shape(lambda: _forward(_fwd_setup_inputs(0)))
    return out.shape, out.dtype

N_MICROBATCH = 1
ADAM_LR = 0.001
ADAM_B1 = 0.9
ADAM_B2 = 0.999
ADAM_EPS = 1e-08
ADAM_WD = 0.01
ADAM_STEP = 10
PER_EXAMPLE_BATCH_AXIS = {'x': 0, 'p': 1, 'loss_target': 0}
SHARED_INPUTS = []
_WEIGHT_DTYPES = {'mix_pre_g': _jnp.float32, 'mix_post_g': _jnp.float32, 'ffn_pre_g': _jnp.float32, 'ffn_post_g': _jnp.float32, 'ple_norm_g': _jnp.float32, 'w_in_even': _jnp.float32, 'w_out_even': _jnp.float32, 'hg_lb_logits': _jnp.float32, 'hg_norm_g': _jnp.float32, 'w_in_odd': _jnp.float32, 'conv_w': _jnp.float32, 'conv_b': _jnp.float32, 'rg_wa': _jnp.float32, 'rg_ba': _jnp.float32, 'rg_wx': _jnp.float32, 'rg_bx': _jnp.float32, 'rg_lambda': _jnp.float32, 'w_out_odd': _jnp.float32, 'w_gate_up': _jnp.float32, 'w_down': _jnp.float32, 'w_ple_up': _jnp.float32, 'w_ple_gate': _jnp.float32}
MOMENT_SCALE = {'mix_pre_g': 4.175227e-01, 'mix_post_g': 7.963542e+00, 'ffn_pre_g': 3.547129e-01, 'ffn_post_g': 8.014914e+00, 'ple_norm_g': 8.019339e+00, 'w_in_even': 2.705541e-01, 'w_out_even': 3.900434e-01, 'hg_lb_logits': 3.207491e-02, 'hg_norm_g': 1.220991e+00, 'w_in_odd': 1.874703e-01, 'conv_w': 1.898431e-01, 'conv_b': 3.172656e+00, 'rg_wa': 5.651914e-02, 'rg_ba': 5.183318e-02, 'rg_wx': 1.012678e-01, 'rg_bx': 7.090421e-02, 'rg_lambda': 1.027338e-01, 'w_out_odd': 2.339016e-01, 'w_gate_up': 1.427366e-01, 'w_down': 2.430211e-01, 'w_ple_up': 2.208053e-01, 'w_ple_gate': 1.211421e-01}


def _to_microbatches(a, axis):
    t = _jnp.moveaxis(a, axis, 0)
    t = t.reshape((N_MICROBATCH, t.shape[0] // N_MICROBATCH) + t.shape[1:])
    return _jnp.moveaxis(t, 1, axis + 1)


def setup_inputs(seed: int = 0) -> dict:
    inp = _fwd_setup_inputs(seed)
    key = _jax.random.fold_in(_jax.random.key(seed), 7919)
    shape, _ = _output_shape()
    out = dict(inp)
    out["loss_target"] = _jax.random.normal(_jax.random.fold_in(key, 0), shape, _jnp.float32)
    for i, name in enumerate(TWIN_WEIGHTS):
        w = inp[name].astype(_jnp.float32)
        if MOMENT_SCALE is None:
            s = _jnp.sqrt(_jnp.mean(_jnp.square(w)) + 1e-30)
        else:
            s = MOMENT_SCALE[name]
        km, kv = _jax.random.split(_jax.random.fold_in(key, i + 1))
        out[name] = w
        out["m_" + name] = s * _jax.random.normal(km, w.shape, _jnp.float32)
        out["v_" + name] = (s * s) * _jax.random.uniform(kv, w.shape, _jnp.float32, 0.5, 1.5)
    if N_MICROBATCH > 1:
        for name, axis in PER_EXAMPLE_BATCH_AXIS.items():
            out[name] = _to_microbatches(out[name], axis)
    return {'x': out['x'], 'p': out['p'], 'mix_pre_g': out['mix_pre_g'], 'mix_post_g': out['mix_post_g'], 'ffn_pre_g': out['ffn_pre_g'], 'ffn_post_g': out['ffn_post_g'], 'ple_norm_g': out['ple_norm_g'], 'w_in_even': out['w_in_even'], 'w_out_even': out['w_out_even'], 'hg_lb_logits': out['hg_lb_logits'], 'hg_norm_g': out['hg_norm_g'], 'w_in_odd': out['w_in_odd'], 'conv_w': out['conv_w'], 'conv_b': out['conv_b'], 'rg_wa': out['rg_wa'], 'rg_ba': out['rg_ba'], 'rg_wx': out['rg_wx'], 'rg_bx': out['rg_bx'], 'rg_lambda': out['rg_lambda'], 'w_out_odd': out['w_out_odd'], 'w_gate_up': out['w_gate_up'], 'w_down': out['w_down'], 'w_ple_up': out['w_ple_up'], 'w_ple_gate': out['w_ple_gate'], 'loss_target': out['loss_target'], 'm_mix_pre_g': out['m_mix_pre_g'], 'm_mix_post_g': out['m_mix_post_g'], 'm_ffn_pre_g': out['m_ffn_pre_g'], 'm_ffn_post_g': out['m_ffn_post_g'], 'm_ple_norm_g': out['m_ple_norm_g'], 'm_w_in_even': out['m_w_in_even'], 'm_w_out_even': out['m_w_out_even'], 'm_hg_lb_logits': out['m_hg_lb_logits'], 'm_hg_norm_g': out['m_hg_norm_g'], 'm_w_in_odd': out['m_w_in_odd'], 'm_conv_w': out['m_conv_w'], 'm_conv_b': out['m_conv_b'], 'm_rg_wa': out['m_rg_wa'], 'm_rg_ba': out['m_rg_ba'], 'm_rg_wx': out['m_rg_wx'], 'm_rg_bx': out['m_rg_bx'], 'm_rg_lambda': out['m_rg_lambda'], 'm_w_out_odd': out['m_w_out_odd'], 'm_w_gate_up': out['m_w_gate_up'], 'm_w_down': out['m_w_down'], 'm_w_ple_up': out['m_w_ple_up'], 'm_w_ple_gate': out['m_w_ple_gate'], 'v_mix_pre_g': out['v_mix_pre_g'], 'v_mix_post_g': out['v_mix_post_g'], 'v_ffn_pre_g': out['v_ffn_pre_g'], 'v_ffn_post_g': out['v_ffn_post_g'], 'v_ple_norm_g': out['v_ple_norm_g'], 'v_w_in_even': out['v_w_in_even'], 'v_w_out_even': out['v_w_out_even'], 'v_hg_lb_logits': out['v_hg_lb_logits'], 'v_hg_norm_g': out['v_hg_norm_g'], 'v_w_in_odd': out['v_w_in_odd'], 'v_conv_w': out['v_conv_w'], 'v_conv_b': out['v_conv_b'], 'v_rg_wa': out['v_rg_wa'], 'v_rg_ba': out['v_rg_ba'], 'v_rg_wx': out['v_rg_wx'], 'v_rg_bx': out['v_rg_bx'], 'v_rg_lambda': out['v_rg_lambda'], 'v_w_out_odd': out['v_w_out_odd'], 'v_w_gate_up': out['v_w_gate_up'], 'v_w_down': out['v_w_down'], 'v_w_ple_up': out['v_w_ple_up'], 'v_w_ple_gate': out['v_w_ple_gate']}


def _loss(weights, diff, rest, loss_target):
    with _jax.named_scope("forward"):
        args = {**rest, TWIN_DIFF_INPUT: diff, **{k: w.astype(_WEIGHT_DTYPES[k]) for k, w in weights.items()}}
        y = _forward(args)
    with _jax.named_scope("loss_head"):
        err = _jnp.square(y.astype(_jnp.float32) - loss_target)
        return 0.5 * _jnp.sum(_jnp.mean(err, axis=-1)) if err.ndim else 0.5 * err


def _adamw(w, g, m, v):
    m = ADAM_B1 * m + (1.0 - ADAM_B1) * g
    v = ADAM_B2 * v + (1.0 - ADAM_B2) * _jnp.square(g)
    m_hat = m / (1.0 - ADAM_B1 ** ADAM_STEP)
    v_hat = v / (1.0 - ADAM_B2 ** ADAM_STEP)
    delta = -ADAM_LR * (m_hat / (_jnp.sqrt(v_hat) + ADAM_EPS) + ADAM_WD * w)
    return delta, m, v


def reference(x, p, mix_pre_g, mix_post_g, ffn_pre_g, ffn_post_g, ple_norm_g, w_in_even, w_out_even, hg_lb_logits, hg_norm_g, w_in_odd, conv_w, conv_b, rg_wa, rg_ba, rg_wx, rg_bx, rg_lambda, w_out_odd, w_gate_up, w_down, w_ple_up, w_ple_gate, loss_target, m_mix_pre_g, m_mix_post_g, m_ffn_pre_g, m_ffn_post_g, m_ple_norm_g, m_w_in_even, m_w_out_even, m_hg_lb_logits, m_hg_norm_g, m_w_in_odd, m_conv_w, m_conv_b, m_rg_wa, m_rg_ba, m_rg_wx, m_rg_bx, m_rg_lambda, m_w_out_odd, m_w_gate_up, m_w_down, m_w_ple_up, m_w_ple_gate, v_mix_pre_g, v_mix_post_g, v_ffn_pre_g, v_ffn_post_g, v_ple_norm_g, v_w_in_even, v_w_out_even, v_hg_lb_logits, v_hg_norm_g, v_w_in_odd, v_conv_w, v_conv_b, v_rg_wa, v_rg_ba, v_rg_wx, v_rg_bx, v_rg_lambda, v_w_out_odd, v_w_gate_up, v_w_down, v_w_ple_up, v_w_ple_gate):
    given = dict(x=x, p=p, mix_pre_g=mix_pre_g, mix_post_g=mix_post_g, ffn_pre_g=ffn_pre_g, ffn_post_g=ffn_post_g, ple_norm_g=ple_norm_g, w_in_even=w_in_even, w_out_even=w_out_even, hg_lb_logits=hg_lb_logits, hg_norm_g=hg_norm_g, w_in_odd=w_in_odd, conv_w=conv_w, conv_b=conv_b, rg_wa=rg_wa, rg_ba=rg_ba, rg_wx=rg_wx, rg_bx=rg_bx, rg_lambda=rg_lambda, w_out_odd=w_out_odd, w_gate_up=w_gate_up, w_down=w_down, w_ple_up=w_ple_up, w_ple_gate=w_ple_gate, loss_target=loss_target, m_mix_pre_g=m_mix_pre_g, m_mix_post_g=m_mix_post_g, m_ffn_pre_g=m_ffn_pre_g, m_ffn_post_g=m_ffn_post_g, m_ple_norm_g=m_ple_norm_g, m_w_in_even=m_w_in_even, m_w_out_even=m_w_out_even, m_hg_lb_logits=m_hg_lb_logits, m_hg_norm_g=m_hg_norm_g, m_w_in_odd=m_w_in_odd, m_conv_w=m_conv_w, m_conv_b=m_conv_b, m_rg_wa=m_rg_wa, m_rg_ba=m_rg_ba, m_rg_wx=m_rg_wx, m_rg_bx=m_rg_bx, m_rg_lambda=m_rg_lambda, m_w_out_odd=m_w_out_odd, m_w_gate_up=m_w_gate_up, m_w_down=m_w_down, m_w_ple_up=m_w_ple_up, m_w_ple_gate=m_w_ple_gate, v_mix_pre_g=v_mix_pre_g, v_mix_post_g=v_mix_post_g, v_ffn_pre_g=v_ffn_pre_g, v_ffn_post_g=v_ffn_post_g, v_ple_norm_g=v_ple_norm_g, v_w_in_even=v_w_in_even, v_w_out_even=v_w_out_even, v_hg_lb_logits=v_hg_lb_logits, v_hg_norm_g=v_hg_norm_g, v_w_in_odd=v_w_in_odd, v_conv_w=v_conv_w, v_conv_b=v_conv_b, v_rg_wa=v_rg_wa, v_rg_ba=v_rg_ba, v_rg_wx=v_rg_wx, v_rg_bx=v_rg_bx, v_rg_lambda=v_rg_lambda, v_w_out_odd=v_w_out_odd, v_w_gate_up=v_w_gate_up, v_w_down=v_w_down, v_w_ple_up=v_w_ple_up, v_w_ple_gate=v_w_ple_gate)
    weights = {n: given[n] for n in TWIN_WEIGHTS}
    shared = {n: given[n] for n in SHARED_INPUTS}
    per_example = {n: given[n] for n in ['x', 'p']}
    grad_fn = _jax.value_and_grad(_loss, argnums=(0, 1))

    def one_microbatch(ex, loss_target):
        ex = dict(ex)
        diff = ex.pop(TWIN_DIFF_INPUT)
        return grad_fn(weights, diff, {**shared, **ex}, loss_target)

    if N_MICROBATCH == 1:
        loss, (grad_w, grad_x) = one_microbatch(per_example, given["loss_target"])
    else:
        def body(carry, xs):
            loss_sum, grad_sum = carry
            l_k, (gw_k, gx_k) = one_microbatch(xs[0], xs[1])
            with _jax.named_scope("update"):
                return (loss_sum + l_k, _jax.tree.map(_jnp.add, grad_sum, gw_k)), gx_k

        init = (_jnp.zeros((), _jnp.float32), _jax.tree.map(_jnp.zeros_like, weights))
        (loss, grad_w), grad_x = _jax.lax.scan(body, init, (per_example, given["loss_target"]))
    with _jax.named_scope("update"):
        delta_w, new_m, new_v = {}, {}, {}
        for n in TWIN_WEIGHTS:
            delta_w[n], new_m[n], new_v[n] = _adamw(weights[n], grad_w[n], given["m_" + n], given["v_" + n])
    return (loss, grad_x, *[grad_w[n] for n in TWIN_WEIGHTS], *[delta_w[n] for n in TWIN_WEIGHTS],
            *[new_m[n] for n in TWIN_WEIGHTS], *[new_v[n] for n in TWIN_WEIGHTS])
```

```python
import functools

import jax
import jax.numpy as jnp
from jax import lax
from jax.experimental import pallas as pl
from jax.experimental.pallas import tpu as pltpu

f32 = jnp.float32
bf16 = jnp.bfloat16
HIGHEST = lax.Precision.HIGHEST
MESH = pl.DeviceIdType.MESH

LANES = 128
SUBLANES = 8
MXU_WIDTH = 256
VMEM_LIMIT_BYTES = 56 * 1024 * 1024

RMS_EPS = 1e-6
HEADS = 8
HEAD_DIM = 128
SB_BLOCK = 128
HG_SUB = 16
HG_CHUNK = 64
RG_BLOCK = 256
CONV_WIDTH = 4
RG_C = 8.0
ADAM_LR, ADAM_B1, ADAM_B2, ADAM_EPS, ADAM_WD, ADAM_STEP = 0.001, 0.9, 0.999, 1e-08, 0.01, 10
N_CHIPS = 4

NN = (((1,), (0,)), ((), ()))
NT = (((1,), (1,)), ((), ()))
TN = (((0,), (0,)), ((), ()))


def _params(*sem):
    return pltpu.CompilerParams(dimension_semantics=sem, vmem_limit_bytes=VMEM_LIMIT_BYTES)


def _pick(dim, target, align=LANES):
    if dim <= target:
        return dim
    t = (target // align) * align
    while t >= align:
        if dim % t == 0:
            return t
        t -= align
    return dim


def _mm(a, b, mode, out_dtype, name, *, b_shards=False, out_shards=False, tm=1024, tn=512, tk=512):
    if mode == "nn":
        m, kc = a.shape
        n = b.shape[-1] * (N_CHIPS if b_shards else 1)
    elif mode == "nt":
        m, kc = a.shape
        n = b.shape[-2]
    else:
        kc, m = a.shape
        n = b.shape[-1]
    n_sh = n // N_CHIPS
    kc_sh = kc // N_CHIPS
    tm = _pick(m, tm)
    n_unit = n_sh if (out_shards or (b_shards and mode == "nn")) else n
    k_unit = kc_sh if (b_shards and mode == "nt") else kc
    tn = _pick(n_unit, tn)
    tk = _pick(k_unit, tk)
    tn = n_unit if tn < MXU_WIDTH else tn
    tk = k_unit if tk < MXU_WIDTH else tk
    nk = kc // tk
    dims = {"nn": NN, "nt": NT, "tn": TN}[mode]

    def body(a_ref, b_ref, o_ref, acc_ref):
        k = pl.program_id(2)

        @pl.when(k == 0)
        def _():
            acc_ref[...] = jnp.zeros_like(acc_ref)

        acc_ref[...] += lax.dot_general(a_ref[...], b_ref[...], dims, preferred_element_type=f32)

        @pl.when(k == nk - 1)
        def _():
            o_ref[...] = acc_ref[...].astype(o_ref.dtype)

    if mode == "tn":
        a_spec = pl.BlockSpec((tk, tm), lambda i, j, k: (k, i))
    else:
        a_spec = pl.BlockSpec((tm, tk), lambda i, j, k: (i, k))
    nb_sh = n_sh // tn
    if mode == "nn":
        if b_shards:
            b_spec = pl.BlockSpec((None, tk, tn), lambda i, j, k: (j // nb_sh, k, j % nb_sh))
        else:
            b_spec = pl.BlockSpec((tk, tn), lambda i, j, k: (k, j))
    elif mode == "nt":
        if b_shards:
            kb_sh = kc_sh // tk
            b_spec = pl.BlockSpec((None, tn, tk), lambda i, j, k: (k // kb_sh, j, k % kb_sh))
        else:
            b_spec = pl.BlockSpec((tn, tk), lambda i, j, k: (j, k))
    else:
        b_spec = pl.BlockSpec((tk, tn), lambda i, j, k: (k, j))
    if out_shards:
        out_shape = jax.ShapeDtypeStruct((N_CHIPS, m, n_sh), out_dtype)
        o_spec = pl.BlockSpec((None, tm, tn), lambda i, j, k: (j // nb_sh, i, j % nb_sh))
    else:
        out_shape = jax.ShapeDtypeStruct((m, n), out_dtype)
        o_spec = pl.BlockSpec((tm, tn), lambda i, j, k: (i, j))
    return pl.pallas_call(
        body, name=name, out_shape=out_shape, grid=(m // tm, n // tn, nk),
        in_specs=[a_spec, b_spec], out_specs=o_spec,
        scratch_shapes=[pltpu.VMEM((tm, tn), f32)],
        compiler_params=_params("parallel", "parallel", "arbitrary"),
    )(a, b)


def _row_spec(entry, tr):
    if isinstance(entry, tuple):
        arr, width, cb = entry[:3]
        rb = (entry[3] // tr) if len(entry) > 3 else 0
        return arr, pl.BlockSpec((tr, width), lambda i, cb=cb, rb=rb: (i + rb, cb))
    return entry, pl.BlockSpec((tr, entry.shape[1]), lambda i: (i, 0))


def _whole_spec(arr):
    return pl.BlockSpec(arr.shape, lambda i: (0,) * arr.ndim)


def _rowwise(fn, rows, bcast, outs, name, tr, row_ids=False, m=None):
    arrs, specs = zip(*[_row_spec(e, tr) for e in rows])
    m = arrs[0].shape[0] if m is None else m
    nr, nb = len(rows), len(bcast)

    def body(*refs):
        vals = [r[...] for r in refs[:nr + nb]]
        if row_ids:
            rid = pl.program_id(0) * tr + lax.broadcasted_iota(jnp.int32, (tr, 1), 0)
            res = fn(rid, *vals)
        else:
            res = fn(*vals)
        for r, o in zip(res, refs[nr + nb:]):
            o[...] = r.astype(o.dtype)

    return pl.pallas_call(
        body, name=name, grid=(m // tr,),
        out_shape=[jax.ShapeDtypeStruct((m, w), dt) for w, dt in outs],
        in_specs=list(specs) + [_whole_spec(b) for b in bcast],
        out_specs=[pl.BlockSpec((tr, w), lambda i: (i, 0)) for w, _ in outs],
        compiler_params=_params("parallel"),
    )(*arrs, *bcast)


def _rowwise_vjp(fn, rows, bcast, cts, row_grads, name, tr, row_ids=False, concat=False):
    arrs, specs = zip(*[_row_spec(e, tr) for e in rows])
    ct_arrs, ct_specs = zip(*[_row_spec(e, tr) for e in cts])
    m = arrs[0].shape[0]
    nr, nb, nc, ng = len(rows), len(bcast), len(cts), len(row_grads)
    widths = [specs[idx].block_shape[1] for idx, _ in row_grads]

    def body(*refs):
        vals = [r[...] for r in refs[:nr + nb]]
        ctv = [r[...] for r in refs[nr + nb:nr + nb + nc]]
        out_refs = refs[nr + nb + nc:]
        if row_ids:
            rid = pl.program_id(0) * tr + lax.broadcasted_iota(jnp.int32, (tr, 1), 0)
            f = functools.partial(fn, rid)
        else:
            f = fn
        res, vjp = jax.vjp(f, *vals)
        grads = vjp(tuple(c.astype(r.dtype) for c, r in zip(ctv, res)))
        if concat:
            off = 0
            for (idx, _), w in zip(row_grads, widths):
                out_refs[0][:, off:off + w] = grads[idx].astype(out_refs[0].dtype)
                off += w
            b_refs = out_refs[1:]
        else:
            for (idx, _), o in zip(row_grads, out_refs[:ng]):
                o[...] = grads[idx].astype(o.dtype)
            b_refs = out_refs[ng:]

        @pl.when(pl.program_id(0) == 0)
        def _():
            for o in b_refs:
                o[...] = jnp.zeros_like(o)

        for j, o in enumerate(b_refs):
            o[...] += grads[nr + j].astype(f32)

    if concat:
        dt = row_grads[0][1]
        g_shapes = [jax.ShapeDtypeStruct((m, sum(widths)), dt)]
        g_specs = [pl.BlockSpec((tr, sum(widths)), lambda i: (i, 0))]
    else:
        g_shapes = [jax.ShapeDtypeStruct((m, w), dt) for (_, dt), w in zip(row_grads, widths)]
        g_specs = [pl.BlockSpec((tr, w), lambda i: (i, 0)) for w in widths]
    res = pl.pallas_call(
        body, name=name, grid=(m // tr,),
        out_shape=g_shapes + [jax.ShapeDtypeStruct(b.shape, f32) for b in bcast],
        in_specs=list(specs) + [_whole_spec(b) for b in bcast] + list(ct_specs),
        out_specs=g_specs + [_whole_spec(b) for b in bcast],
        compiler_params=_params("arbitrary"),
    )(*arrs, *bcast, *ct_arrs)
    n_g = len(g_shapes)
    return list(res[:n_g]), list(res[n_g:])


def _rms(x, g):
    return x * lax.rsqrt(jnp.mean(x * x, axis=-1, keepdims=True) + RMS_EPS) * g


def _sigmoid(x):
    return 1.0 / (1.0 + jnp.exp(-x))


def _silu(x):
    return x * _sigmoid(x)


def _gelu_tanh(x):
    return 0.5 * x * (1.0 + jnp.tanh(0.7978845608028654 * (x + 0.044715 * (x * x * x))))


def _softplus(x):
    return jnp.maximum(x, 0.0) + jnp.log(1.0 + jnp.exp(-jnp.abs(x)))


def _expm1(x):
    small = jnp.abs(x) < 0.01
    series = x * (1.0 + x * (0.5 + x * (1.0 / 6.0 + x * (1.0 / 24.0))))
    return jnp.where(small, series, jnp.exp(jnp.where(small, 0.0, x)) - 1.0)


def _f_pre(h, g):
    return (_rms(h, g),)


def _f_pre_id(h, g):
    return h, _rms(h, g)


def _f_post_pre(h, m, g_post, g_pre):
    hb = h + _rms(m, g_post)
    return hb, _rms(hb, g_pre)


def _f_post(h, m, g_post):
    hc = h + _rms(m, g_post)
    return hc, hc


def _f_swiglu(g, u):
    return (_silu(g) * u,)


def _f_ple_pre(h, t, e, g_ple, g_pre):
    hd = h + _rms(_sigmoid(t) * e, g_ple)
    return hd, _rms(hd, g_pre)


def _f_ple(h, t, e, g_ple):
    return (h + _rms(_sigmoid(t) * e, g_ple),)


def _f_rg_gate(rid, ra, ix, xc, ba, bx, lam):
    r = _sigmoid(ra + ba)
    i = _sigmoid(ix + bx)
    log_a = -RG_C * r * _softplus(-lam)
    a = jnp.exp(log_a)
    mult = jnp.sqrt(-_expm1(2.0 * log_a))
    mult = jnp.where(rid == 0, 1.0, mult)
    return a, xc * i * mult


def _f_rg_out(gate, hs):
    return (_gelu_tanh(gate) * hs,)


def _loss_head(y, target, tr=256):
    m, d = y.shape

    def body(y_ref, t_ref, dy_ref, l_ref):
        err = y_ref[...] - t_ref[...]
        dy_ref[...] = err * (1.0 / d)

        @pl.when(pl.program_id(0) == 0)
        def _():
            l_ref[...] = jnp.zeros_like(l_ref)

        part = 0.5 * jnp.sum(jnp.mean(err * err, axis=-1, keepdims=True), axis=0, keepdims=True)
        l_ref[...] += jnp.broadcast_to(part, l_ref.shape)

    dy, l = pl.pallas_call(
        body, name="loss_head", grid=(m // tr,),
        out_shape=[jax.ShapeDtypeStruct((m, d), f32), jax.ShapeDtypeStruct((SUBLANES, LANES), f32)],
        in_specs=[pl.BlockSpec((tr, d), lambda i: (i, 0))] * 2,
        out_specs=[pl.BlockSpec((tr, d), lambda i: (i, 0)), pl.BlockSpec((SUBLANES, LANES), lambda i: (0, 0))],
        compiler_params=_params("arbitrary"),
    )(y, target)
    return dy, l[0, 0]


def _hilo_dot(a, u):
    hi = a.astype(bf16)
    lo = (a - hi.astype(f32)).astype(bf16)
    return (lax.dot_general(hi, u, NN, preferred_element_type=f32)
            + lax.dot_general(lo, u, NN, preferred_element_type=f32))


def _sb_scores(q_t, k_s, tri, diag):
    z = lax.dot_general(q_t, k_s, NT, preferred_element_type=f32)
    sp = jnp.log(1.0 + jnp.exp(-jnp.abs(z)))
    a = -(jnp.maximum(z, 0.0) + sp)
    b = a + z
    if diag:
        a = jnp.where(tri, a, 0.0)
    return a, b


def _sb_masks():
    row = lax.broadcasted_iota(jnp.int32, (SB_BLOCK, SB_BLOCK), 0)
    col = lax.broadcasted_iota(jnp.int32, (SB_BLOCK, SB_BLOCK), 1)
    tri = col < row
    u_excl = (row > col).astype(bf16)
    u_incl = (row >= col).astype(bf16)
    return tri, u_excl, u_incl


def _sb_fwd(proj):
    s = proj.shape[0]
    nblk = s // SB_BLOCK
    scale = HEAD_DIM ** -0.5

    def body(q_ref, k_ref, v_ref, o_ref, qs, ks, vs):
        qs[...] = (q_ref[...] * scale).astype(bf16)
        ks[...] = k_ref[...].astype(bf16)
        vs[...] = v_ref[...].astype(bf16)
        tri, u_excl, _ = _sb_masks()

        def block(q_t, s_idx, carry, diag):
            ra, acc = carry
            rows = pl.ds(pl.multiple_of(s_idx * SB_BLOCK, SB_BLOCK), SB_BLOCK)
            a, b = _sb_scores(q_t, ks[rows, :], tri, diag)
            w = jnp.exp(b + _hilo_dot(a, u_excl) + ra)
            if diag:
                w = jnp.where(tri, w, 0.0)
            acc = acc + lax.dot_general(w.astype(bf16), vs[rows, :], NN, preferred_element_type=f32)
            return ra + jnp.sum(a, axis=1, keepdims=True), acc

        def query_block(t, _):
            rows = pl.ds(pl.multiple_of(t * SB_BLOCK, SB_BLOCK), SB_BLOCK)
            q_t = qs[rows, :]
            init = (jnp.zeros((SB_BLOCK, 1), f32), jnp.zeros((SB_BLOCK, HEAD_DIM), f32))
            carry = block(q_t, t, init, True)
            carry = lax.fori_loop(0, t, lambda i, c: block(q_t, t - 1 - i, c, False), carry)
            o_ref[rows, :] = carry[1]
            return 0

        lax.fori_loop(0, nblk, query_block, 0)

    col = lambda off: pl.BlockSpec((s, HEAD_DIM), lambda h: (0, off + h))
    return pl.pallas_call(
        body, name="sb_fwd", grid=(HEADS,),
        out_shape=jax.ShapeDtypeStruct((s, HEADS * HEAD_DIM), f32),
        in_specs=[col(0), col(HEADS), col(2 * HEADS)],
        out_specs=col(0),
        scratch_shapes=[pltpu.VMEM((s, HEAD_DIM), bf16)] * 3,
        compiler_params=_params("parallel"),
    )(proj, proj, proj)


def _sb_bwd(proj, o, do):
    s = proj.shape[0]
    nblk = s // SB_BLOCK
    scale = HEAD_DIM ** -0.5

    def body(q_ref, k_ref, v_ref, o_ref, do_ref, dq_ref, dk_ref, dv_ref, qs, ks, vs, dos, dk_acc, dv_acc):
        qs[...] = (q_ref[...] * scale).astype(bf16)
        ks[...] = k_ref[...].astype(bf16)
        vs[...] = v_ref[...].astype(bf16)
        dos[...] = do_ref[...].astype(bf16)
        dk_acc[...] = jnp.zeros_like(dk_acc)
        dv_acc[...] = jnp.zeros_like(dv_acc)
        tri, u_excl, u_incl = _sb_masks()

        def block(q_t, do_t, d_t, s_idx, carry, diag):
            ra, rg, dq = carry
            rows = pl.ds(pl.multiple_of(s_idx * SB_BLOCK, SB_BLOCK), SB_BLOCK)
            k_s = ks[rows, :]
            v_s = vs[rows, :]
            a, b = _sb_scores(q_t, k_s, tri, diag)
            w = jnp.exp(b + _hilo_dot(a, u_excl) + ra)
            if diag:
                w = jnp.where(tri, w, 0.0)
            dw = lax.dot_general(do_t, v_s, NT, preferred_element_type=f32)
            wb = w.astype(bf16)
            g = wb.astype(f32) * dw
            da = d_t - (_hilo_dot(g, u_incl) + rg)
            dz = g * jnp.exp(a) - da * jnp.exp(b)
            if diag:
                dz = jnp.where(tri, dz, 0.0)
            dzb = dz.astype(bf16)
            dq = dq + lax.dot_general(dzb, k_s, NN, preferred_element_type=f32)
            dk_acc[rows, :] += lax.dot_general(dzb, q_t, TN, preferred_element_type=f32)
            dv_acc[rows, :] += lax.dot_general(wb, do_t, TN, preferred_element_type=f32)
            return ra + jnp.sum(a, axis=1, keepdims=True), rg + jnp.sum(g, axis=1, keepdims=True), dq

        def query_block(t, _):
            rows = pl.ds(pl.multiple_of(t * SB_BLOCK, SB_BLOCK), SB_BLOCK)
            q_t = qs[rows, :]
            do_t = dos[rows, :]
            d_t = jnp.sum(do_t.astype(f32) * o_ref[rows, :], axis=1, keepdims=True)
            zero = jnp.zeros((SB_BLOCK, 1), f32)
            carry = block(q_t, do_t, d_t, t, (zero, zero, jnp.zeros((SB_BLOCK, HEAD_DIM), f32)), True)
            carry = lax.fori_loop(0, t, lambda i, c: block(q_t, do_t, d_t, t - 1 - i, c, False), carry)
            dq_ref[rows, :] = (carry[2] * scale).astype(dq_ref.dtype)
            return 0

        lax.fori_loop(0, nblk, query_block, 0)
        dk_ref[...] = dk_acc[...].astype(dk_ref.dtype)
        dv_ref[...] = dv_acc[...].astype(dv_ref.dtype)

    col = lambda off: pl.BlockSpec((s, HEAD_DIM), lambda h: (0, off + h))
    out = jax.ShapeDtypeStruct((s, HEADS * HEAD_DIM), bf16)
    return pl.pallas_call(
        body, name="sb_bwd", grid=(HEADS,),
        out_shape=[out, out, out],
        in_specs=[col(0), col(HEADS), col(2 * HEADS), col(0), col(0)],
        out_specs=[col(0), col(0), col(0)],
        scratch_shapes=[pltpu.VMEM((s, HEAD_DIM), bf16)] * 4 + [pltpu.VMEM((s, HEAD_DIM), f32)] * 2,
        compiler_params=_params("parallel"),
    )(proj, proj, proj, o, do)


def _hg_sub(state_t, hq, hf, hi, lb):
    rid = lax.broadcasted_iota(jnp.int32, (HG_SUB, 1), 0)
    f = lb + (1.0 - lb) * _sigmoid(hf)
    q = _silu(hq)
    k = 1.0 - f
    lf = jnp.log(f)
    r2 = lax.broadcasted_iota(jnp.int32, (HG_SUB, HG_SUB), 0)
    c2 = lax.broadcasted_iota(jnp.int32, (HG_SUB, HG_SUB), 1)
    cum = lax.dot_general((c2 <= r2).astype(f32), lf, NN, precision=HIGHEST, preferred_element_type=f32)
    o = lax.dot_general(q * jnp.exp(cum), state_t, NT, precision=HIGHEST, preferred_element_type=f32)
    for s in range(HG_SUB):
        pick = rid == s
        row = lambda x: jnp.sum(jnp.where(pick, x, 0.0), axis=0, keepdims=True)
        later = rid >= s
        dec = jnp.exp(jnp.where(later, cum - row(cum), 0.0))
        score = jnp.sum(jnp.where(later, q * row(k) * dec, 0.0), axis=1, keepdims=True)
        o = o + score * row(hi)
    last = jnp.sum(lf, axis=0, keepdims=True)
    k_dec = k * jnp.exp(last - cum)
    new_state = state_t * jnp.exp(last) + lax.dot_general(hi, k_dec, TN, precision=HIGHEST, preferred_element_type=f32)
    return new_state, o


def _hg_chunk(state_t, hq, hf, hi, hg, l0, l1, ng):
    lb = _sigmoid(l0 - l1)
    outs = []
    for j in range(len(hq)):
        state_t, o = _hg_sub(state_t, hq[j], hf[j], hi[j], lb)
        outs.append(_rms(o, ng) * _silu(hg[j]))
    return state_t, outs


def _hg_pieces(ref, c):
    n = HG_CHUNK // HG_SUB
    return [ref[pl.ds(pl.multiple_of(c * HG_CHUNK + j * HG_SUB, HG_SUB), HG_SUB), :] for j in range(n)]


def _hg_specs(s):
    col = lambda off: pl.BlockSpec((s, HEAD_DIM), lambda h: (0, off + h))
    lrow = pl.BlockSpec((1, HEAD_DIM), lambda h: (0, h))
    ng = pl.BlockSpec((1, HEAD_DIM), lambda h: (0, 0))
    return col, lrow, ng


def _hg_fwd(proj, logits, ng):
    s = proj.shape[0]
    nch = s // HG_CHUNK
    n = HG_CHUNK // HG_SUB

    def body(hq_ref, hf_ref, hi_ref, hg_ref, l0_ref, l1_ref, ng_ref, o_ref, st_ref):
        def chunk(c, state_t):
            st_ref[c] = state_t
            state_t, outs = _hg_chunk(state_t, _hg_pieces(hq_ref, c), _hg_pieces(hf_ref, c), _hg_pieces(hi_ref, c),
                                      _hg_pieces(hg_ref, c), l0_ref[...], l1_ref[...], ng_ref[...])
            for j in range(n):
                o_ref[pl.ds(pl.multiple_of(c * HG_CHUNK + j * HG_SUB, HG_SUB), HG_SUB), :] = outs[j]
            return state_t

        lax.fori_loop(0, nch, chunk, jnp.zeros((HEAD_DIM, HEAD_DIM), f32))

    col, lrow, ngs = _hg_specs(s)
    return pl.pallas_call(
        body, name="hg_fwd", grid=(HEADS,),
        out_shape=[jax.ShapeDtypeStruct((s, HEADS * HEAD_DIM), f32),
                   jax.ShapeDtypeStruct((HEADS, nch, HEAD_DIM, HEAD_DIM), f32)],
        in_specs=[col(3 * HEADS), col(4 * HEADS), col(5 * HEADS), col(6 * HEADS), lrow, lrow, ngs],
        out_specs=[col(0), pl.BlockSpec((None, nch, HEAD_DIM, HEAD_DIM), lambda h: (h, 0, 0, 0))],
        compiler_params=_params("parallel"),
    )(proj, proj, proj, proj, logits[0:1], logits[1:2], ng)


def _hg_bwd(proj, logits, ng, states, do, do_off):
    s = proj.shape[0]
    nch = s // HG_CHUNK
    n = HG_CHUNK // HG_SUB

    def body(hq_ref, hf_ref, hi_ref, hg_ref, l0_ref, l1_ref, ng_ref, st_ref, do_ref,
             dq_ref, df_ref, di_ref, dg_ref, dl0_ref, dl1_ref, dng_ref):
        dl0_ref[...] = jnp.zeros_like(dl0_ref)
        dl1_ref[...] = jnp.zeros_like(dl1_ref)
        dng_ref[...] = jnp.zeros_like(dng_ref)

        def chunk(i, dstate):
            c = nch - 1 - i
            args = (st_ref[c], _hg_pieces(hq_ref, c), _hg_pieces(hf_ref, c), _hg_pieces(hi_ref, c),
                    _hg_pieces(hg_ref, c), l0_ref[...], l1_ref[...], ng_ref[...])
            _, vjp = jax.vjp(_hg_chunk, *args)
            dst, dq, df, di, dg, dl0, dl1, dng = vjp((dstate, _hg_pieces(do_ref, c)))
            for j in range(n):
                rows = pl.ds(pl.multiple_of(c * HG_CHUNK + j * HG_SUB, HG_SUB), HG_SUB)
                dq_ref[rows, :] = dq[j].astype(dq_ref.dtype)
                df_ref[rows, :] = df[j].astype(df_ref.dtype)
                di_ref[rows, :] = di[j].astype(di_ref.dtype)
                dg_ref[rows, :] = dg[j].astype(dg_ref.dtype)
            dl0_ref[...] += dl0
            dl1_ref[...] += dl1
            dng_ref[...] += dng
            return dst

        lax.fori_loop(0, nch, chunk, jnp.zeros((HEAD_DIM, HEAD_DIM), f32))

    col, lrow, ngs = _hg_specs(s)
    big = jax.ShapeDtypeStruct((s, HEADS * HEAD_DIM), bf16)
    vec = jax.ShapeDtypeStruct((1, HEADS * HEAD_DIM), f32)
    vspec = pl.BlockSpec((1, HEAD_DIM), lambda h: (0, h))
    dq, df, di, dg, dl0, dl1, dng = pl.pallas_call(
        body, name="hg_bwd", grid=(HEADS,),
        out_shape=[big, big, big, big, vec, vec, vec],
        in_specs=[col(3 * HEADS), col(4 * HEADS), col(5 * HEADS), col(6 * HEADS), lrow, lrow, ngs,
                  pl.BlockSpec((None, nch, HEAD_DIM, HEAD_DIM), lambda h: (h, 0, 0, 0)), col(do_off)],
        out_specs=[col(0), col(0), col(0), col(0), vspec, vspec, vspec],
        compiler_params=_params("parallel"),
    )(proj, proj, proj, proj, logits[0:1], logits[1:2], ng, states, do)
    return (jnp.concatenate([dq, df, di, dg], axis=1), jnp.concatenate([dl0, dl1], axis=0),
            dng.reshape(HEADS, HEAD_DIM))


def _shift_rows(x, k, down):
    if k == 0:
        return x
    s = x.shape[0]
    row = lax.broadcasted_iota(jnp.int32, x.shape, 0)
    if down:
        return jnp.where(row < k, 0.0, pltpu.roll(x, k, 0))
    return jnp.where(row >= s - k, 0.0, pltpu.roll(x, s - k, 0))


def _conv_fwd(proj, w, b):
    s, w2 = proj.shape
    width = w2 // 2
    nb = width // RG_BLOCK

    def body(x_ref, w_ref, b_ref, y_ref, yb_ref):
        x = x_ref[...]
        y = jnp.broadcast_to(b_ref[...], x.shape)
        for k in range(CONV_WIDTH):
            y = y + _shift_rows(x, k, True) * w_ref[pl.ds(k, 1), :]
        y_ref[...] = y
        yb_ref[...] = y.astype(bf16)

    blk = pl.BlockSpec((s, RG_BLOCK), lambda j: (0, j))
    return pl.pallas_call(
        body, name="conv_fwd", grid=(nb,),
        out_shape=[jax.ShapeDtypeStruct((s, width), f32), jax.ShapeDtypeStruct((s, width), bf16)],
        in_specs=[pl.BlockSpec((s, RG_BLOCK), lambda j: (0, nb + j)),
                  pl.BlockSpec((CONV_WIDTH, RG_BLOCK), lambda j: (0, j)), pl.BlockSpec((1, RG_BLOCK), lambda j: (0, j))],
        out_specs=[blk, blk],
        compiler_params=_params("parallel"),
    )(proj, w, b)


def _conv_bwd(proj, w, dy_a, dy_b):
    s, w2 = proj.shape
    width = w2 // 2
    nb = width // RG_BLOCK

    def body(x_ref, w_ref, dya_ref, dyb_ref, dx_ref, dw_ref, db_ref):
        x = x_ref[...]
        dy_v = dya_ref[...] + dyb_ref[...]
        dx = jnp.zeros_like(x)
        for k in range(CONV_WIDTH):
            dx = dx + _shift_rows(dy_v, k, False) * w_ref[pl.ds(k, 1), :]
            dw_ref[pl.ds(k, 1), :] = jnp.sum(dy_v * _shift_rows(x, k, True), axis=0, keepdims=True)
        dx_ref[...] = dx.astype(dx_ref.dtype)
        db_ref[...] = jnp.sum(dy_v, axis=0, keepdims=True)

    blk = pl.BlockSpec((s, RG_BLOCK), lambda j: (0, j))
    wblk = pl.BlockSpec((CONV_WIDTH, RG_BLOCK), lambda j: (0, j))
    bblk = pl.BlockSpec((1, RG_BLOCK), lambda j: (0, j))
    return pl.pallas_call(
        body, name="conv_bwd", grid=(nb,),
        out_shape=[jax.ShapeDtypeStruct((s, width), bf16), jax.ShapeDtypeStruct((CONV_WIDTH, width), f32),
                   jax.ShapeDtypeStruct((1, width), f32)],
        in_specs=[pl.BlockSpec((s, RG_BLOCK), lambda j: (0, nb + j)), wblk, blk, blk],
        out_specs=[blk, wblk, bblk],
        compiler_params=_params("parallel"),
    )(proj, w, dy_a, dy_b)


def _rg_rows(w_ref):
    return jnp.concatenate([w_ref[q] for q in range(N_CHIPS)], axis=0)


def _bd_fwd(xb, wa, wx):
    s, width = xb.shape
    nb = width // RG_BLOCK

    def body(x_ref, wa_ref, wx_ref, ra_ref, ix_ref):
        x = x_ref[...]
        ra_ref[...] = lax.dot_general(x, _rg_rows(wa_ref), NN, preferred_element_type=f32)
        ix_ref[...] = lax.dot_general(x, _rg_rows(wx_ref), NN, preferred_element_type=f32)

    blk = pl.BlockSpec((s, RG_BLOCK), lambda j: (0, j))
    wblk = pl.BlockSpec((N_CHIPS, None, RG_BLOCK // N_CHIPS, RG_BLOCK), lambda j: (0, j, 0, 0))
    out = jax.ShapeDtypeStruct((s, width), f32)
    return pl.pallas_call(
        body, name="bd_fwd", grid=(nb,), out_shape=[out, out],
        in_specs=[blk, wblk, wblk], out_specs=[blk, blk],
        compiler_params=_params("parallel"),
    )(xb, wa, wx)


def _bd_bwd(xb, wa, wx, dra, dix):
    s, width = xb.shape
    nb = width // RG_BLOCK
    rq = RG_BLOCK // N_CHIPS

    def body(x_ref, wa_ref, wx_ref, dra_ref, dix_ref, dx_ref, dwa_ref, dwx_ref):
        x = x_ref[...]
        dra_v = dra_ref[...]
        dix_v = dix_ref[...]
        dx_ref[...] = (lax.dot_general(dra_v, _rg_rows(wa_ref), NT, preferred_element_type=f32)
                       + lax.dot_general(dix_v, _rg_rows(wx_ref), NT, preferred_element_type=f32))
        dwa = lax.dot_general(x, dra_v, TN, preferred_element_type=f32)
        dwx = lax.dot_general(x, dix_v, TN, preferred_element_type=f32)
        for q in range(N_CHIPS):
            dwa_ref[q] = dwa[q * rq:(q + 1) * rq, :].astype(dwa_ref.dtype)
            dwx_ref[q] = dwx[q * rq:(q + 1) * rq, :].astype(dwx_ref.dtype)

    blk = pl.BlockSpec((s, RG_BLOCK), lambda j: (0, j))
    wblk = pl.BlockSpec((N_CHIPS, None, rq, RG_BLOCK), lambda j: (0, j, 0, 0))
    wout = jax.ShapeDtypeStruct((N_CHIPS, nb, rq, RG_BLOCK), bf16)
    return pl.pallas_call(
        body, name="bd_bwd", grid=(nb,),
        out_shape=[jax.ShapeDtypeStruct((s, width), f32), wout, wout],
        in_specs=[blk, wblk, wblk, blk, blk], out_specs=[blk, wblk, wblk],
        compiler_params=_params("parallel"),
    )(xb, wa, wx, dra, dix)


def _scan_tiles(a_ref, u_ref, out_ref, n_tiles, reverse):
    cb = a_ref.shape[1]
    row = lax.broadcasted_iota(jnp.int32, (SUBLANES, cb), 0)

    def tile(i, carry):
        t = (n_tiles - 1 - i) if reverse else i
        rows = pl.ds(pl.multiple_of(t * SUBLANES, SUBLANES), SUBLANES)
        a = a_ref[rows, :]
        b = u_ref[rows, :]
        for d in (1, 2, 4):
            if reverse:
                valid = row < SUBLANES - d
                shift = SUBLANES - d
            else:
                valid = row >= d
                shift = d
            a_n = jnp.where(valid, pltpu.roll(a, shift, 0), 1.0)
            b_n = jnp.where(valid, pltpu.roll(b, shift, 0), 0.0)
            b = a * b_n + b
            a = a * a_n
        out = a * carry + b
        out_ref[rows, :] = out
        edge = t * SUBLANES + (0 if reverse else SUBLANES - 1)
        return jnp.broadcast_to(out_ref[pl.ds(edge, 1), :], (SUBLANES, cb))

    lax.fori_loop(0, n_tiles, tile, jnp.zeros((SUBLANES, cb), f32))


def _scan_fwd(a, u):
    s, width = a.shape

    def body(a_ref, u_ref, h_ref):
        _scan_tiles(a_ref, u_ref, h_ref, s // SUBLANES, False)

    blk = pl.BlockSpec((s, RG_BLOCK), lambda j: (0, j))
    return pl.pallas_call(
        body, name="scan_fwd", grid=(width // RG_BLOCK,), out_shape=jax.ShapeDtypeStruct((s, width), f32),
        in_specs=[blk, blk], out_specs=blk, compiler_params=_params("parallel"),
    )(a, u)


def _scan_bwd(a, h, dh):
    s, width = a.shape

    def body(a_ref, h_ref, dh_ref, du_ref, da_ref, a_next):
        a_next[...] = _shift_rows(a_ref[...], 1, False)
        _scan_tiles(a_next, dh_ref, du_ref, s // SUBLANES, True)
        da_ref[...] = du_ref[...] * _shift_rows(h_ref[...], 1, True)

    blk = pl.BlockSpec((s, RG_BLOCK), lambda j: (0, j))
    out = jax.ShapeDtypeStruct((s, width), f32)
    return pl.pallas_call(
        body, name="scan_bwd", grid=(width // RG_BLOCK,), out_shape=[out, out],
        in_specs=[blk, blk, blk], out_specs=[blk, blk],
        scratch_shapes=[pltpu.VMEM((s, RG_BLOCK), f32)],
        compiler_params=_params("parallel"),
    )(a, h, dh)


EW_TILE_BYTES = 1 << 20


def _ew_rows(rows, width):
    limit = max(16, EW_TILE_BYTES // (4 * width))
    t = (min(limit, rows) // 16) * 16
    while t >= 16:
        if rows % t == 0:
            return t
        t -= 16
    return rows


def _as2d(a):
    return a.reshape(-1, a.shape[-1])


def _ew(fn, ins, out_dtypes, name):
    shape = ins[0].shape
    ins2 = [_as2d(a) for a in ins]
    rows, width = ins2[0].shape
    res = _rowwise(fn, ins2, [], [(width, dt) for dt in out_dtypes], name, _ew_rows(rows, width))
    return [r.reshape(shape) for r in res]


def _sum_chips(got):
    g2 = _as2d(got)
    rows, width = g2.shape[0] // N_CHIPS, g2.shape[1]
    up = lambda t: t.astype(f32)
    (out,) = _rowwise(lambda a, b, c, e: (((up(a) + up(b)) + up(c)) + up(e),), [(g2, width, 0, j * rows) for j in range(N_CHIPS)], [],
                      [(width, f32)], "rs_add_chips", _ew_rows(rows, width), m=rows)
    return out.reshape(got.shape[1:])


def _f_adamw(w, g, m, v):
    m = ADAM_B1 * m + (1.0 - ADAM_B1) * g
    v = ADAM_B2 * v + (1.0 - ADAM_B2) * (g * g)
    m_hat = m / (1.0 - ADAM_B1 ** ADAM_STEP)
    v_hat = v / (1.0 - ADAM_B2 ** ADAM_STEP)
    delta = -ADAM_LR * (m_hat / (jnp.sqrt(v_hat) + ADAM_EPS) + ADAM_WD * w)
    return delta, m, v


ANY = pl.BlockSpec(memory_space=pl.ANY)


def _place():
    x, y, c = lax.axis_index("x"), lax.axis_index("y"), lax.axis_index("c")
    return x, y, c, [(1 - x, y), (x, 1 - y), (1 - x, 1 - y)]


def _rcopy(src, dst, send_sem, recv_sem, dev):
    return pltpu.make_async_remote_copy(src_ref=src, dst_ref=dst, send_sem=send_sem, recv_sem=recv_sem,
                                        device_id=dev, device_id_type=MESH)


def _half(ref, h, kind, lead=0):
    ax = lead + (1 if kind == "rg" else 0)
    n = ref.shape[ax] // 2
    idx = [slice(None)] * ref.ndim
    idx[ax] = pl.ds(h * n, n)
    return ref.at[tuple(idx)]


def _gather_weights(shards, kinds):
    n = len(shards)

    def body(*refs):
        src, out = refs[:n], refs[n:2 * n]
        send_sems, recv_sems, local_sems = refs[2 * n:]
        x, y, c, chips = _place()
        my_q = 2 * x + y
        sibling = (x, y, 1 - c)
        local = [pltpu.make_async_copy(src[e], out[e].at[my_q], local_sems.at[e]) for e in range(n)]
        for cp in local:
            cp.start()
        started = []
        for e in range(n):
            for j, chip in enumerate(chips):
                cp = _rcopy(_half(src[e], c, kinds[e]), _half(out[e].at[my_q], c, kinds[e]),
                            send_sems.at[6 * e + j], recv_sems.at[6 * e + j], (*chip, c))
                cp.start()
                started.append(cp)
        for e in range(n):
            for j, chip in enumerate(chips):
                landed = _half(out[e].at[2 * chip[0] + chip[1]], c, kinds[e])
                _rcopy(landed, landed, send_sems.at[6 * e + j], recv_sems.at[6 * e + j], (*chip, c)).wait_recv()
                cp = _rcopy(landed, landed, send_sems.at[6 * e + 3 + j], recv_sems.at[6 * e + 3 + j], sibling)
                cp.start()
                started.append(cp)
        for e in range(n):
            for j, chip in enumerate(chips):
                passed = _half(out[e].at[2 * chip[0] + chip[1]], 1 - c, kinds[e])
                _rcopy(passed, passed, send_sems.at[6 * e + 3 + j], recv_sems.at[6 * e + 3 + j], sibling).wait_recv()
        for cp in started:
            cp.wait_send()
        for cp in local:
            cp.wait()

    return pl.pallas_call(
        body, name="gather_weights",
        out_shape=[jax.ShapeDtypeStruct((N_CHIPS,) + s.shape, s.dtype) for s in shards],
        in_specs=[ANY] * n, out_specs=[ANY] * n,
        scratch_shapes=[pltpu.SemaphoreType.DMA((6 * n,)), pltpu.SemaphoreType.DMA((6 * n,)),
                        pltpu.SemaphoreType.DMA((n,))],
    )(*shards)


def _exchange_halves(grads, kinds):
    n = len(grads)

    def body(*refs):
        g, mine, theirs = refs[:n], refs[n:2 * n], refs[2 * n:3 * n]
        send_sems, recv_sems, local_sems = refs[3 * n:]
        x, y, c, _ = _place()
        sibling = (x, y, 1 - c)
        local = [pltpu.make_async_copy(_half(g[e], c, kinds[e], 1), mine[e], local_sems.at[e]) for e in range(n)]
        sent = [_rcopy(_half(g[e], 1 - c, kinds[e], 1), theirs[e], send_sems.at[e], recv_sems.at[e], sibling)
                for e in range(n)]
        for cp in local + sent:
            cp.start()
        for cp in sent:
            cp.wait()
        for cp in local:
            cp.wait()

    def half_shape(s, kind):
        shp = list(s.shape)
        shp[2 if kind == "rg" else 1] //= 2
        return jax.ShapeDtypeStruct(tuple(shp), s.dtype)

    halves = [half_shape(s, k) for s, k in zip(grads, kinds)]
    res = pl.pallas_call(
        body, name="rs_core_exchange", out_shape=halves + halves,
        in_specs=[ANY] * n, out_specs=[ANY] * (2 * n),
        scratch_shapes=[pltpu.SemaphoreType.DMA((n,)), pltpu.SemaphoreType.DMA((n,)), pltpu.SemaphoreType.DMA((n,))],
    )(*grads)
    return res[:n], res[n:]


def _exchange_chips(parts):
    n = len(parts)

    def body(*refs):
        p, got = refs[:n], refs[n:2 * n]
        send_sems, recv_sems, local_sems = refs[2 * n:]
        x, y, c, chips = _place()
        local = [pltpu.make_async_copy(p[e].at[2 * x + y], got[e].at[3], local_sems.at[e]) for e in range(n)]
        sent = [_rcopy(p[e].at[2 * chip[0] + chip[1]], got[e].at[j], send_sems.at[3 * e + j], recv_sems.at[3 * e + j],
                       (*chip, c))
                for e in range(n) for j, chip in enumerate(chips)]
        for cp in local + sent:
            cp.start()
        for cp in sent:
            cp.wait()
        for cp in local:
            cp.wait()

    return pl.pallas_call(
        body, name="rs_chip_exchange", out_shape=[jax.ShapeDtypeStruct(s.shape, s.dtype) for s in parts],
        in_specs=[ANY] * n, out_specs=[ANY] * n,
        scratch_shapes=[pltpu.SemaphoreType.DMA((3 * n,)), pltpu.SemaphoreType.DMA((3 * n,)),
                        pltpu.SemaphoreType.DMA((n,))],
    )(*parts)


def _share_halves(halves, kinds, layers, out_shapes):
    n = len(halves)
    n_out = len(out_shapes)

    def body(*refs):
        h, out = refs[:n], refs[n:n + n_out]
        send_sems, recv_sems, local_sems = refs[n + n_out:]
        x, y, c, _ = _place()
        sibling = (x, y, 1 - c)
        local, sent = [], []
        for e, (o, l) in enumerate(layers):
            dst = _half(out[o].at[l], c, kinds[e])
            local.append(pltpu.make_async_copy(h[e], dst, local_sems.at[e]))
            sent.append(_rcopy(h[e], dst, send_sems.at[e], recv_sems.at[e], sibling))
        for cp in local + sent:
            cp.start()
        for e, (o, l) in enumerate(layers):
            dst = _half(out[o].at[l], 1 - c, kinds[e])
            _rcopy(h[e], dst, send_sems.at[e], recv_sems.at[e], sibling).wait_recv()
        for cp in sent:
            cp.wait_send()
        for cp in local:
            cp.wait()

    return pl.pallas_call(
        body, name="rs_share_halves", out_shape=list(out_shapes),
        in_specs=[ANY] * n, out_specs=[ANY] * n_out,
        scratch_shapes=[pltpu.SemaphoreType.DMA((n,)), pltpu.SemaphoreType.DMA((n,)), pltpu.SemaphoreType.DMA((n,))],
    )(*halves)


def _all_gather_rows(v, reduce):
    rows = v.shape[0]

    def body(v_ref, out_ref, *rest):
        if reduce:
            sum_ref, send_sems, recv_sems, local_sem = rest
        else:
            send_sems, recv_sems, local_sem = rest
        x, y, c, chips = _place()
        me, sibling = (x, y, c), (x, y, 1 - c)

        def blk(px, py, pc):
            return out_ref.at[pl.ds((4 * px + 2 * py + pc) * rows, rows), :]

        def copy(k, block, to, src=None):
            return _rcopy(blk(*block) if src is None else src, blk(*block), send_sems.at[k], recv_sems.at[k], to)

        mine = pltpu.make_async_copy(v_ref, blk(*me), local_sem)
        mine.start()
        first = [copy(0, me, sibling, src=v_ref)] + [copy(1 + j, me, (*chip, c), src=v_ref) for j, chip in enumerate(chips)]
        for cp in first:
            cp.start()
        passed = [copy(4 + j, (*chip, c), sibling) for j, chip in enumerate(chips)]
        for j, chip in enumerate(chips):
            copy(1 + j, (*chip, c), me).wait_recv()
            passed[j].start()
        copy(0, sibling, me).wait_recv()
        for j, chip in enumerate(chips):
            copy(4 + j, (*chip, 1 - c), me).wait_recv()
        for cp in first + passed:
            cp.wait_send()
        mine.wait()
        if reduce:
            acc = out_ref[pl.ds(0, rows), :]
            for d in range(1, 8):
                acc = acc + out_ref[pl.ds(d * rows, rows), :]
            sum_ref[...] = acc

    vm = pl.BlockSpec(memory_space=pltpu.VMEM)
    gathered = jax.ShapeDtypeStruct((8 * rows, LANES), f32)
    res = pl.pallas_call(
        body, name="all_reduce_small" if reduce else "all_gather_small",
        out_shape=[gathered, jax.ShapeDtypeStruct((rows, LANES), f32)] if reduce else [gathered],
        in_specs=[vm], out_specs=[vm, vm] if reduce else [vm],
        scratch_shapes=[pltpu.SemaphoreType.DMA((7,)), pltpu.SemaphoreType.DMA((7,)), pltpu.SemaphoreType.DMA],
    )(v)
    return res[1] if reduce else res[0]


def _pack(parts):
    flat = jnp.concatenate([p.reshape(-1).astype(f32) for p in parts])
    rows = -(-flat.shape[0] // (SUBLANES * LANES)) * SUBLANES
    return jnp.pad(flat, (0, rows * LANES - flat.shape[0])).reshape(rows, LANES)


def _unpack(packed, shapes):
    flat = packed.reshape(-1)
    out, off = [], 0
    for shp in shapes:
        size = 1
        for d in shp:
            size *= d
        out.append(flat[off:off + size].reshape(shp))
        off += size
    return out


WEIGHTS = ["mix_pre_g", "mix_post_g", "ffn_pre_g", "ffn_post_g", "ple_norm_g", "w_in_even", "w_out_even",
           "hg_lb_logits", "hg_norm_g", "w_in_odd", "conv_w", "conv_b", "rg_wa", "rg_ba", "rg_wx", "rg_bx",
           "rg_lambda", "w_out_odd", "w_gate_up", "w_down", "w_ple_up", "w_ple_gate"]
BIG = [("w_in_even", 0, "col"), ("w_out_even", 0, "row"), ("w_gate_up", 0, "col"), ("w_down", 0, "row"),
       ("w_ple_up", 0, "col"), ("w_ple_gate", 0, "row"), ("w_in_odd", 0, "col"), ("rg_wa", 0, "rg"),
       ("rg_wx", 0, "rg"), ("w_out_odd", 0, "row"), ("w_gate_up", 1, "col"), ("w_down", 1, "row"),
       ("w_ple_up", 1, "col"), ("w_ple_gate", 1, "row")]
BIG_NAMES = ["w_in_even", "w_out_even", "w_gate_up", "w_down", "w_ple_up", "w_ple_gate", "w_in_odd", "rg_wa", "rg_wx",
             "w_out_odd"]
SMALL_SHARDED = ["conv_w", "conv_b", "rg_ba", "rg_bx", "rg_lambda"]
REPLICATED = ["mix_pre_g", "mix_post_g", "ffn_pre_g", "ffn_post_g", "ple_norm_g", "hg_lb_logits", "hg_norm_g"]
TR = 256
TR_WIDE = 128


def _cast_shard(w, layer, kind):
    if kind == "rg":
        flat = _as2d(w[layer])
        (out,) = _rowwise(lambda v: (v,), [flat], [], [(flat.shape[1], bf16)], "cast_" + kind, _ew_rows(*flat.shape))
        return out.reshape(w.shape[1:])
    rows, width = w.shape[1], w.shape[2]
    tr = _ew_rows(rows, width)
    (out,) = _rowwise(lambda v: (v,), [(_as2d(w), width, 0, layer * rows)], [], [(width, bf16)], "cast_" + kind, tr, m=rows)
    return out


def _ffn_ple_fwd(h, m, l, gains, w_gu, w_d, w_pu, w_pg, pb, next_pre):
    d = h.shape[1]
    ff = w_d.shape[0]
    hb, nf = _rowwise(_f_post_pre, [h, m], [gains["mix_post_g"][l], gains["ffn_pre_g"][l]], [(d, f32), (d, bf16)],
                      f"post_mix{l}", TR)
    gu = _mm(nf, w_gu, "nn", f32, f"ffn_up{l}", b_shards=True)
    (act,) = _rowwise(_f_swiglu, [(gu, ff, 0), (gu, ff, 1)], [], [(ff, bf16)], f"swiglu{l}", TR_WIDE)
    f = _mm(act, w_d, "nn", f32, f"ffn_down{l}")
    hc, hcb = _rowwise(_f_post, [hb, f], [gains["ffn_post_g"][l]], [(d, f32), (d, bf16)], f"post_ffn{l}", TR)
    t = _mm(hcb, w_pg, "nn", f32, f"ple_gate{l}")
    e = _mm(pb, w_pu, "nn", f32, f"ple_up{l}", b_shards=True)
    if next_pre is None:
        (hd,) = _rowwise(_f_ple, [hc, t, e], [gains["ple_norm_g"][l]], [(d, f32)], f"ple{l}", TR)
        n_next = None
    else:
        hd, n_next = _rowwise(_f_ple_pre, [hc, t, e], [gains["ple_norm_g"][l], next_pre], [(d, f32), (d, bf16)],
                              f"ple{l}", TR)
    saved = dict(h=h, m=m, hb=hb, nf=nf, gu=gu, act=act, f=f, hc=hc, hcb=hcb, t=t, e=e)
    return hd, n_next, saved


def _ffn_ple_bwd(sv, l, gains, w_gu, w_d, w_pg, pb, next_pre, dhd, dn_next):
    ff = w_d.shape[0]
    if next_pre is None:
        (dhc, dt, de), (dg_ple,) = _rowwise_vjp(_f_ple, [sv["hc"], sv["t"], sv["e"]], [gains["ple_norm_g"][l]], [dhd],
                                                [(0, f32), (1, bf16), (2, bf16)], f"ple_bwd{l}", TR)
        dg_next = None
    else:
        (dhc, dt, de), (dg_ple, dg_next) = _rowwise_vjp(
            _f_ple_pre, [sv["hc"], sv["t"], sv["e"]], [gains["ple_norm_g"][l], next_pre], [dhd, dn_next],
            [(0, f32), (1, bf16), (2, bf16)], f"ple_bwd{l}", TR)
    gw = {}
    gw["w_ple_gate"] = _mm(sv["hcb"], dt, "tn", bf16, f"ple_gate_wgrad{l}")
    gw["w_ple_up"] = _mm(pb, de, "tn", bf16, f"ple_up_wgrad{l}", out_shards=True)
    dhcb = _mm(dt, w_pg, "nt", f32, f"ple_gate_dgrad{l}")
    (dhb, df), (dg_fpost,) = _rowwise_vjp(_f_post, [sv["hb"], sv["f"]], [gains["ffn_post_g"][l]], [dhc, dhcb],
                                          [(0, f32), (1, bf16)], f"post_ffn_bwd{l}", TR)
    gw["w_down"] = _mm(sv["act"], df, "tn", bf16, f"ffn_down_wgrad{l}")
    dact = _mm(df, w_d, "nt", f32, f"ffn_down_dgrad{l}")
    (dgu,), _ = _rowwise_vjp(_f_swiglu, [(sv["gu"], ff, 0), (sv["gu"], ff, 1)], [], [dact], [(0, bf16), (1, bf16)],
                             f"swiglu_bwd{l}", TR_WIDE, concat=True)
    gw["w_gate_up"] = _mm(sv["nf"], dgu, "tn", bf16, f"ffn_up_wgrad{l}", out_shards=True)
    dnf = _mm(dgu, w_gu, "nt", f32, f"ffn_up_dgrad{l}", b_shards=True)
    (dh, dm), (dg_post, dg_fpre) = _rowwise_vjp(
        _f_post_pre, [sv["h"], sv["m"]], [gains["mix_post_g"][l], gains["ffn_pre_g"][l]], [dhb, dnf],
        [(0, f32), (1, bf16)], f"post_mix_bwd{l}", TR)
    gg = dict(ple_norm_g=dg_ple, ffn_post_g=dg_fpost, mix_post_g=dg_post, ffn_pre_g=dg_fpre)
    return dh, dm, gw, gg, dg_next


def kernel(x, p, mix_pre_g, mix_post_g, ffn_pre_g, ffn_post_g, ple_norm_g, w_in_even, w_out_even, hg_lb_logits, hg_norm_g, w_in_odd, conv_w, conv_b, rg_wa, rg_ba, rg_wx, rg_bx, rg_lambda, w_out_odd, w_gate_up, w_down, w_ple_up, w_ple_gate, loss_target, m_mix_pre_g, m_mix_post_g, m_ffn_pre_g, m_ffn_post_g, m_ple_norm_g, m_w_in_even, m_w_out_even, m_hg_lb_logits, m_hg_norm_g, m_w_in_odd, m_conv_w, m_conv_b, m_rg_wa, m_rg_ba, m_rg_wx, m_rg_bx, m_rg_lambda, m_w_out_odd, m_w_gate_up, m_w_down, m_w_ple_up, m_w_ple_gate, v_mix_pre_g, v_mix_post_g, v_ffn_pre_g, v_ffn_post_g, v_ple_norm_g, v_w_in_even, v_w_out_even, v_hg_lb_logits, v_hg_norm_g, v_w_in_odd, v_conv_w, v_conv_b, v_rg_wa, v_rg_ba, v_rg_wx, v_rg_bx, v_rg_lambda, v_w_out_odd, v_w_gate_up, v_w_down, v_w_ple_up, v_w_ple_gate):
    given = dict(locals())
    w = {n: given[n] for n in WEIGHTS}
    mom = {n: given["m_" + n] for n in WEIGHTS}
    var = {n: given["v_" + n] for n in WEIGHTS}
    xi, yi = lax.axis_index("x"), lax.axis_index("y")
    my_q = 2 * xi + yi
    h0 = x[0]
    s, d = h0.shape
    n_layers = p.shape[0]
    gains = {n: [w[n][l:l + 1] for l in range(n_layers)]
             for n in ("mix_pre_g", "mix_post_g", "ffn_pre_g", "ffn_post_g", "ple_norm_g")}

    kinds = [k for _, _, k in BIG]
    shards = [_cast_shard(w[n], l, k) for n, l, k in BIG]
    full = dict(zip([(n, l) for n, l, _ in BIG], _gather_weights(shards, kinds)))

    def rows_full(name, l):
        g = full[(name, l)]
        return g.reshape(g.shape[0] * g.shape[1], g.shape[2])

    small_shapes = [w[n].shape[1:] for n in SMALL_SHARDED]
    gathered = _all_gather_rows(_pack([w[n][0] for n in SMALL_SHARDED]), False)
    per_chip = [_unpack(gathered.reshape(8, -1)[2 * q], small_shapes) for q in range(N_CHIPS)]
    conv_w_f, conv_b_f, rg_ba_f, rg_bx_f, rg_lam_f = [
        jnp.concatenate([per_chip[q][i] for q in range(N_CHIPS)], axis=-1) for i in range(len(SMALL_SHARDED))]
    lru = conv_b_f.shape[0]
    conv_b_f, rg_ba_f, rg_bx_f, rg_lam_f = [v.reshape(1, lru) for v in (conv_b_f, rg_ba_f, rg_bx_f, rg_lam_f)]
    pb = [_ew(lambda v: (v,), [p[l, 0]], [bf16], f"cast_p{l}")[0] for l in range(n_layers)]

    (n0,) = _rowwise(_f_pre, [h0], [gains["mix_pre_g"][0]], [(d, bf16)], "pre_mix0", TR)
    proj0 = _mm(n0, full[("w_in_even", 0)], "nn", f32, "in_even", b_shards=True)
    a_out = _sb_fwd(proj0)
    b_out, hg_states = _hg_fwd(proj0, w["hg_lb_logits"], w["hg_norm_g"])
    mix0 = jnp.concatenate([a_out, b_out], axis=1).astype(bf16)
    m0 = _mm(mix0, rows_full("w_out_even", 0), "nn", f32, "out_even")
    h1, n1, sv0 = _ffn_ple_fwd(h0, m0, 0, gains, full[("w_gate_up", 0)], rows_full("w_down", 0), full[("w_ple_up", 0)],
                               rows_full("w_ple_gate", 0), pb[0], gains["mix_pre_g"][1])

    proj1 = _mm(n1, full[("w_in_odd", 0)], "nn", f32, "in_odd", b_shards=True)
    xc, xcb = _conv_fwd(proj1, conv_w_f, conv_b_f)
    ra, ix = _bd_fwd(xcb, full[("rg_wa", 0)], full[("rg_wx", 0)])
    gate_in = [ra, ix, xc], [rg_ba_f, rg_bx_f, rg_lam_f]
    a_rec, u_rec = _rowwise(_f_rg_gate, *gate_in, [(lru, f32), (lru, f32)], "rg_gate", TR_WIDE, row_ids=True)
    hs = _scan_fwd(a_rec, u_rec)
    out_in = [(proj1, lru, 0), hs]
    (yb,) = _rowwise(_f_rg_out, out_in, [], [(lru, bf16)], "rg_out", TR_WIDE)
    m1 = _mm(yb, rows_full("w_out_odd", 0), "nn", f32, "out_odd")
    h2, _, sv1 = _ffn_ple_fwd(h1, m1, 1, gains, full[("w_gate_up", 1)], rows_full("w_down", 1), full[("w_ple_up", 1)],
                              rows_full("w_ple_gate", 1), pb[1], None)
    dy, loss_local = _loss_head(h2, loss_target[0])
    loss = lax.psum(loss_local, ("x", "y", "c"))

    gbig = {}
    dh1, dm1, gw, gg1, _ = _ffn_ple_bwd(sv1, 1, gains, full[("w_gate_up", 1)], rows_full("w_down", 1),
                                        rows_full("w_ple_gate", 1), pb[1], None, dy, None)
    gbig.update({(n, 1): g for n, g in gw.items()})
    gbig[("w_out_odd", 0)] = _mm(yb, dm1, "tn", bf16, "out_odd_wgrad")
    dyo = _mm(dm1, rows_full("w_out_odd", 0), "nt", f32, "out_odd_dgrad")
    (dgate, dhs), _ = _rowwise_vjp(_f_rg_out, out_in, [], [dyo], [(0, bf16), (1, f32)], "rg_out_bwd", TR_WIDE)
    du, da = _scan_bwd(a_rec, hs, dhs)
    (dra, dix, dxc_direct), (g_ba, g_bx, g_lam) = _rowwise_vjp(
        _f_rg_gate, *gate_in, [da, du], [(0, bf16), (1, bf16), (2, f32)], "rg_gate_bwd", TR_WIDE, row_ids=True)
    dxc_gates, gbig[("rg_wa", 0)], gbig[("rg_wx", 0)] = _bd_bwd(xcb, full[("rg_wa", 0)], full[("rg_wx", 0)], dra, dix)
    dxb, g_conv_w, g_conv_b = _conv_bwd(proj1, conv_w_f, dxc_direct, dxc_gates)
    dproj1 = jnp.concatenate([dgate, dxb], axis=1)
    gbig[("w_in_odd", 0)] = _mm(n1, dproj1, "tn", bf16, "in_odd_wgrad", out_shards=True)
    dn1 = _mm(dproj1, full[("w_in_odd", 0)], "nt", f32, "in_odd_dgrad", b_shards=True)

    dh0, dm0, gw, gg0, g_pre1 = _ffn_ple_bwd(sv0, 0, gains, full[("w_gate_up", 0)], rows_full("w_down", 0),
                                             rows_full("w_ple_gate", 0), pb[0], gains["mix_pre_g"][1], dh1, dn1)
    gbig.update({(n, 0): g for n, g in gw.items()})
    gbig[("w_out_even", 0)] = _mm(mix0, dm0, "tn", bf16, "out_even_wgrad")
    dmix = _mm(dm0, rows_full("w_out_even", 0), "nt", f32, "out_even_dgrad")
    dq, dk, dv = _sb_bwd(proj0, a_out, dmix)
    dhg, g_logits, g_ng = _hg_bwd(proj0, w["hg_lb_logits"], w["hg_norm_g"], hg_states, dmix, HEADS)
    dproj0 = jnp.concatenate([dq, dk, dv, dhg], axis=1)
    gbig[("w_in_even", 0)] = _mm(n0, dproj0, "tn", bf16, "in_even_wgrad", out_shards=True)
    dn0 = _mm(dproj0, full[("w_in_even", 0)], "nt", f32, "in_even_dgrad", b_shards=True)
    (grad_x,), (g_pre0,) = _rowwise_vjp(_f_pre_id, [h0], [gains["mix_pre_g"][0]], [dh0, dn0], [(0, f32)], "pre_mix0_bwd", TR)

    def shard_major(g, kind):
        return g.reshape(N_CHIPS, g.shape[0] // N_CHIPS, g.shape[1]) if kind == "row" else g

    glist = [shard_major(gbig[(n, l)], k) for n, l, k in BIG]
    mine, theirs = _exchange_halves(glist, kinds)
    chip_part = [_ew(lambda u, v: (u.astype(f32) + v.astype(f32),), [a, b], [bf16], "rs_add_cores")[0]
                 for a, b in zip(mine, theirs)]
    got = _exchange_chips(chip_part)
    reduced = [_sum_chips(g) for g in got]
    layers = [(BIG_NAMES.index(n), l) for n, l, _ in BIG]
    grad_big = dict(zip(BIG_NAMES, _share_halves(reduced, kinds, layers,
                                                 [jax.ShapeDtypeStruct(w[n].shape, f32) for n in BIG_NAMES])))

    g_small = dict(
        mix_pre_g=jnp.concatenate([g_pre0, g_pre1], 0), mix_post_g=jnp.concatenate([gg0["mix_post_g"], gg1["mix_post_g"]], 0),
        ffn_pre_g=jnp.concatenate([gg0["ffn_pre_g"], gg1["ffn_pre_g"]], 0),
        ffn_post_g=jnp.concatenate([gg0["ffn_post_g"], gg1["ffn_post_g"]], 0),
        ple_norm_g=jnp.concatenate([gg0["ple_norm_g"], gg1["ple_norm_g"]], 0),
        hg_lb_logits=g_logits, hg_norm_g=jnp.sum(g_ng, axis=0, keepdims=True),
        conv_w=g_conv_w, conv_b=g_conv_b, rg_ba=g_ba, rg_bx=g_bx, rg_lambda=g_lam)
    order = REPLICATED + SMALL_SHARDED
    summed = dict(zip(order, _unpack(_all_reduce_rows(_pack([g_small[n] for n in order])),
                                     [g_small[n].shape for n in order])))
    grad_small = {n: summed[n].reshape(w[n].shape) for n in REPLICATED}
    for n in SMALL_SHARDED:
        width = w[n].shape[-1]
        if n in ("rg_ba", "rg_bx"):
            full_g = summed[n].reshape(1, -1, RG_BLOCK)
        else:
            full_g = summed[n].reshape((1,) + g_small[n].shape) if n == "conv_w" else summed[n]
        grad_small[n] = lax.dynamic_slice_in_dim(full_g, my_q * width, width, axis=full_g.ndim - 1).reshape(w[n].shape)

    grads = {**grad_big, **grad_small}
    delta, new_m, new_v = {}, {}, {}
    for n in BIG_NAMES:
        delta[n], new_m[n], new_v[n] = _ew(_f_adamw, [w[n], grads[n], mom[n], var[n]], [f32, f32, f32], "adamw_" + n)
    small = REPLICATED + SMALL_SHARDED
    packed = [_pack([src[n] for n in small]) for src in (w, grads, mom, var)]
    outs = _ew(_f_adamw, packed, [f32, f32, f32], "adamw_small")
    for dst, o in zip((delta, new_m, new_v), outs):
        dst.update(zip(small, _unpack(o, [w[n].shape for n in small])))
    return (loss, grad_x.reshape(x.shape), *[grads[n] for n in WEIGHTS], *[delta[n] for n in WEIGHTS],
            *[new_m[n] for n in WEIGHTS], *[new_v[n] for n in WEIGHTS])


def _all_reduce_rows(v):
    return _all_gather_rows(v, True)
```

```python
import functools

import jax
import jax.numpy as jnp
from jax import lax
from jax.experimental import pallas as pl
from jax.experimental.pallas import tpu as pltpu

f32 = jnp.float32
bf16 = jnp.bfloat16
HIGHEST = lax.Precision.HIGHEST
MESH = pl.DeviceIdType.MESH

LANES = 128
SUBLANES = 8
MXU_WIDTH = 256
VMEM_LIMIT_BYTES = 56 * 1024 * 1024

RMS_EPS = 1e-6
HEADS = 8
HEAD_DIM = 128
SB_BLOCK = 128
HG_SUB = 16
HG_CHUNK = 64
RG_BLOCK = 256
CONV_WIDTH = 4
RG_C = 8.0
ADAM_LR, ADAM_B1, ADAM_B2, ADAM_EPS, ADAM_WD, ADAM_STEP = 0.001, 0.9, 0.999, 1e-08, 0.01, 10
N_CHIPS = 4

NN = (((1,), (0,)), ((), ()))
NT = (((1,), (1,)), ((), ()))
TN = (((0,), (0,)), ((), ()))


def _params(*sem):
    return pltpu.CompilerParams(dimension_semantics=sem, vmem_limit_bytes=VMEM_LIMIT_BYTES)


def _pick(dim, target, align=LANES):
    if dim <= target:
        return dim
    t = (target // align) * align
    while t >= align:
        if dim % t == 0:
            return t
        t -= align
    return dim


MM_FULL_K = 2816
MM_VMEM_BUDGET = 44 * 1024 * 1024


def _mm_tiles(m, n_unit, k_unit, k_whole, out_bytes):
    tk = k_unit if (k_whole and k_unit <= MM_FULL_K) else _pick(k_unit, 512)
    tk = k_unit if tk < MXU_WIDTH else tk
    tm = m if m <= 2048 else _pick(m, 512)
    tn = _pick(n_unit, 2048 if (m > 2048 or tk < k_unit or not k_whole) else 1024)
    tn = n_unit if tn < MXU_WIDTH else tn
    need = lambda: 2 * (2 * tm * tk + 2 * tk * tn + out_bytes * tm * tn)
    while need() > MM_VMEM_BUDGET:
        if tn > 512 and (tn // 2) % LANES == 0 and n_unit % (tn // 2) == 0:
            tn //= 2
        elif tm > 256:
            tm //= 2
        else:
            break
    return tm, tn, tk


def _mm(a, b, mode, out_dtype, name, *, b_shards=False, out_shards=False):
    if mode == "nn":
        m, kc = a.shape
        n = b.shape[-1] * (N_CHIPS if b_shards else 1)
    elif mode == "nt":
        m, kc = a.shape
        n = b.shape[-2]
    else:
        kc, m = a.shape
        n = b.shape[-1]
    n_sh = n // N_CHIPS
    kc_sh = kc // N_CHIPS
    n_unit = n_sh if (out_shards or (b_shards and mode == "nn")) else n
    k_split = b_shards and mode == "nt"
    tm, tn, tk = _mm_tiles(m, n_unit, kc_sh if k_split else kc, not k_split, jnp.dtype(out_dtype).itemsize)
    nk = kc // tk
    assert nk == 1 or out_dtype == f32, "a chunked contraction accumulates in the f32 output block"
    dims = {"nn": NN, "nt": NT, "tn": TN}[mode]

    def body(a_ref, b_ref, o_ref):
        part = lax.dot_general(a_ref[...], b_ref[...], dims, preferred_element_type=f32)
        if nk == 1:
            o_ref[...] = part.astype(o_ref.dtype)
        else:
            k = pl.program_id(2)

            @pl.when(k == 0)
            def _():
                o_ref[...] = part

            @pl.when(k > 0)
            def _():
                o_ref[...] += part

    if mode == "tn":
        a_spec = pl.BlockSpec((tk, tm), lambda i, j, k: (k, i))
    else:
        a_spec = pl.BlockSpec((tm, tk), lambda i, j, k: (i, k))
    nb_sh = n_sh // tn
    if mode == "nn":
        if b_shards:
            b_spec = pl.BlockSpec((None, tk, tn), lambda i, j, k: (j // nb_sh, k, j % nb_sh))
        else:
            b_spec = pl.BlockSpec((tk, tn), lambda i, j, k: (k, j))
    elif mode == "nt":
        if b_shards:
            kb_sh = kc_sh // tk
            b_spec = pl.BlockSpec((None, tn, tk), lambda i, j, k: (k // kb_sh, j, k % kb_sh))
        else:
            b_spec = pl.BlockSpec((tn, tk), lambda i, j, k: (j, k))
    else:
        b_spec = pl.BlockSpec((tk, tn), lambda i, j, k: (k, j))
    if out_shards:
        out_shape = jax.ShapeDtypeStruct((N_CHIPS, m, n_sh), out_dtype)
        o_spec = pl.BlockSpec((None, tm, tn), lambda i, j, k: (j // nb_sh, i, j % nb_sh))
    else:
        out_shape = jax.ShapeDtypeStruct((m, n), out_dtype)
        o_spec = pl.BlockSpec((tm, tn), lambda i, j, k: (i, j))
    return pl.pallas_call(
        body, name=name, out_shape=out_shape, grid=(m // tm, n // tn, nk),
        in_specs=[a_spec, b_spec], out_specs=o_spec,
        compiler_params=_params("parallel", "parallel", "arbitrary"),
    )(a, b)


def _row_spec(entry, tr):
    if isinstance(entry, tuple):
        arr, width, cb = entry[:3]
        rb = (entry[3] // tr) if len(entry) > 3 else 0
        return arr, pl.BlockSpec((tr, width), lambda i, cb=cb, rb=rb: (i + rb, cb))
    return entry, pl.BlockSpec((tr, entry.shape[1]), lambda i: (i, 0))


def _whole_spec(arr):
    return pl.BlockSpec(arr.shape, lambda i: (0,) * arr.ndim)


def _rowwise(fn, rows, bcast, outs, name, tr, row_ids=False, m=None):
    arrs, specs = zip(*[_row_spec(e, tr) for e in rows])
    m = arrs[0].shape[0] if m is None else m
    nr, nb = len(rows), len(bcast)

    def body(*refs):
        vals = [r[...] for r in refs[:nr + nb]]
        if row_ids:
            rid = pl.program_id(0) * tr + lax.broadcasted_iota(jnp.int32, (tr, 1), 0)
            res = fn(rid, *vals)
        else:
            res = fn(*vals)
        for r, o in zip(res, refs[nr + nb:]):
            o[...] = r.astype(o.dtype)

    return pl.pallas_call(
        body, name=name, grid=(m // tr,),
        out_shape=[jax.ShapeDtypeStruct((m, w), dt) for w, dt in outs],
        in_specs=list(specs) + [_whole_spec(b) for b in bcast],
        out_specs=[pl.BlockSpec((tr, w), lambda i: (i, 0)) for w, _ in outs],
        compiler_params=_params("parallel"),
    )(*arrs, *bcast)


def _rowwise_vjp(fn, rows, bcast, cts, row_grads, name, tr, row_ids=False, concat=False):
    arrs, specs = zip(*[_row_spec(e, tr) for e in rows])
    ct_arrs, ct_specs = zip(*[_row_spec(e, tr) for e in cts])
    m = arrs[0].shape[0]
    nr, nb, nc, ng = len(rows), len(bcast), len(cts), len(row_grads)
    widths = [specs[idx].block_shape[1] for idx, _ in row_grads]

    def body(*refs):
        vals = [r[...] for r in refs[:nr + nb]]
        ctv = [r[...] for r in refs[nr + nb:nr + nb + nc]]
        out_refs = refs[nr + nb + nc:]
        if row_ids:
            rid = pl.program_id(0) * tr + lax.broadcasted_iota(jnp.int32, (tr, 1), 0)
            f = functools.partial(fn, rid)
        else:
            f = fn
        res, vjp = jax.vjp(f, *vals)
        grads = vjp(tuple(c.astype(r.dtype) for c, r in zip(ctv, res)))
        if concat:
            off = 0
            for (idx, _), w in zip(row_grads, widths):
                out_refs[0][:, off:off + w] = grads[idx].astype(out_refs[0].dtype)
                off += w
            b_refs = out_refs[1:]
        else:
            for (idx, _), o in zip(row_grads, out_refs[:ng]):
                o[...] = grads[idx].astype(o.dtype)
            b_refs = out_refs[ng:]

        @pl.when(pl.program_id(0) == 0)
        def _():
            for o in b_refs:
                o[...] = jnp.zeros_like(o)

        for j, o in enumerate(b_refs):
            o[...] += grads[nr + j].astype(f32)

    if concat:
        dt = row_grads[0][1]
        g_shapes = [jax.ShapeDtypeStruct((m, sum(widths)), dt)]
        g_specs = [pl.BlockSpec((tr, sum(widths)), lambda i: (i, 0))]
    else:
        g_shapes = [jax.ShapeDtypeStruct((m, w), dt) for (_, dt), w in zip(row_grads, widths)]
        g_specs = [pl.BlockSpec((tr, w), lambda i: (i, 0)) for w in widths]
    res = pl.pallas_call(
        body, name=name, grid=(m // tr,),
        out_shape=g_shapes + [jax.ShapeDtypeStruct(b.shape, f32) for b in bcast],
        in_specs=list(specs) + [_whole_spec(b) for b in bcast] + list(ct_specs),
        out_specs=g_specs + [_whole_spec(b) for b in bcast],
        compiler_params=_params("arbitrary"),
    )(*arrs, *bcast, *ct_arrs)
    n_g = len(g_shapes)
    return list(res[:n_g]), list(res[n_g:])


def _rms(x, g):
    return x * lax.rsqrt(jnp.mean(x * x, axis=-1, keepdims=True) + RMS_EPS) * g


def _sigmoid(x):
    return 1.0 / (1.0 + jnp.exp(-x))


def _silu(x):
    return x * _sigmoid(x)


def _gelu_tanh(x):
    return 0.5 * x * (1.0 + jnp.tanh(0.7978845608028654 * (x + 0.044715 * (x * x * x))))


def _softplus(x):
    return jnp.maximum(x, 0.0) + jnp.log(1.0 + jnp.exp(-jnp.abs(x)))


def _expm1(x):
    small = jnp.abs(x) < 0.01
    series = x * (1.0 + x * (0.5 + x * (1.0 / 6.0 + x * (1.0 / 24.0))))
    return jnp.where(small, series, jnp.exp(jnp.where(small, 0.0, x)) - 1.0)


def _f_pre(h, g):
    return (_rms(h, g),)


def _f_pre_id(h, g):
    return h, _rms(h, g)


def _f_post_pre(h, m, g_post, g_pre):
    hb = h + _rms(m, g_post)
    return hb, _rms(hb, g_pre)


def _f_post(h, m, g_post):
    hc = h + _rms(m, g_post)
    return hc, hc


def _f_swiglu(g, u):
    return (_silu(g) * u,)


def _f_ple_pre(h, t, e, g_ple, g_pre):
    hd = h + _rms(_sigmoid(t) * e, g_ple)
    return hd, _rms(hd, g_pre)


def _f_ple(h, t, e, g_ple):
    return (h + _rms(_sigmoid(t) * e, g_ple),)


def _f_rg_gate(rid, ra, ix, xc, ba, bx, lam):
    r = _sigmoid(ra + ba)
    i = _sigmoid(ix + bx)
    log_a = -RG_C * r * _softplus(-lam)
    a = jnp.exp(log_a)
    mult = jnp.sqrt(-_expm1(2.0 * log_a))
    mult = jnp.where(rid == 0, 1.0, mult)
    return a, xc * i * mult


def _f_rg_out(gate, hs):
    return (_gelu_tanh(gate) * hs,)


def _loss_head(y, target, tr=256):
    m, d = y.shape

    def body(y_ref, t_ref, dy_ref, l_ref):
        err = y_ref[...] - t_ref[...]
        dy_ref[...] = err * (1.0 / d)

        @pl.when(pl.program_id(0) == 0)
        def _():
            l_ref[...] = jnp.zeros_like(l_ref)

        part = 0.5 * jnp.sum(jnp.mean(err * err, axis=-1, keepdims=True), axis=0, keepdims=True)
        l_ref[...] += jnp.broadcast_to(part, l_ref.shape)

    dy, l = pl.pallas_call(
        body, name="loss_head", grid=(m // tr,),
        out_shape=[jax.ShapeDtypeStruct((m, d), f32), jax.ShapeDtypeStruct((SUBLANES, LANES), f32)],
        in_specs=[pl.BlockSpec((tr, d), lambda i: (i, 0))] * 2,
        out_specs=[pl.BlockSpec((tr, d), lambda i: (i, 0)), pl.BlockSpec((SUBLANES, LANES), lambda i: (0, 0))],
        compiler_params=_params("arbitrary"),
    )(y, target)
    return dy, l[0, 0]


def _hilo_dot(a, u):
    hi = a.astype(bf16)
    lo = (a - hi.astype(f32)).astype(bf16)
    return (lax.dot_general(hi, u, NN, preferred_element_type=f32)
            + lax.dot_general(lo, u, NN, preferred_element_type=f32))


def _sb_scores(q_t, k_s, tri, diag):
    z = lax.dot_general(q_t, k_s, NT, preferred_element_type=f32)
    sp = jnp.log(1.0 + jnp.exp(-jnp.abs(z)))
    a = -(jnp.maximum(z, 0.0) + sp)
    b = a + z
    if diag:
        a = jnp.where(tri, a, 0.0)
    return a, b


def _sb_masks():
    row = lax.broadcasted_iota(jnp.int32, (SB_BLOCK, SB_BLOCK), 0)
    col = lax.broadcasted_iota(jnp.int32, (SB_BLOCK, SB_BLOCK), 1)
    tri = col < row
    u_excl = (row > col).astype(bf16)
    u_incl = (row >= col).astype(bf16)
    return tri, u_excl, u_incl


def _sb_fwd(proj):
    s = proj.shape[0]
    nblk = s // SB_BLOCK
    scale = HEAD_DIM ** -0.5

    def body(q_ref, k_ref, v_ref, o_ref, qs, ks, vs):
        qs[...] = (q_ref[...] * scale).astype(bf16)
        ks[...] = k_ref[...].astype(bf16)
        vs[...] = v_ref[...].astype(bf16)
        tri, u_excl, _ = _sb_masks()

        def block(q_t, s_idx, carry, diag):
            ra, acc = carry
            rows = pl.ds(pl.multiple_of(s_idx * SB_BLOCK, SB_BLOCK), SB_BLOCK)
            a, b = _sb_scores(q_t, ks[rows, :], tri, diag)
            w = jnp.exp(b + _hilo_dot(a, u_excl) + ra)
            if diag:
                w = jnp.where(tri, w, 0.0)
            acc = acc + lax.dot_general(w.astype(bf16), vs[rows, :], NN, preferred_element_type=f32)
            return ra + jnp.sum(a, axis=1, keepdims=True), acc

        def query_block(t, _):
            rows = pl.ds(pl.multiple_of(t * SB_BLOCK, SB_BLOCK), SB_BLOCK)
            q_t = qs[rows, :]
            init = (jnp.zeros((SB_BLOCK, 1), f32), jnp.zeros((SB_BLOCK, HEAD_DIM), f32))
            carry = block(q_t, t, init, True)
            carry = lax.fori_loop(0, t, lambda i, c: block(q_t, t - 1 - i, c, False), carry)
            o_ref[rows, :] = carry[1]
            return 0

        lax.fori_loop(0, nblk, query_block, 0)

    col = lambda off: pl.BlockSpec((s, HEAD_DIM), lambda h: (0, off + h))
    return pl.pallas_call(
        body, name="sb_fwd", grid=(HEADS,),
        out_shape=jax.ShapeDtypeStruct((s, HEADS * HEAD_DIM), f32),
        in_specs=[col(0), col(HEADS), col(2 * HEADS)],
        out_specs=col(0),
        scratch_shapes=[pltpu.VMEM((s, HEAD_DIM), bf16)] * 3,
        compiler_params=_params("parallel"),
    )(proj, proj, proj)


def _sb_bwd(proj, o, do):
    s = proj.shape[0]
    nblk = s // SB_BLOCK
    scale = HEAD_DIM ** -0.5

    def body(q_ref, k_ref, v_ref, o_ref, do_ref, dq_ref, dk_ref, dv_ref, qs, ks, vs, dos, dk_acc, dv_acc):
        qs[...] = (q_ref[...] * scale).astype(bf16)
        ks[...] = k_ref[...].astype(bf16)
        vs[...] = v_ref[...].astype(bf16)
        dos[...] = do_ref[...].astype(bf16)
        dk_acc[...] = jnp.zeros_like(dk_acc)
        dv_acc[...] = jnp.zeros_like(dv_acc)
        tri, u_excl, u_incl = _sb_masks()

        def block(q_t, do_t, d_t, s_idx, carry, diag):
            ra, rg, dq = carry
            rows = pl.ds(pl.multiple_of(s_idx * SB_BLOCK, SB_BLOCK), SB_BLOCK)
            k_s = ks[rows, :]
            v_s = vs[rows, :]
            a, b = _sb_scores(q_t, k_s, tri, diag)
            w = jnp.exp(b + _hilo_dot(a, u_excl) + ra)
            if diag:
                w = jnp.where(tri, w, 0.0)
            dw = lax.dot_general(do_t, v_s, NT, preferred_element_type=f32)
            wb = w.astype(bf16)
            g = wb.astype(f32) * dw
            da = d_t - (_hilo_dot(g, u_incl) + rg)
            dz = g * jnp.exp(a) - da * jnp.exp(b)
            if diag:
                dz = jnp.where(tri, dz, 0.0)
            dzb = dz.astype(bf16)
            dq = dq + lax.dot_general(dzb, k_s, NN, preferred_element_type=f32)
            dk_acc[rows, :] += lax.dot_general(dzb, q_t, TN, preferred_element_type=f32)
            dv_acc[rows, :] += lax.dot_general(wb, do_t, TN, preferred_element_type=f32)
            return ra + jnp.sum(a, axis=1, keepdims=True), rg + jnp.sum(g, axis=1, keepdims=True), dq

        def query_block(t, _):
            rows = pl.ds(pl.multiple_of(t * SB_BLOCK, SB_BLOCK), SB_BLOCK)
            q_t = qs[rows, :]
            do_t = dos[rows, :]
            d_t = jnp.sum(do_t.astype(f32) * o_ref[rows, :], axis=1, keepdims=True)
            zero = jnp.zeros((SB_BLOCK, 1), f32)
            carry = block(q_t, do_t, d_t, t, (zero, zero, jnp.zeros((SB_BLOCK, HEAD_DIM), f32)), True)
            carry = lax.fori_loop(0, t, lambda i, c: block(q_t, do_t, d_t, t - 1 - i, c, False), carry)
            dq_ref[rows, :] = (carry[2] * scale).astype(dq_ref.dtype)
            return 0

        lax.fori_loop(0, nblk, query_block, 0)
        dk_ref[...] = dk_acc[...].astype(dk_ref.dtype)
        dv_ref[...] = dv_acc[...].astype(dv_ref.dtype)

    col = lambda off: pl.BlockSpec((s, HEAD_DIM), lambda h: (0, off + h))
    out = jax.ShapeDtypeStruct((s, HEADS * HEAD_DIM), bf16)
    return pl.pallas_call(
        body, name="sb_bwd", grid=(HEADS,),
        out_shape=[out, out, out],
        in_specs=[col(0), col(HEADS), col(2 * HEADS), col(0), col(0)],
        out_specs=[col(0), col(0), col(0)],
        scratch_shapes=[pltpu.VMEM((s, HEAD_DIM), bf16)] * 4 + [pltpu.VMEM((s, HEAD_DIM), f32)] * 2,
        compiler_params=_params("parallel"),
    )(proj, proj, proj, o, do)


def _hg_sub(state_t, hq, hf, hi, lb):
    rid = lax.broadcasted_iota(jnp.int32, (HG_SUB, 1), 0)
    f = lb + (1.0 - lb) * _sigmoid(hf)
    q = _silu(hq)
    k = 1.0 - f
    lf = jnp.log(f)
    r2 = lax.broadcasted_iota(jnp.int32, (HG_SUB, HG_SUB), 0)
    c2 = lax.broadcasted_iota(jnp.int32, (HG_SUB, HG_SUB), 1)
    cum = lax.dot_general((c2 <= r2).astype(f32), lf, NN, precision=HIGHEST, preferred_element_type=f32)
    o = lax.dot_general(q * jnp.exp(cum), state_t, NT, precision=HIGHEST, preferred_element_type=f32)
    for s in range(HG_SUB):
        pick = rid == s
        row = lambda x: jnp.sum(jnp.where(pick, x, 0.0), axis=0, keepdims=True)
        later = rid >= s
        dec = jnp.exp(jnp.where(later, cum - row(cum), 0.0))
        score = jnp.sum(jnp.where(later, q * row(k) * dec, 0.0), axis=1, keepdims=True)
        o = o + score * row(hi)
    last = jnp.sum(lf, axis=0, keepdims=True)
    k_dec = k * jnp.exp(last - cum)
    new_state = state_t * jnp.exp(last) + lax.dot_general(hi, k_dec, TN, precision=HIGHEST, preferred_element_type=f32)
    return new_state, o


def _hg_chunk(state_t, hq, hf, hi, hg, l0, l1, ng):
    lb = _sigmoid(l0 - l1)
    outs = []
    for j in range(len(hq)):
        state_t, o = _hg_sub(state_t, hq[j], hf[j], hi[j], lb)
        outs.append(_rms(o, ng) * _silu(hg[j]))
    return state_t, outs


def _hg_pieces(ref, c):
    n = HG_CHUNK // HG_SUB
    return [ref[pl.ds(pl.multiple_of(c * HG_CHUNK + j * HG_SUB, HG_SUB), HG_SUB), :] for j in range(n)]


def _hg_specs(s):
    col = lambda off: pl.BlockSpec((s, HEAD_DIM), lambda h: (0, off + h))
    lrow = pl.BlockSpec((1, HEAD_DIM), lambda h: (0, h))
    ng = pl.BlockSpec((1, HEAD_DIM), lambda h: (0, 0))
    return col, lrow, ng


def _hg_fwd(proj, logits, ng):
    s = proj.shape[0]
    nch = s // HG_CHUNK
    n = HG_CHUNK // HG_SUB

    def body(hq_ref, hf_ref, hi_ref, hg_ref, l0_ref, l1_ref, ng_ref, o_ref, st_ref):
        def chunk(c, state_t):
            st_ref[c] = state_t
            state_t, outs = _hg_chunk(state_t, _hg_pieces(hq_ref, c), _hg_pieces(hf_ref, c), _hg_pieces(hi_ref, c),
                                      _hg_pieces(hg_ref, c), l0_ref[...], l1_ref[...], ng_ref[...])
            for j in range(n):
                o_ref[pl.ds(pl.multiple_of(c * HG_CHUNK + j * HG_SUB, HG_SUB), HG_SUB), :] = outs[j]
            return state_t

        lax.fori_loop(0, nch, chunk, jnp.zeros((HEAD_DIM, HEAD_DIM), f32))

    col, lrow, ngs = _hg_specs(s)
    return pl.pallas_call(
        body, name="hg_fwd", grid=(HEADS,),
        out_shape=[jax.ShapeDtypeStruct((s, HEADS * HEAD_DIM), f32),
                   jax.ShapeDtypeStruct((HEADS, nch, HEAD_DIM, HEAD_DIM), f32)],
        in_specs=[col(3 * HEADS), col(4 * HEADS), col(5 * HEADS), col(6 * HEADS), lrow, lrow, ngs],
        out_specs=[col(0), pl.BlockSpec((None, nch, HEAD_DIM, HEAD_DIM), lambda h: (h, 0, 0, 0))],
        compiler_params=_params("parallel"),
    )(proj, proj, proj, proj, logits[0:1], logits[1:2], ng)


def _hg_bwd(proj, logits, ng, states, do, do_off):
    s = proj.shape[0]
    nch = s // HG_CHUNK
    n = HG_CHUNK // HG_SUB

    def body(hq_ref, hf_ref, hi_ref, hg_ref, l0_ref, l1_ref, ng_ref, st_ref, do_ref,
             dq_ref, df_ref, di_ref, dg_ref, dl0_ref, dl1_ref, dng_ref):
        dl0_ref[...] = jnp.zeros_like(dl0_ref)
        dl1_ref[...] = jnp.zeros_like(dl1_ref)
        dng_ref[...] = jnp.zeros_like(dng_ref)

        def chunk(i, dstate):
            c = nch - 1 - i
            args = (st_ref[c], _hg_pieces(hq_ref, c), _hg_pieces(hf_ref, c), _hg_pieces(hi_ref, c),
                    _hg_pieces(hg_ref, c), l0_ref[...], l1_ref[...], ng_ref[...])
            _, vjp = jax.vjp(_hg_chunk, *args)
            dst, dq, df, di, dg, dl0, dl1, dng = vjp((dstate, _hg_pieces(do_ref, c)))
            for j in range(n):
                rows = pl.ds(pl.multiple_of(c * HG_CHUNK + j * HG_SUB, HG_SUB), HG_SUB)
                dq_ref[rows, :] = dq[j].astype(dq_ref.dtype)
                df_ref[rows, :] = df[j].astype(df_ref.dtype)
                di_ref[rows, :] = di[j].astype(di_ref.dtype)
                dg_ref[rows, :] = dg[j].astype(dg_ref.dtype)
            dl0_ref[...] += dl0
            dl1_ref[...] += dl1
            dng_ref[...] += dng
            return dst

        lax.fori_loop(0, nch, chunk, jnp.zeros((HEAD_DIM, HEAD_DIM), f32))

    col, lrow, ngs = _hg_specs(s)
    big = jax.ShapeDtypeStruct((s, HEADS * HEAD_DIM), bf16)
    vec = jax.ShapeDtypeStruct((1, HEADS * HEAD_DIM), f32)
    vspec = pl.BlockSpec((1, HEAD_DIM), lambda h: (0, h))
    dq, df, di, dg, dl0, dl1, dng = pl.pallas_call(
        body, name="hg_bwd", grid=(HEADS,),
        out_shape=[big, big, big, big, vec, vec, vec],
        in_specs=[col(3 * HEADS), col(4 * HEADS), col(5 * HEADS), col(6 * HEADS), lrow, lrow, ngs,
                  pl.BlockSpec((None, nch, HEAD_DIM, HEAD_DIM), lambda h: (h, 0, 0, 0)), col(do_off)],
        out_specs=[col(0), col(0), col(0), col(0), vspec, vspec, vspec],
        compiler_params=_params("parallel"),
    )(proj, proj, proj, proj, logits[0:1], logits[1:2], ng, states, do)
    return (jnp.concatenate([dq, df, di, dg], axis=1), jnp.concatenate([dl0, dl1], axis=0),
            dng.reshape(HEADS, HEAD_DIM))


def _shift_rows(x, k, down):
    if k == 0:
        return x
    s = x.shape[0]
    row = lax.broadcasted_iota(jnp.int32, x.shape, 0)
    if down:
        return jnp.where(row < k, 0.0, pltpu.roll(x, k, 0))
    return jnp.where(row >= s - k, 0.0, pltpu.roll(x, s - k, 0))


def _conv_fwd(proj, w, b):
    s, w2 = proj.shape
    width = w2 // 2
    nb = width // RG_BLOCK

    def body(x_ref, w_ref, b_ref, y_ref, yb_ref):
        x = x_ref[...]
        y = jnp.broadcast_to(b_ref[...], x.shape)
        for k in range(CONV_WIDTH):
            y = y + _shift_rows(x, k, True) * w_ref[pl.ds(k, 1), :]
        y_ref[...] = y
        yb_ref[...] = y.astype(bf16)

    blk = pl.BlockSpec((s, RG_BLOCK), lambda j: (0, j))
    return pl.pallas_call(
        body, name="conv_fwd", grid=(nb,),
        out_shape=[jax.ShapeDtypeStruct((s, width), f32), jax.ShapeDtypeStruct((s, width), bf16)],
        in_specs=[pl.BlockSpec((s, RG_BLOCK), lambda j: (0, nb + j)),
                  pl.BlockSpec((CONV_WIDTH, RG_BLOCK), lambda j: (0, j)), pl.BlockSpec((1, RG_BLOCK), lambda j: (0, j))],
        out_specs=[blk, blk],
        compiler_params=_params("parallel"),
    )(proj, w, b)


def _conv_bwd(proj, w, dy_a, dy_b):
    s, w2 = proj.shape
    width = w2 // 2
    nb = width // RG_BLOCK

    def body(x_ref, w_ref, dya_ref, dyb_ref, dx_ref, dw_ref, db_ref):
        x = x_ref[...]
        dy_v = dya_ref[...] + dyb_ref[...]
        dx = jnp.zeros_like(x)
        for k in range(CONV_WIDTH):
            dx = dx + _shift_rows(dy_v, k, False) * w_ref[pl.ds(k, 1), :]
            dw_ref[pl.ds(k, 1), :] = jnp.sum(dy_v * _shift_rows(x, k, True), axis=0, keepdims=True)
        dx_ref[...] = dx.astype(dx_ref.dtype)
        db_ref[...] = jnp.sum(dy_v, axis=0, keepdims=True)

    blk = pl.BlockSpec((s, RG_BLOCK), lambda j: (0, j))
    wblk = pl.BlockSpec((CONV_WIDTH, RG_BLOCK), lambda j: (0, j))
    bblk = pl.BlockSpec((1, RG_BLOCK), lambda j: (0, j))
    return pl.pallas_call(
        body, name="conv_bwd", grid=(nb,),
        out_shape=[jax.ShapeDtypeStruct((s, width), bf16), jax.ShapeDtypeStruct((CONV_WIDTH, width), f32),
                   jax.ShapeDtypeStruct((1, width), f32)],
        in_specs=[pl.BlockSpec((s, RG_BLOCK), lambda j: (0, nb + j)), wblk, blk, blk],
        out_specs=[blk, wblk, bblk],
        compiler_params=_params("parallel"),
    )(proj, w, dy_a, dy_b)


def _rg_rows(w_ref):
    return jnp.concatenate([w_ref[q] for q in range(N_CHIPS)], axis=0)


def _bd_fwd(xb, wa, wx):
    s, width = xb.shape
    nb = width // RG_BLOCK

    def body(x_ref, wa_ref, wx_ref, ra_ref, ix_ref):
        x = x_ref[...]
        ra_ref[...] = lax.dot_general(x, _rg_rows(wa_ref), NN, preferred_element_type=f32)
        ix_ref[...] = lax.dot_general(x, _rg_rows(wx_ref), NN, preferred_element_type=f32)

    blk = pl.BlockSpec((s, RG_BLOCK), lambda j: (0, j))
    wblk = pl.BlockSpec((N_CHIPS, None, RG_BLOCK // N_CHIPS, RG_BLOCK), lambda j: (0, j, 0, 0))
    out = jax.ShapeDtypeStruct((s, width), f32)
    return pl.pallas_call(
        body, name="bd_fwd", grid=(nb,), out_shape=[out, out],
        in_specs=[blk, wblk, wblk], out_specs=[blk, blk],
        compiler_params=_params("parallel"),
    )(xb, wa, wx)


def _bd_bwd(xb, wa, wx, dra, dix):
    s, width = xb.shape
    nb = width // RG_BLOCK
    rq = RG_BLOCK // N_CHIPS

    def body(x_ref, wa_ref, wx_ref, dra_ref, dix_ref, dx_ref, dwa_ref, dwx_ref):
        x = x_ref[...]
        dra_v = dra_ref[...]
        dix_v = dix_ref[...]
        dx_ref[...] = (lax.dot_general(dra_v, _rg_rows(wa_ref), NT, preferred_element_type=f32)
                       + lax.dot_general(dix_v, _rg_rows(wx_ref), NT, preferred_element_type=f32))
        dwa = lax.dot_general(x, dra_v, TN, preferred_element_type=f32)
        dwx = lax.dot_general(x, dix_v, TN, preferred_element_type=f32)
        for q in range(N_CHIPS):
            dwa_ref[q] = dwa[q * rq:(q + 1) * rq, :].astype(dwa_ref.dtype)
            dwx_ref[q] = dwx[q * rq:(q + 1) * rq, :].astype(dwx_ref.dtype)

    blk = pl.BlockSpec((s, RG_BLOCK), lambda j: (0, j))
    wblk = pl.BlockSpec((N_CHIPS, None, rq, RG_BLOCK), lambda j: (0, j, 0, 0))
    wout = jax.ShapeDtypeStruct((N_CHIPS, nb, rq, RG_BLOCK), bf16)
    return pl.pallas_call(
        body, name="bd_bwd", grid=(nb,),
        out_shape=[jax.ShapeDtypeStruct((s, width), f32), wout, wout],
        in_specs=[blk, wblk, wblk, blk, blk], out_specs=[blk, wblk, wblk],
        compiler_params=_params("parallel"),
    )(xb, wa, wx, dra, dix)


def _scan_tiles(a_ref, u_ref, out_ref, n_tiles, reverse):
    cb = a_ref.shape[1]
    row = lax.broadcasted_iota(jnp.int32, (SUBLANES, cb), 0)

    def tile(i, carry):
        t = (n_tiles - 1 - i) if reverse else i
        rows = pl.ds(pl.multiple_of(t * SUBLANES, SUBLANES), SUBLANES)
        a = a_ref[rows, :]
        b = u_ref[rows, :]
        for d in (1, 2, 4):
            if reverse:
                valid = row < SUBLANES - d
                shift = SUBLANES - d
            else:
                valid = row >= d
                shift = d
            a_n = jnp.where(valid, pltpu.roll(a, shift, 0), 1.0)
            b_n = jnp.where(valid, pltpu.roll(b, shift, 0), 0.0)
            b = a * b_n + b
            a = a * a_n
        out = a * carry + b
        out_ref[rows, :] = out
        edge = t * SUBLANES + (0 if reverse else SUBLANES - 1)
        return jnp.broadcast_to(out_ref[pl.ds(edge, 1), :], (SUBLANES, cb))

    lax.fori_loop(0, n_tiles, tile, jnp.zeros((SUBLANES, cb), f32))


def _scan_fwd(a, u):
    s, width = a.shape

    def body(a_ref, u_ref, h_ref):
        _scan_tiles(a_ref, u_ref, h_ref, s // SUBLANES, False)

    blk = pl.BlockSpec((s, RG_BLOCK), lambda j: (0, j))
    return pl.pallas_call(
        body, name="scan_fwd", grid=(width // RG_BLOCK,), out_shape=jax.ShapeDtypeStruct((s, width), f32),
        in_specs=[blk, blk], out_specs=blk, compiler_params=_params("parallel"),
    )(a, u)


def _scan_bwd(a, h, dh):
    s, width = a.shape

    def body(a_ref, h_ref, dh_ref, du_ref, da_ref, a_next):
        a_next[...] = _shift_rows(a_ref[...], 1, False)
        _scan_tiles(a_next, dh_ref, du_ref, s // SUBLANES, True)
        da_ref[...] = du_ref[...] * _shift_rows(h_ref[...], 1, True)

    blk = pl.BlockSpec((s, RG_BLOCK), lambda j: (0, j))
    out = jax.ShapeDtypeStruct((s, width), f32)
    return pl.pallas_call(
        body, name="scan_bwd", grid=(width // RG_BLOCK,), out_shape=[out, out],
        in_specs=[blk, blk, blk], out_specs=[blk, blk],
        scratch_shapes=[pltpu.VMEM((s, RG_BLOCK), f32)],
        compiler_params=_params("parallel"),
    )(a, h, dh)


EW_TILE_BYTES = 1 << 20


def _ew_rows(rows, width):
    limit = max(16, EW_TILE_BYTES // (4 * width))
    t = (min(limit, rows) // 16) * 16
    while t >= 16:
        if rows % t == 0:
            return t
        t -= 16
    return rows


def _as2d(a):
    return a.reshape(-1, a.shape[-1])


def _ew(fn, ins, out_dtypes, name):
    shape = ins[0].shape
    ins2 = [_as2d(a) for a in ins]
    rows, width = ins2[0].shape
    res = _rowwise(fn, ins2, [], [(width, dt) for dt in out_dtypes], name, _ew_rows(rows, width))
    return [r.reshape(shape) for r in res]


def _f_adamw(w, g, m, v):
    m = ADAM_B1 * m + (1.0 - ADAM_B1) * g
    v = ADAM_B2 * v + (1.0 - ADAM_B2) * (g * g)
    m_hat = m / (1.0 - ADAM_B1 ** ADAM_STEP)
    v_hat = v / (1.0 - ADAM_B2 ** ADAM_STEP)
    delta = -ADAM_LR * (m_hat / (jnp.sqrt(v_hat) + ADAM_EPS) + ADAM_WD * w)
    return delta, m, v


ANY = pl.BlockSpec(memory_space=pl.ANY)


def _place():
    x, y, c = lax.axis_index("x"), lax.axis_index("y"), lax.axis_index("c")
    return x, y, c, [(1 - x, y), (x, 1 - y), (1 - x, 1 - y)]


def _rcopy(src, dst, send_sem, recv_sem, dev):
    return pltpu.make_async_remote_copy(src_ref=src, dst_ref=dst, send_sem=send_sem, recv_sem=recv_sem,
                                        device_id=dev, device_id_type=MESH)


def _half(ref, h, kind, lead=0):
    ax = lead + (1 if kind == "rg" else 0)
    n = ref.shape[ax] // 2
    idx = [slice(None)] * ref.ndim
    idx[ax] = pl.ds(h * n, n)
    return ref.at[tuple(idx)]


def _gather_weights(slots, kinds):
    n = len(slots)

    def body(*refs):
        out = refs[n:2 * n]
        send_sems, recv_sems = refs[2 * n:]
        x, y, c, chips = _place()
        my_q = 2 * x + y
        sibling = (x, y, 1 - c)
        started = []
        for e in range(n):
            for j, chip in enumerate(chips):
                mine = _half(out[e].at[my_q], c, kinds[e])
                cp = _rcopy(mine, mine, send_sems.at[6 * e + j], recv_sems.at[6 * e + j], (*chip, c))
                cp.start()
                started.append(cp)
        for e in range(n):
            for j, chip in enumerate(chips):
                landed = _half(out[e].at[2 * chip[0] + chip[1]], c, kinds[e])
                _rcopy(landed, landed, send_sems.at[6 * e + j], recv_sems.at[6 * e + j], (*chip, c)).wait_recv()
                cp = _rcopy(landed, landed, send_sems.at[6 * e + 3 + j], recv_sems.at[6 * e + 3 + j], sibling)
                cp.start()
                started.append(cp)
        for e in range(n):
            for j, chip in enumerate(chips):
                passed = _half(out[e].at[2 * chip[0] + chip[1]], 1 - c, kinds[e])
                _rcopy(passed, passed, send_sems.at[6 * e + 3 + j], recv_sems.at[6 * e + 3 + j], sibling).wait_recv()
        for cp in started:
            cp.wait_send()

    return pl.pallas_call(
        body, name="gather_weights",
        out_shape=[jax.ShapeDtypeStruct(s.shape, s.dtype) for s in slots],
        in_specs=[ANY] * n, out_specs=[ANY] * n, input_output_aliases={e: e for e in range(n)},
        scratch_shapes=[pltpu.SemaphoreType.DMA((6 * n,)), pltpu.SemaphoreType.DMA((6 * n,))],
    )(*slots)


def _exchange_halves(grads, kinds):
    n = len(grads)

    def body(*refs):
        g, theirs = refs[:n], refs[n:2 * n]
        send_sems, recv_sems = refs[2 * n:]
        x, y, c, _ = _place()
        sent = [_rcopy(_half(g[e], 1 - c, kinds[e], 1), theirs[e], send_sems.at[e], recv_sems.at[e], (x, y, 1 - c))
                for e in range(n)]
        for cp in sent:
            cp.start()
        for cp in sent:
            cp.wait()

    def half_shape(s, kind):
        shp = list(s.shape)
        shp[2 if kind == "rg" else 1] //= 2
        return jax.ShapeDtypeStruct(tuple(shp), s.dtype)

    return pl.pallas_call(
        body, name="rs_core_exchange", out_shape=[half_shape(s, k) for s, k in zip(grads, kinds)],
        in_specs=[ANY] * n, out_specs=[ANY] * n,
        scratch_shapes=[pltpu.SemaphoreType.DMA((n,)), pltpu.SemaphoreType.DMA((n,))],
    )(*grads)


def _exchange_chips(parts):
    n = len(parts)

    def body(*refs):
        p, got = refs[:n], refs[n:2 * n]
        send_sems, recv_sems = refs[2 * n:]
        x, y, c, chips = _place()
        sent = [_rcopy(p[e].at[2 * chip[0] + chip[1]], got[e].at[j], send_sems.at[3 * e + j], recv_sems.at[3 * e + j],
                       (*chip, c))
                for e in range(n) for j, chip in enumerate(chips)]
        for cp in sent:
            cp.start()
        for cp in sent:
            cp.wait()

    return pl.pallas_call(
        body, name="rs_chip_exchange",
        out_shape=[jax.ShapeDtypeStruct((N_CHIPS - 1,) + s.shape[1:], s.dtype) for s in parts],
        in_specs=[ANY] * n, out_specs=[ANY] * n,
        scratch_shapes=[pltpu.SemaphoreType.DMA((3 * n,)), pltpu.SemaphoreType.DMA((3 * n,))],
    )(*parts)


def _share_halves(shards, kinds):
    n = len(shards)

    def body(*refs):
        out = refs[n:2 * n]
        send_sems, recv_sems = refs[2 * n:]
        x, y, c, _ = _place()
        sent = []
        for e in range(n):
            mine = _half(out[e], c, kinds[e])
            sent.append(_rcopy(mine, mine, send_sems.at[e], recv_sems.at[e], (x, y, 1 - c)))
        for cp in sent:
            cp.start()
        for e in range(n):
            other = _half(out[e], 1 - c, kinds[e])
            _rcopy(other, other, send_sems.at[e], recv_sems.at[e], (x, y, 1 - c)).wait_recv()
        for cp in sent:
            cp.wait_send()

    return pl.pallas_call(
        body, name="rs_share_halves", out_shape=[jax.ShapeDtypeStruct(s.shape, s.dtype) for s in shards],
        in_specs=[ANY] * n, out_specs=[ANY] * n, input_output_aliases={e: e for e in range(n)},
        scratch_shapes=[pltpu.SemaphoreType.DMA((n,)), pltpu.SemaphoreType.DMA((n,))],
    )(*shards)


def _tiles_call(fn, scalars, ins, outs, steps, name):
    n_in = len(ins)

    def body(s_ref, *refs):
        res = fn(*[r[...] for r in refs[:n_in]])
        for r, o in zip(res, refs[n_in:]):
            o[...] = r.astype(o.dtype)

    spec = lambda blk, idx: pl.BlockSpec(blk, lambda i, s, idx=idx: (idx(i, s), 0))
    return pl.pallas_call(
        body, name=name, out_shape=[o[0] for o in outs],
        grid_spec=pltpu.PrefetchScalarGridSpec(
            num_scalar_prefetch=1, grid=(steps,),
            in_specs=[spec(blk, idx) for _, blk, idx in ins], out_specs=[spec(blk, idx) for _, blk, idx in outs]),
        compiler_params=_params("arbitrary"),
    )(scalars, *[a for a, _, _ in ins])


def _groups(kind, shard_shape):
    if kind == "rg":
        return shard_shape[0], shard_shape[1], shard_shape[2]
    return 1, shard_shape[0], shard_shape[1]


def _half_tiles(kind, shard_shape):
    g, rg, width = _groups(kind, shard_shape)
    tr = _ew_rows(rg // 2, width)
    nt = (rg // 2) // tr
    in_whole = lambda i, c: (i // nt) * (rg // tr) + c * nt + i % nt
    return g, rg, width, tr, nt, in_whole


def _cast_into_slot(w, layer, kind, scalars):
    shard_shape = w.shape[1:]
    g, rg, width, tr, nt, _ = _half_tiles(kind, shard_shape)
    per_shard = g * rg // tr
    (buf,) = _tiles_call(
        lambda v: (v,), scalars, [(_as2d(w), (tr, width), lambda i, s: layer * per_shard + i)],
        [(jax.ShapeDtypeStruct((N_CHIPS * g * rg, width), bf16), (tr, width), lambda i, s: s[0] * per_shard + i)],
        per_shard, "cast_" + kind)
    return buf.reshape((N_CHIPS,) + shard_shape)


def _add_cores(grad, theirs, kind, scalars):
    g, rg, width, tr, nt, in_whole = _half_tiles(kind, grad.shape[1:])
    steps = N_CHIPS * g * nt
    (out,) = _tiles_call(
        lambda u, v: (u.astype(f32) + v.astype(f32),), scalars,
        [(_as2d(grad), (tr, width), lambda i, s: in_whole(i, s[1])), (_as2d(theirs), (tr, width), lambda i, s: i)],
        [(jax.ShapeDtypeStruct((steps * tr, width), bf16), (tr, width), lambda i, s: i)], steps, "rs_add_cores")
    return out.reshape(theirs.shape)


def _sum_chips(part, got, kind, scalars):
    shard_shape = list(part.shape[1:])
    shard_shape[1 if kind == "rg" else 0] *= 2
    g, rg, width, tr, nt, in_whole = _half_tiles(kind, shard_shape)
    steps = g * nt
    up = lambda t: t.astype(f32)
    g2 = _as2d(got)
    ins = [(_as2d(part), (tr, width), lambda i, s: s[0] * steps + i)]
    ins += [(g2, (tr, width), lambda i, s, j=j: j * steps + i) for j in range(N_CHIPS - 1)]
    (out,) = _tiles_call(
        lambda a, b, c, e: (((up(a) + up(b)) + up(c)) + up(e),), scalars, ins,
        [(jax.ShapeDtypeStruct((g * rg, width), f32), (tr, width), lambda i, s: in_whole(i, s[1]))], steps, "rs_add_chips")
    return out.reshape(shard_shape)


def _all_gather_rows(v, reduce):
    rows = v.shape[0]

    def body(v_ref, out_ref, *rest):
        if reduce:
            sum_ref, send_sems, recv_sems, local_sem = rest
        else:
            send_sems, recv_sems, local_sem = rest
        x, y, c, chips = _place()
        me, sibling = (x, y, c), (x, y, 1 - c)

        def blk(px, py, pc):
            return out_ref.at[pl.ds((4 * px + 2 * py + pc) * rows, rows), :]

        def copy(k, block, to, src=None):
            return _rcopy(blk(*block) if src is None else src, blk(*block), send_sems.at[k], recv_sems.at[k], to)

        mine = pltpu.make_async_copy(v_ref, blk(*me), local_sem)
        mine.start()
        first = [copy(0, me, sibling, src=v_ref)] + [copy(1 + j, me, (*chip, c), src=v_ref) for j, chip in enumerate(chips)]
        for cp in first:
            cp.start()
        passed = [copy(4 + j, (*chip, c), sibling) for j, chip in enumerate(chips)]
        for j, chip in enumerate(chips):
            copy(1 + j, (*chip, c), me).wait_recv()
            passed[j].start()
        copy(0, sibling, me).wait_recv()
        for j, chip in enumerate(chips):
            copy(4 + j, (*chip, 1 - c), me).wait_recv()
        for cp in first + passed:
            cp.wait_send()
        mine.wait()
        if reduce:
            acc = out_ref[pl.ds(0, rows), :]
            for d in range(1, 8):
                acc = acc + out_ref[pl.ds(d * rows, rows), :]
            sum_ref[...] = acc

    vm = pl.BlockSpec(memory_space=pltpu.VMEM)
    gathered = jax.ShapeDtypeStruct((8 * rows, LANES), f32)
    res = pl.pallas_call(
        body, name="all_reduce_small" if reduce else "all_gather_small",
        out_shape=[gathered, jax.ShapeDtypeStruct((rows, LANES), f32)] if reduce else [gathered],
        in_specs=[vm], out_specs=[vm, vm] if reduce else [vm],
        scratch_shapes=[pltpu.SemaphoreType.DMA((7,)), pltpu.SemaphoreType.DMA((7,)), pltpu.SemaphoreType.DMA],
    )(v)
    return res[1] if reduce else res[0]


def _pack(parts):
    flat = jnp.concatenate([p.reshape(-1).astype(f32) for p in parts])
    rows = -(-flat.shape[0] // (SUBLANES * LANES)) * SUBLANES
    return jnp.pad(flat, (0, rows * LANES - flat.shape[0])).reshape(rows, LANES)


def _unpack(packed, shapes):
    flat = packed.reshape(-1)
    out, off = [], 0
    for shp in shapes:
        size = 1
        for d in shp:
            size *= d
        out.append(flat[off:off + size].reshape(shp))
        off += size
    return out


WEIGHTS = ["mix_pre_g", "mix_post_g", "ffn_pre_g", "ffn_post_g", "ple_norm_g", "w_in_even", "w_out_even",
           "hg_lb_logits", "hg_norm_g", "w_in_odd", "conv_w", "conv_b", "rg_wa", "rg_ba", "rg_wx", "rg_bx",
           "rg_lambda", "w_out_odd", "w_gate_up", "w_down", "w_ple_up", "w_ple_gate"]
BIG = [("w_in_even", 0, "col"), ("w_out_even", 0, "row"), ("w_gate_up", 0, "col"), ("w_down", 0, "row"),
       ("w_ple_up", 0, "col"), ("w_ple_gate", 0, "row"), ("w_in_odd", 0, "col"), ("rg_wa", 0, "rg"),
       ("rg_wx", 0, "rg"), ("w_out_odd", 0, "row"), ("w_gate_up", 1, "col"), ("w_down", 1, "row"),
       ("w_ple_up", 1, "col"), ("w_ple_gate", 1, "row")]
BIG_NAMES = ["w_in_even", "w_out_even", "w_gate_up", "w_down", "w_ple_up", "w_ple_gate", "w_in_odd", "rg_wa", "rg_wx",
             "w_out_odd"]
SMALL_SHARDED = ["conv_w", "conv_b", "rg_ba", "rg_bx", "rg_lambda"]
REPLICATED = ["mix_pre_g", "mix_post_g", "ffn_pre_g", "ffn_post_g", "ple_norm_g", "hg_lb_logits", "hg_norm_g"]
TR = 256
TR_WIDE = 128


def _ffn_ple_fwd(h, m, l, gains, w_gu, w_d, w_pu, w_pg, pb, next_pre):
    d = h.shape[1]
    ff = w_d.shape[0]
    hb, nf = _rowwise(_f_post_pre, [h, m], [gains["mix_post_g"][l], gains["ffn_pre_g"][l]], [(d, f32), (d, bf16)],
                      f"post_mix{l}", TR)
    gu = _mm(nf, w_gu, "nn", f32, f"ffn_up{l}", b_shards=True)
    (act,) = _rowwise(_f_swiglu, [(gu, ff, 0), (gu, ff, 1)], [], [(ff, bf16)], f"swiglu{l}", TR_WIDE)
    f = _mm(act, w_d, "nn", f32, f"ffn_down{l}")
    hc, hcb = _rowwise(_f_post, [hb, f], [gains["ffn_post_g"][l]], [(d, f32), (d, bf16)], f"post_ffn{l}", TR)
    t = _mm(hcb, w_pg, "nn", f32, f"ple_gate{l}")
    e = _mm(pb, w_pu, "nn", f32, f"ple_up{l}", b_shards=True)
    if next_pre is None:
        (hd,) = _rowwise(_f_ple, [hc, t, e], [gains["ple_norm_g"][l]], [(d, f32)], f"ple{l}", TR)
        n_next = None
    else:
        hd, n_next = _rowwise(_f_ple_pre, [hc, t, e], [gains["ple_norm_g"][l], next_pre], [(d, f32), (d, bf16)],
                              f"ple{l}", TR)
    saved = dict(h=h, m=m, hb=hb, nf=nf, gu=gu, act=act, f=f, hc=hc, hcb=hcb, t=t, e=e)
    return hd, n_next, saved


def _ffn_ple_bwd(sv, l, gains, w_gu, w_d, w_pg, pb, next_pre, dhd, dn_next):
    ff = w_d.shape[0]
    if next_pre is None:
        (dhc, dt, de), (dg_ple,) = _rowwise_vjp(_f_ple, [sv["hc"], sv["t"], sv["e"]], [gains["ple_norm_g"][l]], [dhd],
                                                [(0, f32), (1, bf16), (2, bf16)], f"ple_bwd{l}", TR)
        dg_next = None
    else:
        (dhc, dt, de), (dg_ple, dg_next) = _rowwise_vjp(
            _f_ple_pre, [sv["hc"], sv["t"], sv["e"]], [gains["ple_norm_g"][l], next_pre], [dhd, dn_next],
            [(0, f32), (1, bf16), (2, bf16)], f"ple_bwd{l}", TR)
    gw = {}
    gw["w_ple_gate"] = _mm(sv["hcb"], dt, "tn", bf16, f"ple_gate_wgrad{l}")
    gw["w_ple_up"] = _mm(pb, de, "tn", bf16, f"ple_up_wgrad{l}", out_shards=True)
    dhcb = _mm(dt, w_pg, "nt", f32, f"ple_gate_dgrad{l}")
    (dhb, df), (dg_fpost,) = _rowwise_vjp(_f_post, [sv["hb"], sv["f"]], [gains["ffn_post_g"][l]], [dhc, dhcb],
                                          [(0, f32), (1, bf16)], f"post_ffn_bwd{l}", TR)
    gw["w_down"] = _mm(sv["act"], df, "tn", bf16, f"ffn_down_wgrad{l}")
    dact = _mm(df, w_d, "nt", f32, f"ffn_down_dgrad{l}")
    (dgu,), _ = _rowwise_vjp(_f_swiglu, [(sv["gu"], ff, 0), (sv["gu"], ff, 1)], [], [dact], [(0, bf16), (1, bf16)],
                             f"swiglu_bwd{l}", TR_WIDE, concat=True)
    gw["w_gate_up"] = _mm(sv["nf"], dgu, "tn", bf16, f"ffn_up_wgrad{l}", out_shards=True)
    dnf = _mm(dgu, w_gu, "nt", f32, f"ffn_up_dgrad{l}", b_shards=True)
    (dh, dm), (dg_post, dg_fpre) = _rowwise_vjp(
        _f_post_pre, [sv["h"], sv["m"]], [gains["mix_post_g"][l], gains["ffn_pre_g"][l]], [dhb, dnf],
        [(0, f32), (1, bf16)], f"post_mix_bwd{l}", TR)
    gg = dict(ple_norm_g=dg_ple, ffn_post_g=dg_fpost, mix_post_g=dg_post, ffn_pre_g=dg_fpre)
    return dh, dm, gw, gg, dg_next


def kernel(x, p, mix_pre_g, mix_post_g, ffn_pre_g, ffn_post_g, ple_norm_g, w_in_even, w_out_even, hg_lb_logits, hg_norm_g, w_in_odd, conv_w, conv_b, rg_wa, rg_ba, rg_wx, rg_bx, rg_lambda, w_out_odd, w_gate_up, w_down, w_ple_up, w_ple_gate, loss_target, m_mix_pre_g, m_mix_post_g, m_ffn_pre_g, m_ffn_post_g, m_ple_norm_g, m_w_in_even, m_w_out_even, m_hg_lb_logits, m_hg_norm_g, m_w_in_odd, m_conv_w, m_conv_b, m_rg_wa, m_rg_ba, m_rg_wx, m_rg_bx, m_rg_lambda, m_w_out_odd, m_w_gate_up, m_w_down, m_w_ple_up, m_w_ple_gate, v_mix_pre_g, v_mix_post_g, v_ffn_pre_g, v_ffn_post_g, v_ple_norm_g, v_w_in_even, v_w_out_even, v_hg_lb_logits, v_hg_norm_g, v_w_in_odd, v_conv_w, v_conv_b, v_rg_wa, v_rg_ba, v_rg_wx, v_rg_bx, v_rg_lambda, v_w_out_odd, v_w_gate_up, v_w_down, v_w_ple_up, v_w_ple_gate):
    given = dict(locals())
    w = {n: given[n] for n in WEIGHTS}
    mom = {n: given["m_" + n] for n in WEIGHTS}
    var = {n: given["v_" + n] for n in WEIGHTS}
    xi, yi = lax.axis_index("x"), lax.axis_index("y")
    my_q = 2 * xi + yi
    h0 = x[0]
    s, d = h0.shape
    n_layers = p.shape[0]
    gains = {n: [w[n][l:l + 1] for l in range(n_layers)]
             for n in ("mix_pre_g", "mix_post_g", "ffn_pre_g", "ffn_post_g", "ple_norm_g")}

    kinds = [k for _, _, k in BIG]
    place = jnp.stack([my_q, lax.axis_index("c")]).astype(jnp.int32)
    slots = [_cast_into_slot(w[n], l, k, place) for n, l, k in BIG]
    full = dict(zip([(n, l) for n, l, _ in BIG], _gather_weights(slots, kinds)))

    def rows_full(name, l):
        g = full[(name, l)]
        return g.reshape(g.shape[0] * g.shape[1], g.shape[2])

    small_shapes = [w[n].shape[1:] for n in SMALL_SHARDED]
    gathered = _all_gather_rows(_pack([w[n][0] for n in SMALL_SHARDED]), False)
    per_chip = [_unpack(gathered.reshape(8, -1)[2 * q], small_shapes) for q in range(N_CHIPS)]
    conv_w_f, conv_b_f, rg_ba_f, rg_bx_f, rg_lam_f = [
        jnp.concatenate([per_chip[q][i] for q in range(N_CHIPS)], axis=-1) for i in range(len(SMALL_SHARDED))]
    lru = conv_b_f.shape[0]
    conv_b_f, rg_ba_f, rg_bx_f, rg_lam_f = [v.reshape(1, lru) for v in (conv_b_f, rg_ba_f, rg_bx_f, rg_lam_f)]
    pb = [_ew(lambda v: (v,), [p[l, 0]], [bf16], f"cast_p{l}")[0] for l in range(n_layers)]

    (n0,) = _rowwise(_f_pre, [h0], [gains["mix_pre_g"][0]], [(d, bf16)], "pre_mix0", TR)
    proj0 = _mm(n0, full[("w_in_even", 0)], "nn", f32, "in_even", b_shards=True)
    a_out = _sb_fwd(proj0)
    b_out, hg_states = _hg_fwd(proj0, w["hg_lb_logits"], w["hg_norm_g"])
    mix0 = jnp.concatenate([a_out, b_out], axis=1).astype(bf16)
    m0 = _mm(mix0, rows_full("w_out_even", 0), "nn", f32, "out_even")
    h1, n1, sv0 = _ffn_ple_fwd(h0, m0, 0, gains, full[("w_gate_up", 0)], rows_full("w_down", 0), full[("w_ple_up", 0)],
                               rows_full("w_ple_gate", 0), pb[0], gains["mix_pre_g"][1])

    proj1 = _mm(n1, full[("w_in_odd", 0)], "nn", f32, "in_odd", b_shards=True)
    xc, xcb = _conv_fwd(proj1, conv_w_f, conv_b_f)
    ra, ix = _bd_fwd(xcb, full[("rg_wa", 0)], full[("rg_wx", 0)])
    gate_in = [ra, ix, xc], [rg_ba_f, rg_bx_f, rg_lam_f]
    a_rec, u_rec = _rowwise(_f_rg_gate, *gate_in, [(lru, f32), (lru, f32)], "rg_gate", TR_WIDE, row_ids=True)
    hs = _scan_fwd(a_rec, u_rec)
    out_in = [(proj1, lru, 0), hs]
    (yb,) = _rowwise(_f_rg_out, out_in, [], [(lru, bf16)], "rg_out", TR_WIDE)
    m1 = _mm(yb, rows_full("w_out_odd", 0), "nn", f32, "out_odd")
    h2, _, sv1 = _ffn_ple_fwd(h1, m1, 1, gains, full[("w_gate_up", 1)], rows_full("w_down", 1), full[("w_ple_up", 1)],
                              rows_full("w_ple_gate", 1), pb[1], None)
    dy, loss_local = _loss_head(h2, loss_target[0])
    loss = lax.psum(loss_local, ("x", "y", "c"))

    gbig = {}
    dh1, dm1, gw, gg1, _ = _ffn_ple_bwd(sv1, 1, gains, full[("w_gate_up", 1)], rows_full("w_down", 1),
                                        rows_full("w_ple_gate", 1), pb[1], None, dy, None)
    gbig.update({(n, 1): g for n, g in gw.items()})
    gbig[("w_out_odd", 0)] = _mm(yb, dm1, "tn", bf16, "out_odd_wgrad")
    dyo = _mm(dm1, rows_full("w_out_odd", 0), "nt", f32, "out_odd_dgrad")
    (dgate, dhs), _ = _rowwise_vjp(_f_rg_out, out_in, [], [dyo], [(0, bf16), (1, f32)], "rg_out_bwd", TR_WIDE)
    du, da = _scan_bwd(a_rec, hs, dhs)
    (dra, dix, dxc_direct), (g_ba, g_bx, g_lam) = _rowwise_vjp(
        _f_rg_gate, *gate_in, [da, du], [(0, bf16), (1, bf16), (2, f32)], "rg_gate_bwd", TR_WIDE, row_ids=True)
    dxc_gates, gbig[("rg_wa", 0)], gbig[("rg_wx", 0)] = _bd_bwd(xcb, full[("rg_wa", 0)], full[("rg_wx", 0)], dra, dix)
    dxb, g_conv_w, g_conv_b = _conv_bwd(proj1, conv_w_f, dxc_direct, dxc_gates)
    dproj1 = jnp.concatenate([dgate, dxb], axis=1)
    gbig[("w_in_odd", 0)] = _mm(n1, dproj1, "tn", bf16, "in_odd_wgrad", out_shards=True)
    dn1 = _mm(dproj1, full[("w_in_odd", 0)], "nt", f32, "in_odd_dgrad", b_shards=True)

    dh0, dm0, gw, gg0, g_pre1 = _ffn_ple_bwd(sv0, 0, gains, full[("w_gate_up", 0)], rows_full("w_down", 0),
                                             rows_full("w_ple_gate", 0), pb[0], gains["mix_pre_g"][1], dh1, dn1)
    gbig.update({(n, 0): g for n, g in gw.items()})
    gbig[("w_out_even", 0)] = _mm(mix0, dm0, "tn", bf16, "out_even_wgrad")
    dmix = _mm(dm0, rows_full("w_out_even", 0), "nt", f32, "out_even_dgrad")
    dq, dk, dv = _sb_bwd(proj0, a_out, dmix)
    dhg, g_logits, g_ng = _hg_bwd(proj0, w["hg_lb_logits"], w["hg_norm_g"], hg_states, dmix, HEADS)
    dproj0 = jnp.concatenate([dq, dk, dv, dhg], axis=1)
    gbig[("w_in_even", 0)] = _mm(n0, dproj0, "tn", bf16, "in_even_wgrad", out_shards=True)
    dn0 = _mm(dproj0, full[("w_in_even", 0)], "nt", f32, "in_even_dgrad", b_shards=True)
    (grad_x,), (g_pre0,) = _rowwise_vjp(_f_pre_id, [h0], [gains["mix_pre_g"][0]], [dh0, dn0], [(0, f32)], "pre_mix0_bwd", TR)

    def shard_major(g, kind):
        return g.reshape(N_CHIPS, g.shape[0] // N_CHIPS, g.shape[1]) if kind == "row" else g

    glist = [shard_major(gbig[(n, l)], k) for n, l, k in BIG]
    theirs = _exchange_halves(glist, kinds)
    chip_part = [_add_cores(g, t, k, place) for g, t, k in zip(glist, theirs, kinds)]
    got = _exchange_chips(chip_part)
    reduced = _share_halves([_sum_chips(p_, g, k, place) for p_, g, k in zip(chip_part, got, kinds)], kinds)
    by_layer = {(n, l): r for (n, l, _), r in zip(BIG, reduced)}
    grad_big = {n: jnp.stack([by_layer[(n, l)] for l in range(w[n].shape[0])]) for n in BIG_NAMES}

    g_small = dict(
        mix_pre_g=jnp.concatenate([g_pre0, g_pre1], 0), mix_post_g=jnp.concatenate([gg0["mix_post_g"], gg1["mix_post_g"]], 0),
        ffn_pre_g=jnp.concatenate([gg0["ffn_pre_g"], gg1["ffn_pre_g"]], 0),
        ffn_post_g=jnp.concatenate([gg0["ffn_post_g"], gg1["ffn_post_g"]], 0),
        ple_norm_g=jnp.concatenate([gg0["ple_norm_g"], gg1["ple_norm_g"]], 0),
        hg_lb_logits=g_logits, hg_norm_g=jnp.sum(g_ng, axis=0, keepdims=True),
        conv_w=g_conv_w, conv_b=g_conv_b, rg_ba=g_ba, rg_bx=g_bx, rg_lambda=g_lam)
    order = REPLICATED + SMALL_SHARDED
    summed = dict(zip(order, _unpack(_all_reduce_rows(_pack([g_small[n] for n in order])),
                                     [g_small[n].shape for n in order])))
    grad_small = {n: summed[n].reshape(w[n].shape) for n in REPLICATED}
    for n in SMALL_SHARDED:
        width = w[n].shape[-1]
        if n in ("rg_ba", "rg_bx"):
            full_g = summed[n].reshape(1, -1, RG_BLOCK)
        else:
            full_g = summed[n].reshape((1,) + g_small[n].shape) if n == "conv_w" else summed[n]
        grad_small[n] = lax.dynamic_slice_in_dim(full_g, my_q * width, width, axis=full_g.ndim - 1).reshape(w[n].shape)

    grads = {**grad_big, **grad_small}
    delta, new_m, new_v = {}, {}, {}
    for n in BIG_NAMES:
        delta[n], new_m[n], new_v[n] = _ew(_f_adamw, [w[n], grads[n], mom[n], var[n]], [f32, f32, f32], "adamw_" + n)
    small = REPLICATED + SMALL_SHARDED
    packed = [_pack([src[n] for n in small]) for src in (w, grads, mom, var)]
    outs = _ew(_f_adamw, packed, [f32, f32, f32], "adamw_small")
    for dst, o in zip((delta, new_m, new_v), outs):
        dst.update(zip(small, _unpack(o, [w[n].shape for n in small])))
    return (loss, grad_x.reshape(x.shape), *[grads[n] for n in WEIGHTS], *[delta[n] for n in WEIGHTS],
            *[new_m[n] for n in WEIGHTS], *[new_v[n] for n in WEIGHTS])


def _all_reduce_rows(v):
    return _all_gather_rows(v, True)
```

```python
import functools

import jax
import jax.numpy as jnp
from jax import lax
from jax.experimental import pallas as pl
from jax.experimental.pallas import tpu as pltpu

f32 = jnp.float32
bf16 = jnp.bfloat16
HIGHEST = lax.Precision.HIGHEST
MESH = pl.DeviceIdType.MESH

LANES = 128
SUBLANES = 8
MXU_WIDTH = 256
VMEM_LIMIT_BYTES = 56 * 1024 * 1024

RMS_EPS = 1e-6
HEADS = 8
HEAD_DIM = 128
SB_BLOCK = 128
HG_SUB = 16
HG_CHUNK = 64
RG_BLOCK = 256
CONV_WIDTH = 4
RG_C = 8.0
ADAM_LR, ADAM_B1, ADAM_B2, ADAM_EPS, ADAM_WD, ADAM_STEP = 0.001, 0.9, 0.999, 1e-08, 0.01, 10
N_CHIPS = 4

NN = (((1,), (0,)), ((), ()))
NT = (((1,), (1,)), ((), ()))
TN = (((0,), (0,)), ((), ()))


def _params(*sem):
    return pltpu.CompilerParams(dimension_semantics=sem, vmem_limit_bytes=VMEM_LIMIT_BYTES)


class _Comm:
    def __init__(self, arrays, out_shapes, aliases, n_sems, start, finish):
        self.arrays, self.out_shapes, self.aliases, self.n_sems = list(arrays), list(out_shapes), dict(aliases), n_sems
        self.start, self.finish = start, finish


def _grid_call(body, name, steps, out_shape, in_specs, out_specs, scratch_shapes, args, comm=None):
    n_in, n_out, n_scr = len(in_specs), len(out_shape), len(scratch_shapes)
    if comm is None:
        res = pl.pallas_call(body, name=name, grid=(steps,), out_shape=list(out_shape), in_specs=list(in_specs),
                             out_specs=list(out_specs), scratch_shapes=list(scratch_shapes),
                             compiler_params=_params("parallel"))(*args)
        return list(res), []
    c_in, c_out = len(comm.arrays), len(comm.out_shapes)
    any_spec = pl.BlockSpec(memory_space=pl.ANY)

    def wrapped(*refs):
        ins, refs = refs[:n_in], refs[n_in:]
        cins, refs = refs[:c_in], refs[c_in:]
        outs, refs = refs[:n_out], refs[n_out:]
        couts, refs = refs[:c_out], refs[c_out:]
        scr, (send_sems, recv_sems) = refs[:n_scr], refs[n_scr:]

        @pl.when(pl.program_id(0) == 0)
        def _():
            comm.start(cins, couts, send_sems, recv_sems)

        body(*ins, *outs, *scr)

        @pl.when(pl.program_id(0) == steps - 1)
        def _():
            comm.finish(cins, couts, send_sems, recv_sems)

    res = pl.pallas_call(
        wrapped, name=name, grid=(steps,), out_shape=list(out_shape) + comm.out_shapes,
        in_specs=list(in_specs) + [any_spec] * c_in, out_specs=list(out_specs) + [any_spec] * c_out,
        scratch_shapes=list(scratch_shapes) + [pltpu.SemaphoreType.DMA((comm.n_sems,))] * 2,
        input_output_aliases={n_in + i: n_out + o for i, o in comm.aliases.items()},
        compiler_params=_params("arbitrary"),
    )(*args, *comm.arrays)
    return list(res[:n_out]), list(res[n_out:])


def _comm_call(comm, name):
    c_in, c_out = len(comm.arrays), len(comm.out_shapes)
    any_spec = pl.BlockSpec(memory_space=pl.ANY)

    def body(*refs):
        ins, outs, (send_sems, recv_sems) = refs[:c_in], refs[c_in:c_in + c_out], refs[c_in + c_out:]
        comm.start(ins, outs, send_sems, recv_sems)
        comm.finish(ins, outs, send_sems, recv_sems)

    return pl.pallas_call(
        body, name=name, out_shape=comm.out_shapes, in_specs=[any_spec] * c_in, out_specs=[any_spec] * c_out,
        input_output_aliases=comm.aliases, scratch_shapes=[pltpu.SemaphoreType.DMA((comm.n_sems,))] * 2,
    )(*comm.arrays)


def _pick(dim, target, align=LANES):
    if dim <= target:
        return dim
    t = (target // align) * align
    while t >= align:
        if dim % t == 0:
            return t
        t -= align
    return dim


MM_FULL_K = 2816
MM_CHUNK_K = 1408
MM_VMEM_BUDGET = 44 * 1024 * 1024


def _mm_tiles(m, n_unit, k_unit, k_whole, out_bytes):
    tk = k_unit if (k_whole and k_unit <= MM_FULL_K) else _pick(k_unit, MM_CHUNK_K)
    tk = k_unit if tk < MXU_WIDTH else tk
    tm = m if m <= 2048 else _pick(m, 512)
    tn = _pick(n_unit, 2048 if (m > 2048 or tk < k_unit or not k_whole) else 1024)
    tn = n_unit if tn < MXU_WIDTH else tn
    need = lambda: 2 * (2 * tm * tk + 2 * tk * tn + out_bytes * tm * tn)
    while need() > MM_VMEM_BUDGET:
        if tn > 512 and (tn // 2) % LANES == 0 and n_unit % (tn // 2) == 0:
            tn //= 2
        elif tm > 256:
            tm //= 2
        else:
            break
    return tm, tn, tk


def _mm(a, b, mode, out_dtype, name, *, b_shards=False, out_shards=False):
    if mode == "nn":
        m, kc = a.shape
        n = b.shape[-1] * (N_CHIPS if b_shards else 1)
    elif mode == "nt":
        m, kc = a.shape
        n = b.shape[-2]
    else:
        kc, m = a.shape
        n = b.shape[-1]
    n_sh = n // N_CHIPS
    kc_sh = kc // N_CHIPS
    n_unit = n_sh if (out_shards or (b_shards and mode == "nn")) else n
    k_split = b_shards and mode == "nt"
    tm, tn, tk = _mm_tiles(m, n_unit, kc_sh if k_split else kc, not k_split, jnp.dtype(out_dtype).itemsize)
    nk = kc // tk
    assert nk == 1 or out_dtype == f32, "a chunked contraction accumulates in the f32 output block"
    dims = {"nn": NN, "nt": NT, "tn": TN}[mode]

    def body(a_ref, b_ref, o_ref):
        part = lax.dot_general(a_ref[...], b_ref[...], dims, preferred_element_type=f32)
        if nk == 1:
            o_ref[...] = part.astype(o_ref.dtype)
        else:
            k = pl.program_id(2)

            @pl.when(k == 0)
            def _():
                o_ref[...] = part

            @pl.when(k > 0)
            def _():
                o_ref[...] += part

    if mode == "tn":
        a_spec = pl.BlockSpec((tk, tm), lambda i, j, k: (k, i))
    else:
        a_spec = pl.BlockSpec((tm, tk), lambda i, j, k: (i, k))
    nb_sh = n_sh // tn
    if mode == "nn":
        if b_shards:
            b_spec = pl.BlockSpec((None, tk, tn), lambda i, j, k: (j // nb_sh, k, j % nb_sh))
        else:
            b_spec = pl.BlockSpec((tk, tn), lambda i, j, k: (k, j))
    elif mode == "nt":
        if b_shards:
            kb_sh = kc_sh // tk
            b_spec = pl.BlockSpec((None, tn, tk), lambda i, j, k: (k // kb_sh, j, k % kb_sh))
        else:
            b_spec = pl.BlockSpec((tn, tk), lambda i, j, k: (j, k))
    else:
        b_spec = pl.BlockSpec((tk, tn), lambda i, j, k: (k, j))
    if out_shards:
        out_shape = jax.ShapeDtypeStruct((N_CHIPS, m, n_sh), out_dtype)
        o_spec = pl.BlockSpec((None, tm, tn), lambda i, j, k: (j // nb_sh, i, j % nb_sh))
    else:
        out_shape = jax.ShapeDtypeStruct((m, n), out_dtype)
        o_spec = pl.BlockSpec((tm, tn), lambda i, j, k: (i, j))
    return pl.pallas_call(
        body, name=name, out_shape=out_shape, grid=(m // tm, n // tn, nk),
        in_specs=[a_spec, b_spec], out_specs=o_spec,
        compiler_params=_params("parallel", "parallel", "arbitrary"),
    )(a, b)


def _row_spec(entry, tr):
    if isinstance(entry, tuple):
        arr, width, cb = entry[:3]
        rb = (entry[3] // tr) if len(entry) > 3 else 0
        return arr, pl.BlockSpec((tr, width), lambda i, cb=cb, rb=rb: (i + rb, cb))
    return entry, pl.BlockSpec((tr, entry.shape[1]), lambda i: (i, 0))


def _whole_spec(arr):
    return pl.BlockSpec(arr.shape, lambda i: (0,) * arr.ndim)


def _rowwise(fn, rows, bcast, outs, name, tr, row_ids=False, m=None):
    arrs, specs = zip(*[_row_spec(e, tr) for e in rows])
    m = arrs[0].shape[0] if m is None else m
    nr, nb = len(rows), len(bcast)

    def body(*refs):
        vals = [r[...] for r in refs[:nr + nb]]
        if row_ids:
            rid = pl.program_id(0) * tr + lax.broadcasted_iota(jnp.int32, (tr, 1), 0)
            res = fn(rid, *vals)
        else:
            res = fn(*vals)
        for r, o in zip(res, refs[nr + nb:]):
            o[...] = r.astype(o.dtype)

    return pl.pallas_call(
        body, name=name, grid=(m // tr,),
        out_shape=[jax.ShapeDtypeStruct((m, w), dt) for w, dt in outs],
        in_specs=list(specs) + [_whole_spec(b) for b in bcast],
        out_specs=[pl.BlockSpec((tr, w), lambda i: (i, 0)) for w, _ in outs],
        compiler_params=_params("parallel"),
    )(*arrs, *bcast)


def _rowwise_vjp(fn, rows, bcast, cts, row_grads, name, tr, row_ids=False, concat=False):
    arrs, specs = zip(*[_row_spec(e, tr) for e in rows])
    ct_arrs, ct_specs = zip(*[_row_spec(e, tr) for e in cts])
    m = arrs[0].shape[0]
    nr, nb, nc, ng = len(rows), len(bcast), len(cts), len(row_grads)
    widths = [specs[idx].block_shape[1] for idx, _ in row_grads]

    def body(*refs):
        vals = [r[...] for r in refs[:nr + nb]]
        ctv = [r[...] for r in refs[nr + nb:nr + nb + nc]]
        out_refs = refs[nr + nb + nc:]
        if row_ids:
            rid = pl.program_id(0) * tr + lax.broadcasted_iota(jnp.int32, (tr, 1), 0)
            f = functools.partial(fn, rid)
        else:
            f = fn
        res, vjp = jax.vjp(f, *vals)
        grads = vjp(tuple(c.astype(r.dtype) for c, r in zip(ctv, res)))
        if concat:
            off = 0
            for (idx, _), w in zip(row_grads, widths):
                out_refs[0][:, off:off + w] = grads[idx].astype(out_refs[0].dtype)
                off += w
            b_refs = out_refs[1:]
        else:
            for (idx, _), o in zip(row_grads, out_refs[:ng]):
                o[...] = grads[idx].astype(o.dtype)
            b_refs = out_refs[ng:]

        @pl.when(pl.program_id(0) == 0)
        def _():
            for o in b_refs:
                o[...] = jnp.zeros_like(o)

        for j, o in enumerate(b_refs):
            o[...] += grads[nr + j].astype(f32)

    if concat:
        dt = row_grads[0][1]
        g_shapes = [jax.ShapeDtypeStruct((m, sum(widths)), dt)]
        g_specs = [pl.BlockSpec((tr, sum(widths)), lambda i: (i, 0))]
    else:
        g_shapes = [jax.ShapeDtypeStruct((m, w), dt) for (_, dt), w in zip(row_grads, widths)]
        g_specs = [pl.BlockSpec((tr, w), lambda i: (i, 0)) for w in widths]
    res = pl.pallas_call(
        body, name=name, grid=(m // tr,),
        out_shape=g_shapes + [jax.ShapeDtypeStruct(b.shape, f32) for b in bcast],
        in_specs=list(specs) + [_whole_spec(b) for b in bcast] + list(ct_specs),
        out_specs=g_specs + [_whole_spec(b) for b in bcast],
        compiler_params=_params("arbitrary"),
    )(*arrs, *bcast, *ct_arrs)
    n_g = len(g_shapes)
    return list(res[:n_g]), list(res[n_g:])


def _rms(x, g):
    return x * lax.rsqrt(jnp.mean(x * x, axis=-1, keepdims=True) + RMS_EPS) * g


def _sigmoid(x):
    return 1.0 / (1.0 + jnp.exp(-x))


def _silu(x):
    return x * _sigmoid(x)


def _gelu_tanh(x):
    return 0.5 * x * (1.0 + jnp.tanh(0.7978845608028654 * (x + 0.044715 * (x * x * x))))


def _softplus(x):
    return jnp.maximum(x, 0.0) + jnp.log(1.0 + jnp.exp(-jnp.abs(x)))


def _expm1(x):
    small = jnp.abs(x) < 0.01
    series = x * (1.0 + x * (0.5 + x * (1.0 / 6.0 + x * (1.0 / 24.0))))
    return jnp.where(small, series, jnp.exp(jnp.where(small, 0.0, x)) - 1.0)


def _f_pre(h, g):
    return (_rms(h, g),)


def _f_pre_id(h, g):
    return h, _rms(h, g)


def _f_post_pre(h, m, g_post, g_pre):
    hb = h + _rms(m, g_post)
    return hb, _rms(hb, g_pre)


def _f_post(h, m, g_post):
    hc = h + _rms(m, g_post)
    return hc, hc


def _f_swiglu(g, u):
    return (_silu(g) * u,)


def _f_ple_pre(h, t, e, g_ple, g_pre):
    hd = h + _rms(_sigmoid(t) * e, g_ple)
    return hd, _rms(hd, g_pre)


def _f_ple(h, t, e, g_ple):
    return (h + _rms(_sigmoid(t) * e, g_ple),)


def _f_rg_gate(rid, ra, ix, xc, ba, bx, lam):
    r = _sigmoid(ra + ba)
    i = _sigmoid(ix + bx)
    log_a = -RG_C * r * _softplus(-lam)
    a = jnp.exp(log_a)
    mult = jnp.sqrt(-_expm1(2.0 * log_a))
    mult = jnp.where(rid == 0, 1.0, mult)
    return a, xc * i * mult


def _f_rg_out(gate, hs):
    return (_gelu_tanh(gate) * hs,)


def _loss_head(y, target, tr=256):
    m, d = y.shape

    def body(y_ref, t_ref, dy_ref, l_ref):
        err = y_ref[...] - t_ref[...]
        dy_ref[...] = err * (1.0 / d)

        @pl.when(pl.program_id(0) == 0)
        def _():
            l_ref[...] = jnp.zeros_like(l_ref)

        part = 0.5 * jnp.sum(jnp.mean(err * err, axis=-1, keepdims=True), axis=0, keepdims=True)
        l_ref[...] += jnp.broadcast_to(part, l_ref.shape)

    dy, l = pl.pallas_call(
        body, name="loss_head", grid=(m // tr,),
        out_shape=[jax.ShapeDtypeStruct((m, d), f32), jax.ShapeDtypeStruct((SUBLANES, LANES), f32)],
        in_specs=[pl.BlockSpec((tr, d), lambda i: (i, 0))] * 2,
        out_specs=[pl.BlockSpec((tr, d), lambda i: (i, 0)), pl.BlockSpec((SUBLANES, LANES), lambda i: (0, 0))],
        compiler_params=_params("arbitrary"),
    )(y, target)
    return dy, l[0, 0]


def _hilo_dot(a, u):
    hi = a.astype(bf16)
    lo = (a - hi.astype(f32)).astype(bf16)
    return (lax.dot_general(hi, u, NN, preferred_element_type=f32)
            + lax.dot_general(lo, u, NN, preferred_element_type=f32))


def _sb_scores(q_t, k_s, tri, diag):
    z = lax.dot_general(q_t, k_s, NT, preferred_element_type=f32)
    sp = jnp.log(1.0 + jnp.exp(-jnp.abs(z)))
    a = -(jnp.maximum(z, 0.0) + sp)
    b = a + z
    if diag:
        a = jnp.where(tri, a, 0.0)
    return a, b


def _sb_masks():
    row = lax.broadcasted_iota(jnp.int32, (SB_BLOCK, SB_BLOCK), 0)
    col = lax.broadcasted_iota(jnp.int32, (SB_BLOCK, SB_BLOCK), 1)
    tri = col < row
    u_excl = (row > col).astype(bf16)
    u_incl = (row >= col).astype(bf16)
    return tri, u_excl, u_incl


def _sb_fwd(proj, comm=None):
    s = proj.shape[0]
    nblk = s // SB_BLOCK
    scale = HEAD_DIM ** -0.5

    def body(q_ref, k_ref, v_ref, o_ref, qs, ks, vs):
        qs[...] = (q_ref[...] * scale).astype(bf16)
        ks[...] = k_ref[...].astype(bf16)
        vs[...] = v_ref[...].astype(bf16)
        tri, u_excl, _ = _sb_masks()

        def block(q_t, s_idx, carry, diag):
            ra, acc = carry
            rows = pl.ds(pl.multiple_of(s_idx * SB_BLOCK, SB_BLOCK), SB_BLOCK)
            a, b = _sb_scores(q_t, ks[rows, :], tri, diag)
            w = jnp.exp(b + _hilo_dot(a, u_excl) + ra)
            if diag:
                w = jnp.where(tri, w, 0.0)
            acc = acc + lax.dot_general(w.astype(bf16), vs[rows, :], NN, preferred_element_type=f32)
            return ra + jnp.sum(a, axis=1, keepdims=True), acc

        def query_block(t, _):
            rows = pl.ds(pl.multiple_of(t * SB_BLOCK, SB_BLOCK), SB_BLOCK)
            q_t = qs[rows, :]
            init = (jnp.zeros((SB_BLOCK, 1), f32), jnp.zeros((SB_BLOCK, HEAD_DIM), f32))
            carry = block(q_t, t, init, True)
            carry = lax.fori_loop(0, t, lambda i, c: block(q_t, t - 1 - i, c, False), carry)
            o_ref[rows, :] = carry[1]
            return 0

        lax.fori_loop(0, nblk, query_block, 0)

    col = lambda off: pl.BlockSpec((s, HEAD_DIM), lambda h: (0, off + h))
    (o,), sent = _grid_call(
        body, "sb_fwd", HEADS, [jax.ShapeDtypeStruct((s, HEADS * HEAD_DIM), f32)],
        [col(0), col(HEADS), col(2 * HEADS)], [col(0)], [pltpu.VMEM((s, HEAD_DIM), bf16)] * 3,
        (proj, proj, proj), comm)
    return o, sent


def _sb_bwd(proj, o, do, comm=None):
    s = proj.shape[0]
    nblk = s // SB_BLOCK
    scale = HEAD_DIM ** -0.5

    def body(q_ref, k_ref, v_ref, o_ref, do_ref, dq_ref, dk_ref, dv_ref, qs, ks, vs, dos, dk_acc, dv_acc):
        qs[...] = (q_ref[...] * scale).astype(bf16)
        ks[...] = k_ref[...].astype(bf16)
        vs[...] = v_ref[...].astype(bf16)
        dos[...] = do_ref[...].astype(bf16)
        dk_acc[...] = jnp.zeros_like(dk_acc)
        dv_acc[...] = jnp.zeros_like(dv_acc)
        tri, u_excl, u_incl = _sb_masks()

        def block(q_t, do_t, d_t, s_idx, carry, diag):
            ra, rg, dq = carry
            rows = pl.ds(pl.multiple_of(s_idx * SB_BLOCK, SB_BLOCK), SB_BLOCK)
            k_s = ks[rows, :]
            v_s = vs[rows, :]
            a, b = _sb_scores(q_t, k_s, tri, diag)
            w = jnp.exp(b + _hilo_dot(a, u_excl) + ra)
            if diag:
                w = jnp.where(tri, w, 0.0)
            dw = lax.dot_general(do_t, v_s, NT, preferred_element_type=f32)
            wb = w.astype(bf16)
            g = wb.astype(f32) * dw
            da = d_t - (_hilo_dot(g, u_incl) + rg)
            dz = g * jnp.exp(a) - da * jnp.exp(b)
            if diag:
                dz = jnp.where(tri, dz, 0.0)
            dzb = dz.astype(bf16)
            dq = dq + lax.dot_general(dzb, k_s, NN, preferred_element_type=f32)
            dk_acc[rows, :] += lax.dot_general(dzb, q_t, TN, preferred_element_type=f32)
            dv_acc[rows, :] += lax.dot_general(wb, do_t, TN, preferred_element_type=f32)
            return ra + jnp.sum(a, axis=1, keepdims=True), rg + jnp.sum(g, axis=1, keepdims=True), dq

        def query_block(t, _):
            rows = pl.ds(pl.multiple_of(t * SB_BLOCK, SB_BLOCK), SB_BLOCK)
            q_t = qs[rows, :]
            do_t = dos[rows, :]
            d_t = jnp.sum(do_t.astype(f32) * o_ref[rows, :], axis=1, keepdims=True)
            zero = jnp.zeros((SB_BLOCK, 1), f32)
            carry = block(q_t, do_t, d_t, t, (zero, zero, jnp.zeros((SB_BLOCK, HEAD_DIM), f32)), True)
            carry = lax.fori_loop(0, t, lambda i, c: block(q_t, do_t, d_t, t - 1 - i, c, False), carry)
            dq_ref[rows, :] = (carry[2] * scale).astype(dq_ref.dtype)
            return 0

        lax.fori_loop(0, nblk, query_block, 0)
        dk_ref[...] = dk_acc[...].astype(dk_ref.dtype)
        dv_ref[...] = dv_acc[...].astype(dv_ref.dtype)

    col = lambda off: pl.BlockSpec((s, HEAD_DIM), lambda h: (0, off + h))
    out = jax.ShapeDtypeStruct((s, HEADS * HEAD_DIM), bf16)
    return _grid_call(
        body, "sb_bwd", HEADS, [out, out, out], [col(0), col(HEADS), col(2 * HEADS), col(0), col(0)],
        [col(0), col(0), col(0)], [pltpu.VMEM((s, HEAD_DIM), bf16)] * 4 + [pltpu.VMEM((s, HEAD_DIM), f32)] * 2,
        (proj, proj, proj, o, do), comm)


def _hg_sub(state_t, hq, hf, hi, lb):
    rid = lax.broadcasted_iota(jnp.int32, (HG_SUB, 1), 0)
    f = lb + (1.0 - lb) * _sigmoid(hf)
    q = _silu(hq)
    k = 1.0 - f
    lf = jnp.log(f)
    r2 = lax.broadcasted_iota(jnp.int32, (HG_SUB, HG_SUB), 0)
    c2 = lax.broadcasted_iota(jnp.int32, (HG_SUB, HG_SUB), 1)
    cum = lax.dot_general((c2 <= r2).astype(f32), lf, NN, precision=HIGHEST, preferred_element_type=f32)
    o = lax.dot_general(q * jnp.exp(cum), state_t, NT, precision=HIGHEST, preferred_element_type=f32)
    for s in range(HG_SUB):
        pick = rid == s
        row = lambda x: jnp.sum(jnp.where(pick, x, 0.0), axis=0, keepdims=True)
        later = rid >= s
        dec = jnp.exp(jnp.where(later, cum - row(cum), 0.0))
        score = jnp.sum(jnp.where(later, q * row(k) * dec, 0.0), axis=1, keepdims=True)
        o = o + score * row(hi)
    last = jnp.sum(lf, axis=0, keepdims=True)
    k_dec = k * jnp.exp(last - cum)
    new_state = state_t * jnp.exp(last) + lax.dot_general(hi, k_dec, TN, precision=HIGHEST, preferred_element_type=f32)
    return new_state, o


def _hg_chunk(state_t, hq, hf, hi, hg, l0, l1, ng):
    lb = _sigmoid(l0 - l1)
    outs = []
    for j in range(len(hq)):
        state_t, o = _hg_sub(state_t, hq[j], hf[j], hi[j], lb)
        outs.append(_rms(o, ng) * _silu(hg[j]))
    return state_t, outs


def _hg_pieces(ref, c):
    n = HG_CHUNK // HG_SUB
    return [ref[pl.ds(pl.multiple_of(c * HG_CHUNK + j * HG_SUB, HG_SUB), HG_SUB), :] for j in range(n)]


def _hg_specs(s):
    col = lambda off: pl.BlockSpec((s, HEAD_DIM), lambda h: (0, off + h))
    lrow = pl.BlockSpec((1, HEAD_DIM), lambda h: (0, h))
    ng = pl.BlockSpec((1, HEAD_DIM), lambda h: (0, 0))
    return col, lrow, ng


def _hg_fwd(proj, logits, ng, comm=None):
    s = proj.shape[0]
    nch = s // HG_CHUNK
    n = HG_CHUNK // HG_SUB

    def body(hq_ref, hf_ref, hi_ref, hg_ref, l0_ref, l1_ref, ng_ref, o_ref, st_ref):
        def chunk(c, state_t):
            st_ref[c] = state_t
            state_t, outs = _hg_chunk(state_t, _hg_pieces(hq_ref, c), _hg_pieces(hf_ref, c), _hg_pieces(hi_ref, c),
                                      _hg_pieces(hg_ref, c), l0_ref[...], l1_ref[...], ng_ref[...])
            for j in range(n):
                o_ref[pl.ds(pl.multiple_of(c * HG_CHUNK + j * HG_SUB, HG_SUB), HG_SUB), :] = outs[j]
            return state_t

        lax.fori_loop(0, nch, chunk, jnp.zeros((HEAD_DIM, HEAD_DIM), f32))

    col, lrow, ngs = _hg_specs(s)
    (o, states), sent = _grid_call(
        body, "hg_fwd", HEADS,
        [jax.ShapeDtypeStruct((s, HEADS * HEAD_DIM), f32), jax.ShapeDtypeStruct((HEADS, nch, HEAD_DIM, HEAD_DIM), f32)],
        [col(3 * HEADS), col(4 * HEADS), col(5 * HEADS), col(6 * HEADS), lrow, lrow, ngs],
        [col(0), pl.BlockSpec((None, nch, HEAD_DIM, HEAD_DIM), lambda h: (h, 0, 0, 0))], [],
        (proj, proj, proj, proj, logits[0:1], logits[1:2], ng), comm)
    return o, states, sent


def _hg_bwd(proj, logits, ng, states, do, do_off, comm=None):
    s = proj.shape[0]
    nch = s // HG_CHUNK
    n = HG_CHUNK // HG_SUB

    def body(hq_ref, hf_ref, hi_ref, hg_ref, l0_ref, l1_ref, ng_ref, st_ref, do_ref,
             dq_ref, df_ref, di_ref, dg_ref, dl0_ref, dl1_ref, dng_ref):
        dl0_ref[...] = jnp.zeros_like(dl0_ref)
        dl1_ref[...] = jnp.zeros_like(dl1_ref)
        dng_ref[...] = jnp.zeros_like(dng_ref)

        def chunk(i, dstate):
            c = nch - 1 - i
            args = (st_ref[c], _hg_pieces(hq_ref, c), _hg_pieces(hf_ref, c), _hg_pieces(hi_ref, c),
                    _hg_pieces(hg_ref, c), l0_ref[...], l1_ref[...], ng_ref[...])
            _, vjp = jax.vjp(_hg_chunk, *args)
            dst, dq, df, di, dg, dl0, dl1, dng = vjp((dstate, _hg_pieces(do_ref, c)))
            for j in range(n):
                rows = pl.ds(pl.multiple_of(c * HG_CHUNK + j * HG_SUB, HG_SUB), HG_SUB)
                dq_ref[rows, :] = dq[j].astype(dq_ref.dtype)
                df_ref[rows, :] = df[j].astype(df_ref.dtype)
                di_ref[rows, :] = di[j].astype(di_ref.dtype)
                dg_ref[rows, :] = dg[j].astype(dg_ref.dtype)
            dl0_ref[...] += dl0
            dl1_ref[...] += dl1
            dng_ref[...] += dng
            return dst

        lax.fori_loop(0, nch, chunk, jnp.zeros((HEAD_DIM, HEAD_DIM), f32))

    col, lrow, ngs = _hg_specs(s)
    big = jax.ShapeDtypeStruct((s, HEADS * HEAD_DIM), bf16)
    vec = jax.ShapeDtypeStruct((1, HEADS * HEAD_DIM), f32)
    vspec = pl.BlockSpec((1, HEAD_DIM), lambda h: (0, h))
    (dq, df, di, dg, dl0, dl1, dng), sent = _grid_call(
        body, "hg_bwd", HEADS, [big, big, big, big, vec, vec, vec],
        [col(3 * HEADS), col(4 * HEADS), col(5 * HEADS), col(6 * HEADS), lrow, lrow, ngs,
         pl.BlockSpec((None, nch, HEAD_DIM, HEAD_DIM), lambda h: (h, 0, 0, 0)), col(do_off)],
        [col(0), col(0), col(0), col(0), vspec, vspec, vspec], [],
        (proj, proj, proj, proj, logits[0:1], logits[1:2], ng, states, do), comm)
    return (jnp.concatenate([dq, df, di, dg], axis=1), jnp.concatenate([dl0, dl1], axis=0),
            dng.reshape(HEADS, HEAD_DIM), sent)


def _shift_rows(x, k, down):
    if k == 0:
        return x
    s = x.shape[0]
    row = lax.broadcasted_iota(jnp.int32, x.shape, 0)
    if down:
        return jnp.where(row < k, 0.0, pltpu.roll(x, k, 0))
    return jnp.where(row >= s - k, 0.0, pltpu.roll(x, s - k, 0))


def _conv_fwd(proj, w, b):
    s, w2 = proj.shape
    width = w2 // 2
    nb = width // RG_BLOCK

    def body(x_ref, w_ref, b_ref, y_ref, yb_ref):
        x = x_ref[...]
        y = jnp.broadcast_to(b_ref[...], x.shape)
        for k in range(CONV_WIDTH):
            y = y + _shift_rows(x, k, True) * w_ref[pl.ds(k, 1), :]
        y_ref[...] = y
        yb_ref[...] = y.astype(bf16)

    blk = pl.BlockSpec((s, RG_BLOCK), lambda j: (0, j))
    return pl.pallas_call(
        body, name="conv_fwd", grid=(nb,),
        out_shape=[jax.ShapeDtypeStruct((s, width), f32), jax.ShapeDtypeStruct((s, width), bf16)],
        in_specs=[pl.BlockSpec((s, RG_BLOCK), lambda j: (0, nb + j)),
                  pl.BlockSpec((CONV_WIDTH, RG_BLOCK), lambda j: (0, j)), pl.BlockSpec((1, RG_BLOCK), lambda j: (0, j))],
        out_specs=[blk, blk],
        compiler_params=_params("parallel"),
    )(proj, w, b)


def _conv_bwd(proj, w, dy_a, dy_b):
    s, w2 = proj.shape
    width = w2 // 2
    nb = width // RG_BLOCK

    def body(x_ref, w_ref, dya_ref, dyb_ref, dx_ref, dw_ref, db_ref):
        x = x_ref[...]
        dy_v = dya_ref[...] + dyb_ref[...]
        dx = jnp.zeros_like(x)
        for k in range(CONV_WIDTH):
            dx = dx + _shift_rows(dy_v, k, False) * w_ref[pl.ds(k, 1), :]
            dw_ref[pl.ds(k, 1), :] = jnp.sum(dy_v * _shift_rows(x, k, True), axis=0, keepdims=True)
        dx_ref[...] = dx.astype(dx_ref.dtype)
        db_ref[...] = jnp.sum(dy_v, axis=0, keepdims=True)

    blk = pl.BlockSpec((s, RG_BLOCK), lambda j: (0, j))
    wblk = pl.BlockSpec((CONV_WIDTH, RG_BLOCK), lambda j: (0, j))
    bblk = pl.BlockSpec((1, RG_BLOCK), lambda j: (0, j))
    return pl.pallas_call(
        body, name="conv_bwd", grid=(nb,),
        out_shape=[jax.ShapeDtypeStruct((s, width), bf16), jax.ShapeDtypeStruct((CONV_WIDTH, width), f32),
                   jax.ShapeDtypeStruct((1, width), f32)],
        in_specs=[pl.BlockSpec((s, RG_BLOCK), lambda j: (0, nb + j)), wblk, blk, blk],
        out_specs=[blk, wblk, bblk],
        compiler_params=_params("parallel"),
    )(proj, w, dy_a, dy_b)


def _rg_rows(w_ref):
    return jnp.concatenate([w_ref[q] for q in range(N_CHIPS)], axis=0)


def _bd_fwd(xb, wa, wx):
    s, width = xb.shape
    nb = width // RG_BLOCK

    def body(x_ref, wa_ref, wx_ref, ra_ref, ix_ref):
        x = x_ref[...]
        ra_ref[...] = lax.dot_general(x, _rg_rows(wa_ref), NN, preferred_element_type=f32)
        ix_ref[...] = lax.dot_general(x, _rg_rows(wx_ref), NN, preferred_element_type=f32)

    blk = pl.BlockSpec((s, RG_BLOCK), lambda j: (0, j))
    wblk = pl.BlockSpec((N_CHIPS, None, RG_BLOCK // N_CHIPS, RG_BLOCK), lambda j: (0, j, 0, 0))
    out = jax.ShapeDtypeStruct((s, width), f32)
    return pl.pallas_call(
        body, name="bd_fwd", grid=(nb,), out_shape=[out, out],
        in_specs=[blk, wblk, wblk], out_specs=[blk, blk],
        compiler_params=_params("parallel"),
    )(xb, wa, wx)


def _bd_bwd(xb, wa, wx, dra, dix):
    s, width = xb.shape
    nb = width // RG_BLOCK
    rq = RG_BLOCK // N_CHIPS

    def body(x_ref, wa_ref, wx_ref, dra_ref, dix_ref, dx_ref, dwa_ref, dwx_ref):
        x = x_ref[...]
        dra_v = dra_ref[...]
        dix_v = dix_ref[...]
        dx_ref[...] = (lax.dot_general(dra_v, _rg_rows(wa_ref), NT, preferred_element_type=f32)
                       + lax.dot_general(dix_v, _rg_rows(wx_ref), NT, preferred_element_type=f32))
        dwa = lax.dot_general(x, dra_v, TN, preferred_element_type=f32)
        dwx = lax.dot_general(x, dix_v, TN, preferred_element_type=f32)
        for q in range(N_CHIPS):
            dwa_ref[q] = dwa[q * rq:(q + 1) * rq, :].astype(dwa_ref.dtype)
            dwx_ref[q] = dwx[q * rq:(q + 1) * rq, :].astype(dwx_ref.dtype)

    blk = pl.BlockSpec((s, RG_BLOCK), lambda j: (0, j))
    wblk = pl.BlockSpec((N_CHIPS, None, rq, RG_BLOCK), lambda j: (0, j, 0, 0))
    wout = jax.ShapeDtypeStruct((N_CHIPS, nb, rq, RG_BLOCK), bf16)
    return pl.pallas_call(
        body, name="bd_bwd", grid=(nb,),
        out_shape=[jax.ShapeDtypeStruct((s, width), f32), wout, wout],
        in_specs=[blk, wblk, wblk, blk, blk], out_specs=[blk, wblk, wblk],
        compiler_params=_params("parallel"),
    )(xb, wa, wx, dra, dix)


def _scan_tiles(a_ref, u_ref, out_ref, n_tiles, reverse):
    cb = a_ref.shape[1]
    row = lax.broadcasted_iota(jnp.int32, (SUBLANES, cb), 0)

    def tile(i, carry):
        t = (n_tiles - 1 - i) if reverse else i
        rows = pl.ds(pl.multiple_of(t * SUBLANES, SUBLANES), SUBLANES)
        a = a_ref[rows, :]
        b = u_ref[rows, :]
        for d in (1, 2, 4):
            if reverse:
                valid = row < SUBLANES - d
                shift = SUBLANES - d
            else:
                valid = row >= d
                shift = d
            a_n = jnp.where(valid, pltpu.roll(a, shift, 0), 1.0)
            b_n = jnp.where(valid, pltpu.roll(b, shift, 0), 0.0)
            b = a * b_n + b
            a = a * a_n
        out = a * carry + b
        out_ref[rows, :] = out
        edge = t * SUBLANES + (0 if reverse else SUBLANES - 1)
        return jnp.broadcast_to(out_ref[pl.ds(edge, 1), :], (SUBLANES, cb))

    lax.fori_loop(0, n_tiles, tile, jnp.zeros((SUBLANES, cb), f32))


def _scan_fwd(a, u):
    s, width = a.shape

    def body(a_ref, u_ref, h_ref):
        _scan_tiles(a_ref, u_ref, h_ref, s // SUBLANES, False)

    blk = pl.BlockSpec((s, RG_BLOCK), lambda j: (0, j))
    return pl.pallas_call(
        body, name="scan_fwd", grid=(width // RG_BLOCK,), out_shape=jax.ShapeDtypeStruct((s, width), f32),
        in_specs=[blk, blk], out_specs=blk, compiler_params=_params("parallel"),
    )(a, u)


def _scan_bwd(a, h, dh):
    s, width = a.shape

    def body(a_ref, h_ref, dh_ref, du_ref, da_ref, a_next):
        a_next[...] = _shift_rows(a_ref[...], 1, False)
        _scan_tiles(a_next, dh_ref, du_ref, s // SUBLANES, True)
        da_ref[...] = du_ref[...] * _shift_rows(h_ref[...], 1, True)

    blk = pl.BlockSpec((s, RG_BLOCK), lambda j: (0, j))
    out = jax.ShapeDtypeStruct((s, width), f32)
    return pl.pallas_call(
        body, name="scan_bwd", grid=(width // RG_BLOCK,), out_shape=[out, out],
        in_specs=[blk, blk, blk], out_specs=[blk, blk],
        scratch_shapes=[pltpu.VMEM((s, RG_BLOCK), f32)],
        compiler_params=_params("parallel"),
    )(a, h, dh)


EW_TILE_BYTES = 1 << 20


def _ew_rows(rows, width):
    limit = max(16, EW_TILE_BYTES // (4 * width))
    t = (min(limit, rows) // 16) * 16
    while t >= 16:
        if rows % t == 0:
            return t
        t -= 16
    return rows


def _as2d(a):
    return a.reshape(-1, a.shape[-1])


def _ew(fn, ins, out_dtypes, name):
    shape = ins[0].shape
    ins2 = [_as2d(a) for a in ins]
    rows, width = ins2[0].shape
    res = _rowwise(fn, ins2, [], [(width, dt) for dt in out_dtypes], name, _ew_rows(rows, width))
    return [r.reshape(shape) for r in res]


def _f_adamw(w, g, m, v):
    m = ADAM_B1 * m + (1.0 - ADAM_B1) * g
    v = ADAM_B2 * v + (1.0 - ADAM_B2) * (g * g)
    m_hat = m / (1.0 - ADAM_B1 ** ADAM_STEP)
    v_hat = v / (1.0 - ADAM_B2 ** ADAM_STEP)
    delta = -ADAM_LR * (m_hat / (jnp.sqrt(v_hat) + ADAM_EPS) + ADAM_WD * w)
    return delta, m, v


ANY = pl.BlockSpec(memory_space=pl.ANY)


def _place():
    x, y, c = lax.axis_index("x"), lax.axis_index("y"), lax.axis_index("c")
    return x, y, c, [(1 - x, y), (x, 1 - y), (1 - x, 1 - y)]


def _rcopy(src, dst, send_sem, recv_sem, dev):
    return pltpu.make_async_remote_copy(src_ref=src, dst_ref=dst, send_sem=send_sem, recv_sem=recv_sem,
                                        device_id=dev, device_id_type=MESH)


def _half(ref, h, kind, lead=0):
    ax = lead + (1 if kind == "rg" else 0)
    n = ref.shape[ax] // 2
    idx = [slice(None)] * ref.ndim
    idx[ax] = pl.ds(h * n, n)
    return ref.at[tuple(idx)]


def _gather_weights(slots, kinds):
    n = len(slots)

    def first(out, send_sems, recv_sems):
        x, y, c, chips = _place()
        return [_rcopy(mine, mine, send_sems.at[6 * e + j], recv_sems.at[6 * e + j], (*chip, c))
                for e in range(n) for j, chip in enumerate(chips)
                for mine in [_half(out[e].at[2 * x + y], c, kinds[e])]]

    def start(ins, out, send_sems, recv_sems):
        for cp in first(out, send_sems, recv_sems):
            cp.start()

    def finish(ins, out, send_sems, recv_sems):
        x, y, c, chips = _place()
        sibling = (x, y, 1 - c)
        passed_on = []
        for e in range(n):
            for j, chip in enumerate(chips):
                landed = _half(out[e].at[2 * chip[0] + chip[1]], c, kinds[e])
                _rcopy(landed, landed, send_sems.at[6 * e + j], recv_sems.at[6 * e + j], (*chip, c)).wait_recv()
                cp = _rcopy(landed, landed, send_sems.at[6 * e + 3 + j], recv_sems.at[6 * e + 3 + j], sibling)
                cp.start()
                passed_on.append(cp)
        for e in range(n):
            for j, chip in enumerate(chips):
                passed = _half(out[e].at[2 * chip[0] + chip[1]], 1 - c, kinds[e])
                _rcopy(passed, passed, send_sems.at[6 * e + 3 + j], recv_sems.at[6 * e + 3 + j], sibling).wait_recv()
        for cp in first(out, send_sems, recv_sems) + passed_on:
            cp.wait_send()

    return _Comm(slots, [jax.ShapeDtypeStruct(s.shape, s.dtype) for s in slots], {e: e for e in range(n)}, 6 * n,
                 start, finish)


def _exchange_halves(grads, kinds):
    n = len(grads)

    def copies(g, theirs, send_sems, recv_sems):
        x, y, c, _ = _place()
        return [_rcopy(_half(g[e], 1 - c, kinds[e], 1), theirs[e], send_sems.at[e], recv_sems.at[e], (x, y, 1 - c))
                for e in range(n)]

    def start(*refs):
        for cp in copies(*refs):
            cp.start()

    def finish(*refs):
        for cp in copies(*refs):
            cp.wait()

    def half_shape(s, kind):
        shp = list(s.shape)
        shp[2 if kind == "rg" else 1] //= 2
        return jax.ShapeDtypeStruct(tuple(shp), s.dtype)

    return _Comm(grads, [half_shape(s, k) for s, k in zip(grads, kinds)], {}, n, start, finish)


def _exchange_chips(parts):
    n = len(parts)

    def copies(p, got, send_sems, recv_sems):
        x, y, c, chips = _place()
        return [_rcopy(p[e].at[2 * chip[0] + chip[1]], got[e].at[j], send_sems.at[3 * e + j], recv_sems.at[3 * e + j],
                       (*chip, c))
                for e in range(n) for j, chip in enumerate(chips)]

    def start(*refs):
        for cp in copies(*refs):
            cp.start()

    def finish(*refs):
        for cp in copies(*refs):
            cp.wait()

    return _Comm(parts, [jax.ShapeDtypeStruct((N_CHIPS - 1,) + s.shape[1:], s.dtype) for s in parts], {}, 3 * n,
                 start, finish)


def _share_halves(shards, kinds):
    n = len(shards)

    def copies(out, send_sems, recv_sems, h):
        x, y, c, _ = _place()
        return [_rcopy(part, part, send_sems.at[e], recv_sems.at[e], (x, y, 1 - c))
                for e in range(n) for part in [_half(out[e], c if h == "mine" else 1 - c, kinds[e])]]

    def start(ins, out, send_sems, recv_sems):
        for cp in copies(out, send_sems, recv_sems, "mine"):
            cp.start()

    def finish(ins, out, send_sems, recv_sems):
        for cp in copies(out, send_sems, recv_sems, "other"):
            cp.wait_recv()
        for cp in copies(out, send_sems, recv_sems, "mine"):
            cp.wait_send()

    return _Comm(shards, [jax.ShapeDtypeStruct(s.shape, s.dtype) for s in shards], {e: e for e in range(n)}, n,
                 start, finish)


def _tiles_call(fn, scalars, ins, outs, steps, name):
    n_in = len(ins)

    def body(s_ref, *refs):
        res = fn(*[r[...] for r in refs[:n_in]])
        for r, o in zip(res, refs[n_in:]):
            o[...] = r.astype(o.dtype)

    spec = lambda blk, idx: pl.BlockSpec(blk, lambda i, s, idx=idx: (idx(i, s), 0))
    return pl.pallas_call(
        body, name=name, out_shape=[o[0] for o in outs],
        grid_spec=pltpu.PrefetchScalarGridSpec(
            num_scalar_prefetch=1, grid=(steps,),
            in_specs=[spec(blk, idx) for _, blk, idx in ins], out_specs=[spec(blk, idx) for _, blk, idx in outs]),
        compiler_params=_params("arbitrary"),
    )(scalars, *[a for a, _, _ in ins])


def _groups(kind, shard_shape):
    if kind == "rg":
        return shard_shape[0], shard_shape[1], shard_shape[2]
    return 1, shard_shape[0], shard_shape[1]


def _half_tiles(kind, shard_shape):
    g, rg, width = _groups(kind, shard_shape)
    tr = _ew_rows(rg // 2, width)
    nt = (rg // 2) // tr
    in_whole = lambda i, c: (i // nt) * (rg // tr) + c * nt + i % nt
    return g, rg, width, tr, nt, in_whole


def _cast_into_slot(w, layer, kind, scalars):
    shard_shape = w.shape[1:]
    g, rg, width, tr, nt, _ = _half_tiles(kind, shard_shape)
    per_shard = g * rg // tr
    (buf,) = _tiles_call(
        lambda v: (v,), scalars, [(_as2d(w), (tr, width), lambda i, s: layer * per_shard + i)],
        [(jax.ShapeDtypeStruct((N_CHIPS * g * rg, width), bf16), (tr, width), lambda i, s: s[0] * per_shard + i)],
        per_shard, "cast_" + kind)
    return buf.reshape((N_CHIPS,) + shard_shape)


def _add_cores(grad, theirs, kind, scalars):
    g, rg, width, tr, nt, in_whole = _half_tiles(kind, grad.shape[1:])
    steps = N_CHIPS * g * nt
    (out,) = _tiles_call(
        lambda u, v: (u.astype(f32) + v.astype(f32),), scalars,
        [(_as2d(grad), (tr, width), lambda i, s: in_whole(i, s[1])), (_as2d(theirs), (tr, width), lambda i, s: i)],
        [(jax.ShapeDtypeStruct((steps * tr, width), bf16), (tr, width), lambda i, s: i)], steps, "rs_add_cores")
    return out.reshape(theirs.shape)


def _sum_chips(part, got, kind, scalars):
    shard_shape = list(part.shape[1:])
    shard_shape[1 if kind == "rg" else 0] *= 2
    g, rg, width, tr, nt, in_whole = _half_tiles(kind, shard_shape)
    steps = g * nt
    up = lambda t: t.astype(f32)
    g2 = _as2d(got)
    ins = [(_as2d(part), (tr, width), lambda i, s: s[0] * steps + i)]
    ins += [(g2, (tr, width), lambda i, s, j=j: j * steps + i) for j in range(N_CHIPS - 1)]
    (out,) = _tiles_call(
        lambda a, b, c, e: (((up(a) + up(b)) + up(c)) + up(e),), scalars, ins,
        [(jax.ShapeDtypeStruct((g * rg, width), f32), (tr, width), lambda i, s: in_whole(i, s[1]))], steps, "rs_add_chips")
    return out.reshape(shard_shape)


def _all_gather_rows(v, reduce):
    rows = v.shape[0]

    def body(v_ref, out_ref, *rest):
        if reduce:
            sum_ref, send_sems, recv_sems, local_sem = rest
        else:
            send_sems, recv_sems, local_sem = rest
        x, y, c, chips = _place()
        me, sibling = (x, y, c), (x, y, 1 - c)

        def blk(px, py, pc):
            return out_ref.at[pl.ds((4 * px + 2 * py + pc) * rows, rows), :]

        def copy(k, block, to, src=None):
            return _rcopy(blk(*block) if src is None else src, blk(*block), send_sems.at[k], recv_sems.at[k], to)

        mine = pltpu.make_async_copy(v_ref, blk(*me), local_sem)
        mine.start()
        first = [copy(0, me, sibling, src=v_ref)] + [copy(1 + j, me, (*chip, c), src=v_ref) for j, chip in enumerate(chips)]
        for cp in first:
            cp.start()
        passed = [copy(4 + j, (*chip, c), sibling) for j, chip in enumerate(chips)]
        for j, chip in enumerate(chips):
            copy(1 + j, (*chip, c), me).wait_recv()
            passed[j].start()
        copy(0, sibling, me).wait_recv()
        for j, chip in enumerate(chips):
            copy(4 + j, (*chip, 1 - c), me).wait_recv()
        for cp in first + passed:
            cp.wait_send()
        mine.wait()
        if reduce:
            acc = out_ref[pl.ds(0, rows), :]
            for d in range(1, 8):
                acc = acc + out_ref[pl.ds(d * rows, rows), :]
            sum_ref[...] = acc

    vm = pl.BlockSpec(memory_space=pltpu.VMEM)
    gathered = jax.ShapeDtypeStruct((8 * rows, LANES), f32)
    res = pl.pallas_call(
        body, name="all_reduce_small" if reduce else "all_gather_small",
        out_shape=[gathered, jax.ShapeDtypeStruct((rows, LANES), f32)] if reduce else [gathered],
        in_specs=[vm], out_specs=[vm, vm] if reduce else [vm],
        scratch_shapes=[pltpu.SemaphoreType.DMA((7,)), pltpu.SemaphoreType.DMA((7,)), pltpu.SemaphoreType.DMA],
    )(v)
    return res[1] if reduce else res[0]


def _pack(parts):
    flat = jnp.concatenate([p.reshape(-1).astype(f32) for p in parts])
    rows = -(-flat.shape[0] // (SUBLANES * LANES)) * SUBLANES
    return jnp.pad(flat, (0, rows * LANES - flat.shape[0])).reshape(rows, LANES)


def _unpack(packed, shapes):
    flat = packed.reshape(-1)
    out, off = [], 0
    for shp in shapes:
        size = 1
        for d in shp:
            size *= d
        out.append(flat[off:off + size].reshape(shp))
        off += size
    return out


WEIGHTS = ["mix_pre_g", "mix_post_g", "ffn_pre_g", "ffn_post_g", "ple_norm_g", "w_in_even", "w_out_even",
           "hg_lb_logits", "hg_norm_g", "w_in_odd", "conv_w", "conv_b", "rg_wa", "rg_ba", "rg_wx", "rg_bx",
           "rg_lambda", "w_out_odd", "w_gate_up", "w_down", "w_ple_up", "w_ple_gate"]
BIG = [("w_in_even", 0, "col"), ("w_out_even", 0, "row"), ("w_gate_up", 0, "col"), ("w_down", 0, "row"),
       ("w_ple_up", 0, "col"), ("w_ple_gate", 0, "row"), ("w_in_odd", 0, "col"), ("rg_wa", 0, "rg"),
       ("rg_wx", 0, "rg"), ("w_out_odd", 0, "row"), ("w_gate_up", 1, "col"), ("w_down", 1, "row"),
       ("w_ple_up", 1, "col"), ("w_ple_gate", 1, "row")]
GATHER_FIRST, GATHER_BEHIND_SB, GATHER_BEHIND_HG = [0, 1], [2, 3, 4, 5], [6, 7, 8, 9, 10, 11, 12, 13]
REDUCE_BEHIND_SB, REDUCE_BEHIND_HG, REDUCE_LAST = [6, 7, 8, 9, 10, 11, 12, 13], [2, 3, 4, 5], [0, 1]
BIG_NAMES = ["w_in_even", "w_out_even", "w_gate_up", "w_down", "w_ple_up", "w_ple_gate", "w_in_odd", "rg_wa", "rg_wx",
             "w_out_odd"]
SMALL_SHARDED = ["conv_w", "conv_b", "rg_ba", "rg_bx", "rg_lambda"]
REPLICATED = ["mix_pre_g", "mix_post_g", "ffn_pre_g", "ffn_post_g", "ple_norm_g", "hg_lb_logits", "hg_norm_g"]
TR = 256
TR_WIDE = 128


def _ffn_ple_fwd(h, m, l, gains, w_gu, w_d, w_pu, w_pg, pb, next_pre):
    d = h.shape[1]
    ff = w_d.shape[0]
    hb, nf = _rowwise(_f_post_pre, [h, m], [gains["mix_post_g"][l], gains["ffn_pre_g"][l]], [(d, f32), (d, bf16)],
                      f"post_mix{l}", TR)
    gu = _mm(nf, w_gu, "nn", f32, f"ffn_up{l}", b_shards=True)
    (act,) = _rowwise(_f_swiglu, [(gu, ff, 0), (gu, ff, 1)], [], [(ff, bf16)], f"swiglu{l}", TR_WIDE)
    f = _mm(act, w_d, "nn", f32, f"ffn_down{l}")
    hc, hcb = _rowwise(_f_post, [hb, f], [gains["ffn_post_g"][l]], [(d, f32), (d, bf16)], f"post_ffn{l}", TR)
    t = _mm(hcb, w_pg, "nn", f32, f"ple_gate{l}")
    e = _mm(pb, w_pu, "nn", f32, f"ple_up{l}", b_shards=True)
    if next_pre is None:
        (hd,) = _rowwise(_f_ple, [hc, t, e], [gains["ple_norm_g"][l]], [(d, f32)], f"ple{l}", TR)
        n_next = None
    else:
        hd, n_next = _rowwise(_f_ple_pre, [hc, t, e], [gains["ple_norm_g"][l], next_pre], [(d, f32), (d, bf16)],
                              f"ple{l}", TR)
    saved = dict(h=h, m=m, hb=hb, nf=nf, gu=gu, act=act, f=f, hc=hc, hcb=hcb, t=t, e=e)
    return hd, n_next, saved


def _ffn_ple_bwd(sv, l, gains, w_gu, w_d, w_pg, pb, next_pre, dhd, dn_next):
    ff = w_d.shape[0]
    if next_pre is None:
        (dhc, dt, de), (dg_ple,) = _rowwise_vjp(_f_ple, [sv["hc"], sv["t"], sv["e"]], [gains["ple_norm_g"][l]], [dhd],
                                                [(0, f32), (1, bf16), (2, bf16)], f"ple_bwd{l}", TR)
        dg_next = None
    else:
        (dhc, dt, de), (dg_ple, dg_next) = _rowwise_vjp(
            _f_ple_pre, [sv["hc"], sv["t"], sv["e"]], [gains["ple_norm_g"][l], next_pre], [dhd, dn_next],
            [(0, f32), (1, bf16), (2, bf16)], f"ple_bwd{l}", TR)
    gw = {}
    gw["w_ple_gate"] = _mm(sv["hcb"], dt, "tn", bf16, f"ple_gate_wgrad{l}")
    gw["w_ple_up"] = _mm(pb, de, "tn", bf16, f"ple_up_wgrad{l}", out_shards=True)
    dhcb = _mm(dt, w_pg, "nt", f32, f"ple_gate_dgrad{l}")
    (dhb, df), (dg_fpost,) = _rowwise_vjp(_f_post, [sv["hb"], sv["f"]], [gains["ffn_post_g"][l]], [dhc, dhcb],
                                          [(0, f32), (1, bf16)], f"post_ffn_bwd{l}", TR)
    gw["w_down"] = _mm(sv["act"], df, "tn", bf16, f"ffn_down_wgrad{l}")
    dact = _mm(df, w_d, "nt", f32, f"ffn_down_dgrad{l}")
    (dgu,), _ = _rowwise_vjp(_f_swiglu, [(sv["gu"], ff, 0), (sv["gu"], ff, 1)], [], [dact], [(0, bf16), (1, bf16)],
                             f"swiglu_bwd{l}", TR_WIDE, concat=True)
    gw["w_gate_up"] = _mm(sv["nf"], dgu, "tn", bf16, f"ffn_up_wgrad{l}", out_shards=True)
    dnf = _mm(dgu, w_gu, "nt", f32, f"ffn_up_dgrad{l}", b_shards=True)
    (dh, dm), (dg_post, dg_fpre) = _rowwise_vjp(
        _f_post_pre, [sv["h"], sv["m"]], [gains["mix_post_g"][l], gains["ffn_pre_g"][l]], [dhb, dnf],
        [(0, f32), (1, bf16)], f"post_mix_bwd{l}", TR)
    gg = dict(ple_norm_g=dg_ple, ffn_post_g=dg_fpost, mix_post_g=dg_post, ffn_pre_g=dg_fpre)
    return dh, dm, gw, gg, dg_next


def kernel(x, p, mix_pre_g, mix_post_g, ffn_pre_g, ffn_post_g, ple_norm_g, w_in_even, w_out_even, hg_lb_logits, hg_norm_g, w_in_odd, conv_w, conv_b, rg_wa, rg_ba, rg_wx, rg_bx, rg_lambda, w_out_odd, w_gate_up, w_down, w_ple_up, w_ple_gate, loss_target, m_mix_pre_g, m_mix_post_g, m_ffn_pre_g, m_ffn_post_g, m_ple_norm_g, m_w_in_even, m_w_out_even, m_hg_lb_logits, m_hg_norm_g, m_w_in_odd, m_conv_w, m_conv_b, m_rg_wa, m_rg_ba, m_rg_wx, m_rg_bx, m_rg_lambda, m_w_out_odd, m_w_gate_up, m_w_down, m_w_ple_up, m_w_ple_gate, v_mix_pre_g, v_mix_post_g, v_ffn_pre_g, v_ffn_post_g, v_ple_norm_g, v_w_in_even, v_w_out_even, v_hg_lb_logits, v_hg_norm_g, v_w_in_odd, v_conv_w, v_conv_b, v_rg_wa, v_rg_ba, v_rg_wx, v_rg_bx, v_rg_lambda, v_w_out_odd, v_w_gate_up, v_w_down, v_w_ple_up, v_w_ple_gate):
    given = dict(locals())
    w = {n: given[n] for n in WEIGHTS}
    mom = {n: given["m_" + n] for n in WEIGHTS}
    var = {n: given["v_" + n] for n in WEIGHTS}
    xi, yi = lax.axis_index("x"), lax.axis_index("y")
    my_q = 2 * xi + yi
    h0 = x[0]
    s, d = h0.shape
    n_layers = p.shape[0]
    gains = {n: [w[n][l:l + 1] for l in range(n_layers)]
             for n in ("mix_pre_g", "mix_post_g", "ffn_pre_g", "ffn_post_g", "ple_norm_g")}

    kinds = [k for _, _, k in BIG]
    place = jnp.stack([my_q, lax.axis_index("c")]).astype(jnp.int32)
    slots = [_cast_into_slot(w[n], l, k, place) for n, l, k in BIG]
    full = {}

    def gather(group):
        return _gather_weights([slots[i] for i in group], [kinds[i] for i in group])

    def gathered(group, bufs):
        full.update({BIG[i][:2]: b for i, b in zip(group, bufs)})

    gathered(GATHER_FIRST, _comm_call(gather(GATHER_FIRST), "gather_first"))

    def rows_full(name, l):
        g = full[(name, l)]
        return g.reshape(g.shape[0] * g.shape[1], g.shape[2])

    small_shapes = [w[n].shape[1:] for n in SMALL_SHARDED]
    small_rows = _all_gather_rows(_pack([w[n][0] for n in SMALL_SHARDED]), False)
    per_chip = [_unpack(small_rows.reshape(8, -1)[2 * q], small_shapes) for q in range(N_CHIPS)]
    conv_w_f, conv_b_f, rg_ba_f, rg_bx_f, rg_lam_f = [
        jnp.concatenate([per_chip[q][i] for q in range(N_CHIPS)], axis=-1) for i in range(len(SMALL_SHARDED))]
    lru = conv_b_f.shape[0]
    conv_b_f, rg_ba_f, rg_bx_f, rg_lam_f = [v.reshape(1, lru) for v in (conv_b_f, rg_ba_f, rg_bx_f, rg_lam_f)]
    pb = [_ew(lambda v: (v,), [p[l, 0]], [bf16], f"cast_p{l}")[0] for l in range(n_layers)]

    (n0,) = _rowwise(_f_pre, [h0], [gains["mix_pre_g"][0]], [(d, bf16)], "pre_mix0", TR)
    proj0 = _mm(n0, full[("w_in_even", 0)], "nn", f32, "in_even", b_shards=True)
    a_out, bufs = _sb_fwd(proj0, gather(GATHER_BEHIND_SB))
    gathered(GATHER_BEHIND_SB, bufs)
    b_out, hg_states, bufs = _hg_fwd(proj0, w["hg_lb_logits"], w["hg_norm_g"], gather(GATHER_BEHIND_HG))
    gathered(GATHER_BEHIND_HG, bufs)
    mix0 = jnp.concatenate([a_out, b_out], axis=1).astype(bf16)
    m0 = _mm(mix0, rows_full("w_out_even", 0), "nn", f32, "out_even")
    h1, n1, sv0 = _ffn_ple_fwd(h0, m0, 0, gains, full[("w_gate_up", 0)], rows_full("w_down", 0), full[("w_ple_up", 0)],
                               rows_full("w_ple_gate", 0), pb[0], gains["mix_pre_g"][1])

    proj1 = _mm(n1, full[("w_in_odd", 0)], "nn", f32, "in_odd", b_shards=True)
    xc, xcb = _conv_fwd(proj1, conv_w_f, conv_b_f)
    ra, ix = _bd_fwd(xcb, full[("rg_wa", 0)], full[("rg_wx", 0)])
    gate_in = [ra, ix, xc], [rg_ba_f, rg_bx_f, rg_lam_f]
    a_rec, u_rec = _rowwise(_f_rg_gate, *gate_in, [(lru, f32), (lru, f32)], "rg_gate", TR_WIDE, row_ids=True)
    hs = _scan_fwd(a_rec, u_rec)
    out_in = [(proj1, lru, 0), hs]
    (yb,) = _rowwise(_f_rg_out, out_in, [], [(lru, bf16)], "rg_out", TR_WIDE)
    m1 = _mm(yb, rows_full("w_out_odd", 0), "nn", f32, "out_odd")
    h2, _, sv1 = _ffn_ple_fwd(h1, m1, 1, gains, full[("w_gate_up", 1)], rows_full("w_down", 1), full[("w_ple_up", 1)],
                              rows_full("w_ple_gate", 1), pb[1], None)
    dy, loss_local = _loss_head(h2, loss_target[0])
    loss = lax.psum(loss_local, ("x", "y", "c"))

    gbig, chip_part, got = {}, {}, {}

    def core_sums(group, tag):
        gl = []
        for i in group:
            n_, l_, k_ = BIG[i]
            g_ = gbig[(n_, l_)]
            gl.append(g_.reshape(N_CHIPS, g_.shape[0] // N_CHIPS, g_.shape[1]) if k_ == "row" else g_)
        theirs = _comm_call(_exchange_halves(gl, [kinds[i] for i in group]), "rs_core_exchange_" + tag)
        chip_part.update({i: _add_cores(g_, t_, kinds[i], place) for i, g_, t_ in zip(group, gl, theirs)})
        return _exchange_chips([chip_part[i] for i in group])

    dh1, dm1, gw, gg1, _ = _ffn_ple_bwd(sv1, 1, gains, full[("w_gate_up", 1)], rows_full("w_down", 1),
                                        rows_full("w_ple_gate", 1), pb[1], None, dy, None)
    gbig.update({(n, 1): g for n, g in gw.items()})
    gbig[("w_out_odd", 0)] = _mm(yb, dm1, "tn", bf16, "out_odd_wgrad")
    dyo = _mm(dm1, rows_full("w_out_odd", 0), "nt", f32, "out_odd_dgrad")
    (dgate, dhs), _ = _rowwise_vjp(_f_rg_out, out_in, [], [dyo], [(0, bf16), (1, f32)], "rg_out_bwd", TR_WIDE)
    du, da = _scan_bwd(a_rec, hs, dhs)
    (dra, dix, dxc_direct), (g_ba, g_bx, g_lam) = _rowwise_vjp(
        _f_rg_gate, *gate_in, [da, du], [(0, bf16), (1, bf16), (2, f32)], "rg_gate_bwd", TR_WIDE, row_ids=True)
    dxc_gates, gbig[("rg_wa", 0)], gbig[("rg_wx", 0)] = _bd_bwd(xcb, full[("rg_wa", 0)], full[("rg_wx", 0)], dra, dix)
    dxb, g_conv_w, g_conv_b = _conv_bwd(proj1, conv_w_f, dxc_direct, dxc_gates)
    dproj1 = jnp.concatenate([dgate, dxb], axis=1)
    gbig[("w_in_odd", 0)] = _mm(n1, dproj1, "tn", bf16, "in_odd_wgrad", out_shards=True)
    dn1 = _mm(dproj1, full[("w_in_odd", 0)], "nt", f32, "in_odd_dgrad", b_shards=True)
    chips_layer1 = core_sums(REDUCE_BEHIND_SB, "layer1")

    dh0, dm0, gw, gg0, g_pre1 = _ffn_ple_bwd(sv0, 0, gains, full[("w_gate_up", 0)], rows_full("w_down", 0),
                                             rows_full("w_ple_gate", 0), pb[0], gains["mix_pre_g"][1], dh1, dn1)
    gbig.update({(n, 0): g for n, g in gw.items()})
    chips_ffn0 = core_sums(REDUCE_BEHIND_HG, "ffn0")
    gbig[("w_out_even", 0)] = _mm(mix0, dm0, "tn", bf16, "out_even_wgrad")
    dmix = _mm(dm0, rows_full("w_out_even", 0), "nt", f32, "out_even_dgrad")
    (dq, dk, dv), bufs = _sb_bwd(proj0, a_out, dmix, chips_layer1)
    got.update(zip(REDUCE_BEHIND_SB, bufs))
    dhg, g_logits, g_ng, bufs = _hg_bwd(proj0, w["hg_lb_logits"], w["hg_norm_g"], hg_states, dmix, HEADS, chips_ffn0)
    got.update(zip(REDUCE_BEHIND_HG, bufs))
    dproj0 = jnp.concatenate([dq, dk, dv, dhg], axis=1)
    gbig[("w_in_even", 0)] = _mm(n0, dproj0, "tn", bf16, "in_even_wgrad", out_shards=True)
    dn0 = _mm(dproj0, full[("w_in_even", 0)], "nt", f32, "in_even_dgrad", b_shards=True)
    (grad_x,), (g_pre0,) = _rowwise_vjp(_f_pre_id, [h0], [gains["mix_pre_g"][0]], [dh0, dn0], [(0, f32)], "pre_mix0_bwd", TR)
    got.update(zip(REDUCE_LAST, _comm_call(core_sums(REDUCE_LAST, "mix0"), "rs_chip_exchange_mix0")))
    reduced = _comm_call(_share_halves([_sum_chips(chip_part[i], got[i], kinds[i], place) for i in range(len(BIG))],
                                       kinds), "rs_share_halves")
    by_layer = {(n, l): r for (n, l, _), r in zip(BIG, reduced)}
    grad_big = {n: jnp.stack([by_layer[(n, l)] for l in range(w[n].shape[0])]) for n in BIG_NAMES}

    g_small = dict(
        mix_pre_g=jnp.concatenate([g_pre0, g_pre1], 0), mix_post_g=jnp.concatenate([gg0["mix_post_g"], gg1["mix_post_g"]], 0),
        ffn_pre_g=jnp.concatenate([gg0["ffn_pre_g"], gg1["ffn_pre_g"]], 0),
        ffn_post_g=jnp.concatenate([gg0["ffn_post_g"], gg1["ffn_post_g"]], 0),
        ple_norm_g=jnp.concatenate([gg0["ple_norm_g"], gg1["ple_norm_g"]], 0),
        hg_lb_logits=g_logits, hg_norm_g=jnp.sum(g_ng, axis=0, keepdims=True),
        conv_w=g_conv_w, conv_b=g_conv_b, rg_ba=g_ba, rg_bx=g_bx, rg_lambda=g_lam)
    order = REPLICATED + SMALL_SHARDED
    summed = dict(zip(order, _unpack(_all_reduce_rows(_pack([g_small[n] for n in order])),
                                     [g_small[n].shape for n in order])))
    grad_small = {n: summed[n].reshape(w[n].shape) for n in REPLICATED}
    for n in SMALL_SHARDED:
        width = w[n].shape[-1]
        if n in ("rg_ba", "rg_bx"):
            full_g = summed[n].reshape(1, -1, RG_BLOCK)
        else:
            full_g = summed[n].reshape((1,) + g_small[n].shape) if n == "conv_w" else summed[n]
        grad_small[n] = lax.dynamic_slice_in_dim(full_g, my_q * width, width, axis=full_g.ndim - 1).reshape(w[n].shape)

    grads = {**grad_big, **grad_small}
    delta, new_m, new_v = {}, {}, {}
    for n in BIG_NAMES:
        delta[n], new_m[n], new_v[n] = _ew(_f_adamw, [w[n], grads[n], mom[n], var[n]], [f32, f32, f32], "adamw_" + n)
    small = REPLICATED + SMALL_SHARDED
    packed = [_pack([src[n] for n in small]) for src in (w, grads, mom, var)]
    outs = _ew(_f_adamw, packed, [f32, f32, f32], "adamw_small")
    for dst, o in zip((delta, new_m, new_v), outs):
        dst.update(zip(small, _unpack(o, [w[n].shape for n in small])))
    return (loss, grad_x.reshape(x.shape), *[grads[n] for n in WEIGHTS], *[delta[n] for n in WEIGHTS],
            *[new_m[n] for n in WEIGHTS], *[new_v[n] for n in WEIGHTS])


def _all_reduce_rows(v):
    return _all_gather_rows(v, True)
```

```python
import functools

import jax
import jax.numpy as jnp
from jax import lax
from jax.experimental import pallas as pl
from jax.experimental.pallas import tpu as pltpu

f32 = jnp.float32
bf16 = jnp.bfloat16
HIGHEST = lax.Precision.HIGHEST
MESH = pl.DeviceIdType.MESH

LANES = 128
SUBLANES = 8
MXU_WIDTH = 256
VMEM_LIMIT_BYTES = 56 * 1024 * 1024

RMS_EPS = 1e-6
HEADS = 8
HEAD_DIM = 128
SB_BLOCK = 128
HG_SUB = 16
HG_CHUNK = 64
RG_BLOCK = 256
CONV_WIDTH = 4
RG_C = 8.0
ADAM_LR, ADAM_B1, ADAM_B2, ADAM_EPS, ADAM_WD, ADAM_STEP = 0.001, 0.9, 0.999, 1e-08, 0.01, 10
N_CHIPS = 4

NN = (((1,), (0,)), ((), ()))
NT = (((1,), (1,)), ((), ()))
TN = (((0,), (0,)), ((), ()))


def _params(*sem):
    return pltpu.CompilerParams(dimension_semantics=sem, vmem_limit_bytes=VMEM_LIMIT_BYTES)


class _Comm:
    def __init__(self, arrays, out_shapes, aliases, n_sems, start, finish):
        self.arrays, self.out_shapes, self.aliases, self.n_sems = list(arrays), list(out_shapes), dict(aliases), n_sems
        self.start, self.finish = start, finish


def _grid_call(body, name, steps, out_shape, in_specs, out_specs, scratch_shapes, args, comm=None):
    n_in, n_out, n_scr = len(in_specs), len(out_shape), len(scratch_shapes)
    if comm is None:
        res = pl.pallas_call(body, name=name, grid=(steps,), out_shape=list(out_shape), in_specs=list(in_specs),
                             out_specs=list(out_specs), scratch_shapes=list(scratch_shapes),
                             compiler_params=_params("parallel"))(*args)
        return list(res), []
    c_in, c_out = len(comm.arrays), len(comm.out_shapes)
    any_spec = pl.BlockSpec(memory_space=pl.ANY)

    def wrapped(*refs):
        ins, refs = refs[:n_in], refs[n_in:]
        cins, refs = refs[:c_in], refs[c_in:]
        outs, refs = refs[:n_out], refs[n_out:]
        couts, refs = refs[:c_out], refs[c_out:]
        scr, (send_sems, recv_sems) = refs[:n_scr], refs[n_scr:]

        @pl.when(pl.program_id(0) == 0)
        def _():
            comm.start(cins, couts, send_sems, recv_sems)

        body(*ins, *outs, *scr)

        @pl.when(pl.program_id(0) == steps - 1)
        def _():
            comm.finish(cins, couts, send_sems, recv_sems)

    res = pl.pallas_call(
        wrapped, name=name, grid=(steps,), out_shape=list(out_shape) + comm.out_shapes,
        in_specs=list(in_specs) + [any_spec] * c_in, out_specs=list(out_specs) + [any_spec] * c_out,
        scratch_shapes=list(scratch_shapes) + [pltpu.SemaphoreType.DMA((comm.n_sems,))] * 2,
        input_output_aliases={n_in + i: n_out + o for i, o in comm.aliases.items()},
        compiler_params=_params("arbitrary"),
    )(*args, *comm.arrays)
    return list(res[:n_out]), list(res[n_out:])


def _comm_call(comm, name):
    c_in, c_out = len(comm.arrays), len(comm.out_shapes)
    any_spec = pl.BlockSpec(memory_space=pl.ANY)

    def body(*refs):
        ins, outs, (send_sems, recv_sems) = refs[:c_in], refs[c_in:c_in + c_out], refs[c_in + c_out:]
        comm.start(ins, outs, send_sems, recv_sems)
        comm.finish(ins, outs, send_sems, recv_sems)

    return pl.pallas_call(
        body, name=name, out_shape=comm.out_shapes, in_specs=[any_spec] * c_in, out_specs=[any_spec] * c_out,
        input_output_aliases=comm.aliases, scratch_shapes=[pltpu.SemaphoreType.DMA((comm.n_sems,))] * 2,
    )(*comm.arrays)


def _pick(dim, target, align=LANES):
    if dim <= target:
        return dim
    t = (target // align) * align
    while t >= align:
        if dim % t == 0:
            return t
        t -= align
    return dim


MM_FULL_K = 2816
MM_CHUNK_K = 1408
MM_VMEM_BUDGET = 44 * 1024 * 1024


def _mm_tiles(m, n_unit, k_unit, k_whole, out_bytes):
    tk = k_unit if (k_whole and k_unit <= MM_FULL_K) else _pick(k_unit, MM_CHUNK_K)
    tk = k_unit if tk < MXU_WIDTH else tk
    tm = m if m <= 2048 else _pick(m, 512)
    tn = _pick(n_unit, 2048 if (m > 2048 or tk < k_unit or not k_whole) else 1024)
    tn = n_unit if tn < MXU_WIDTH else tn
    need = lambda: 2 * (2 * tm * tk + 2 * tk * tn + out_bytes * tm * tn)
    while need() > MM_VMEM_BUDGET:
        if tn > 512 and (tn // 2) % LANES == 0 and n_unit % (tn // 2) == 0:
            tn //= 2
        elif tm > 256:
            tm //= 2
        else:
            break
    return tm, tn, tk


def _mm(a, b, mode, out_dtype, name, *, b_shards=False, out_shards=False):
    if mode == "nn":
        m, kc = a.shape
        n = b.shape[-1] * (N_CHIPS if b_shards else 1)
    elif mode == "nt":
        m, kc = a.shape
        n = b.shape[-2]
    else:
        kc, m = a.shape
        n = b.shape[-1]
    n_sh = n // N_CHIPS
    kc_sh = kc // N_CHIPS
    n_unit = n_sh if (out_shards or (b_shards and mode == "nn")) else n
    k_split = b_shards and mode == "nt"
    tm, tn, tk = _mm_tiles(m, n_unit, kc_sh if k_split else kc, not k_split, jnp.dtype(out_dtype).itemsize)
    nk = kc // tk
    assert nk == 1 or out_dtype == f32, "a chunked contraction accumulates in the f32 output block"
    dims = {"nn": NN, "nt": NT, "tn": TN}[mode]

    def body(a_ref, b_ref, o_ref):
        part = lax.dot_general(a_ref[...], b_ref[...], dims, preferred_element_type=f32)
        if nk == 1:
            o_ref[...] = part.astype(o_ref.dtype)
        else:
            k = pl.program_id(2)

            @pl.when(k == 0)
            def _():
                o_ref[...] = part

            @pl.when(k > 0)
            def _():
                o_ref[...] += part

    if mode == "tn":
        a_spec = pl.BlockSpec((tk, tm), lambda i, j, k: (k, i))
    else:
        a_spec = pl.BlockSpec((tm, tk), lambda i, j, k: (i, k))
    nb_sh = n_sh // tn
    if mode == "nn":
        if b_shards:
            b_spec = pl.BlockSpec((None, tk, tn), lambda i, j, k: (j // nb_sh, k, j % nb_sh))
        else:
            b_spec = pl.BlockSpec((tk, tn), lambda i, j, k: (k, j))
    elif mode == "nt":
        if b_shards:
            kb_sh = kc_sh // tk
            b_spec = pl.BlockSpec((None, tn, tk), lambda i, j, k: (k // kb_sh, j, k % kb_sh))
        else:
            b_spec = pl.BlockSpec((tn, tk), lambda i, j, k: (j, k))
    else:
        b_spec = pl.BlockSpec((tk, tn), lambda i, j, k: (k, j))
    if out_shards:
        out_shape = jax.ShapeDtypeStruct((N_CHIPS, m, n_sh), out_dtype)
        o_spec = pl.BlockSpec((None, tm, tn), lambda i, j, k: (j // nb_sh, i, j % nb_sh))
    else:
        out_shape = jax.ShapeDtypeStruct((m, n), out_dtype)
        o_spec = pl.BlockSpec((tm, tn), lambda i, j, k: (i, j))
    return pl.pallas_call(
        body, name=name, out_shape=out_shape, grid=(m // tm, n // tn, nk),
        in_specs=[a_spec, b_spec], out_specs=o_spec,
        compiler_params=_params("parallel", "parallel", "arbitrary"),
    )(a, b)


def _row_spec(entry, tr):
    if isinstance(entry, tuple):
        arr, width, cb = entry[:3]
        rb = (entry[3] // tr) if len(entry) > 3 else 0
        return arr, pl.BlockSpec((tr, width), lambda i, cb=cb, rb=rb: (i + rb, cb))
    return entry, pl.BlockSpec((tr, entry.shape[1]), lambda i: (i, 0))


def _whole_spec(arr):
    return pl.BlockSpec(arr.shape, lambda i: (0,) * arr.ndim)


def _rowwise(fn, rows, bcast, outs, name, tr, row_ids=False, m=None):
    arrs, specs = zip(*[_row_spec(e, tr) for e in rows])
    m = arrs[0].shape[0] if m is None else m
    nr, nb = len(rows), len(bcast)

    def body(*refs):
        vals = [r[...] for r in refs[:nr + nb]]
        if row_ids:
            rid = pl.program_id(0) * tr + lax.broadcasted_iota(jnp.int32, (tr, 1), 0)
            res = fn(rid, *vals)
        else:
            res = fn(*vals)
        for r, o in zip(res, refs[nr + nb:]):
            o[...] = r.astype(o.dtype)

    return pl.pallas_call(
        body, name=name, grid=(m // tr,),
        out_shape=[jax.ShapeDtypeStruct((m, w), dt) for w, dt in outs],
        in_specs=list(specs) + [_whole_spec(b) for b in bcast],
        out_specs=[pl.BlockSpec((tr, w), lambda i: (i, 0)) for w, _ in outs],
        compiler_params=_params("parallel"),
    )(*arrs, *bcast)


def _rowwise_vjp(fn, rows, bcast, cts, row_grads, name, tr, row_ids=False, concat=False):
    arrs, specs = zip(*[_row_spec(e, tr) for e in rows])
    ct_arrs, ct_specs = zip(*[_row_spec(e, tr) for e in cts])
    m = arrs[0].shape[0]
    nr, nb, nc, ng = len(rows), len(bcast), len(cts), len(row_grads)
    widths = [specs[idx].block_shape[1] for idx, _ in row_grads]

    def body(*refs):
        vals = [r[...] for r in refs[:nr + nb]]
        ctv = [r[...] for r in refs[nr + nb:nr + nb + nc]]
        out_refs = refs[nr + nb + nc:]
        if row_ids:
            rid = pl.program_id(0) * tr + lax.broadcasted_iota(jnp.int32, (tr, 1), 0)
            f = functools.partial(fn, rid)
        else:
            f = fn
        res, vjp = jax.vjp(f, *vals)
        grads = vjp(tuple(c.astype(r.dtype) for c, r in zip(ctv, res)))
        if concat:
            off = 0
            for (idx, _), w in zip(row_grads, widths):
                out_refs[0][:, off:off + w] = grads[idx].astype(out_refs[0].dtype)
                off += w
            b_refs = out_refs[1:]
        else:
            for (idx, _), o in zip(row_grads, out_refs[:ng]):
                o[...] = grads[idx].astype(o.dtype)
            b_refs = out_refs[ng:]

        @pl.when(pl.program_id(0) == 0)
        def _():
            for o in b_refs:
                o[...] = jnp.zeros_like(o)

        for j, o in enumerate(b_refs):
            o[...] += grads[nr + j].astype(f32)

    if concat:
        dt = row_grads[0][1]
        g_shapes = [jax.ShapeDtypeStruct((m, sum(widths)), dt)]
        g_specs = [pl.BlockSpec((tr, sum(widths)), lambda i: (i, 0))]
    else:
        g_shapes = [jax.ShapeDtypeStruct((m, w), dt) for (_, dt), w in zip(row_grads, widths)]
        g_specs = [pl.BlockSpec((tr, w), lambda i: (i, 0)) for w in widths]
    res = pl.pallas_call(
        body, name=name, grid=(m // tr,),
        out_shape=g_shapes + [jax.ShapeDtypeStruct(b.shape, f32) for b in bcast],
        in_specs=list(specs) + [_whole_spec(b) for b in bcast] + list(ct_specs),
        out_specs=g_specs + [_whole_spec(b) for b in bcast],
        compiler_params=_params("arbitrary"),
    )(*arrs, *bcast, *ct_arrs)
    n_g = len(g_shapes)
    return list(res[:n_g]), list(res[n_g:])


def _rms(x, g):
    return x * lax.rsqrt(jnp.mean(x * x, axis=-1, keepdims=True) + RMS_EPS) * g


def _sigmoid(x):
    return 1.0 / (1.0 + jnp.exp(-x))


def _silu(x):
    return x * _sigmoid(x)


def _gelu_tanh(x):
    return 0.5 * x * (1.0 + jnp.tanh(0.7978845608028654 * (x + 0.044715 * (x * x * x))))


def _softplus(x):
    return jnp.maximum(x, 0.0) + jnp.log(1.0 + jnp.exp(-jnp.abs(x)))


def _expm1(x):
    small = jnp.abs(x) < 0.01
    series = x * (1.0 + x * (0.5 + x * (1.0 / 6.0 + x * (1.0 / 24.0))))
    return jnp.where(small, series, jnp.exp(jnp.where(small, 0.0, x)) - 1.0)


def _f_pre(h, g):
    return (_rms(h, g),)


def _f_pre_id(h, g):
    return h, _rms(h, g)


def _f_post_pre(h, m, g_post, g_pre):
    hb = h + _rms(m, g_post)
    return hb, _rms(hb, g_pre)


def _f_post(h, m, g_post):
    hc = h + _rms(m, g_post)
    return hc, hc


def _f_swiglu(g, u):
    return (_silu(g) * u,)


def _f_ple_pre(h, t, e, g_ple, g_pre):
    hd = h + _rms(_sigmoid(t) * e, g_ple)
    return hd, _rms(hd, g_pre)


def _f_ple(h, t, e, g_ple):
    return (h + _rms(_sigmoid(t) * e, g_ple),)


def _f_rg_gate(rid, ra, ix, xc, ba, bx, lam):
    r = _sigmoid(ra + ba)
    i = _sigmoid(ix + bx)
    log_a = -RG_C * r * _softplus(-lam)
    a = jnp.exp(log_a)
    mult = jnp.sqrt(-_expm1(2.0 * log_a))
    mult = jnp.where(rid == 0, 1.0, mult)
    return a, xc * i * mult


def _f_rg_out(gate, hs):
    return (_gelu_tanh(gate) * hs,)


def _loss_head(y, target, tr=256):
    m, d = y.shape

    def body(y_ref, t_ref, dy_ref, l_ref):
        err = y_ref[...] - t_ref[...]
        dy_ref[...] = err * (1.0 / d)

        @pl.when(pl.program_id(0) == 0)
        def _():
            l_ref[...] = jnp.zeros_like(l_ref)

        part = 0.5 * jnp.sum(jnp.mean(err * err, axis=-1, keepdims=True), axis=0, keepdims=True)
        l_ref[...] += jnp.broadcast_to(part, l_ref.shape)

    dy, l = pl.pallas_call(
        body, name="loss_head", grid=(m // tr,),
        out_shape=[jax.ShapeDtypeStruct((m, d), f32), jax.ShapeDtypeStruct((SUBLANES, LANES), f32)],
        in_specs=[pl.BlockSpec((tr, d), lambda i: (i, 0))] * 2,
        out_specs=[pl.BlockSpec((tr, d), lambda i: (i, 0)), pl.BlockSpec((SUBLANES, LANES), lambda i: (0, 0))],
        compiler_params=_params("arbitrary"),
    )(y, target)
    return dy, l[0, 0]


def _hilo_dot(a, u):
    hi = a.astype(bf16)
    lo = (a - hi.astype(f32)).astype(bf16)
    return (lax.dot_general(hi, u, NN, preferred_element_type=f32)
            + lax.dot_general(lo, u, NN, preferred_element_type=f32))


def _sb_scores(q_t, k_s, mask):
    z = lax.dot_general(q_t, k_s, NT, preferred_element_type=f32)
    sp = jnp.log(1.0 + jnp.exp(-jnp.abs(z)))
    a = -(jnp.maximum(z, 0.0) + sp)
    b = a + z
    if mask is not None:
        a = jnp.where(mask, a, 0.0)
    return a, b


SB_QROWS = 512
SB_DIAG = SB_QROWS // SB_BLOCK


def _sb_masks():
    row = lax.broadcasted_iota(jnp.int32, (SB_QROWS, SB_BLOCK), 0)
    col = lax.broadcasted_iota(jnp.int32, (SB_QROWS, SB_BLOCK), 1)
    tri = [d * SB_BLOCK + col < row for d in range(SB_DIAG)]
    r = lax.broadcasted_iota(jnp.int32, (SB_BLOCK, SB_BLOCK), 0)
    c = lax.broadcasted_iota(jnp.int32, (SB_BLOCK, SB_BLOCK), 1)
    u_excl = (r > c).astype(bf16)
    u_incl = (r >= c).astype(bf16)
    return tri, u_excl, u_incl


def _sb_walk(block, t, init):
    carry = init
    for d in reversed(range(SB_DIAG)):
        carry = block(t * SB_DIAG + d, carry, d)
    return lax.fori_loop(0, t * SB_DIAG, lambda i, c: block(t * SB_DIAG - 1 - i, c, None), carry)


HEADS_PER_STEP = 2


def _head_lanes(hd):
    return slice(hd * HEAD_DIM, (hd + 1) * HEAD_DIM)


def _head_cols(s):
    width = HEADS_PER_STEP * HEAD_DIM
    return lambda off: pl.BlockSpec((s, width), lambda h, off=off: (0, off // HEADS_PER_STEP + h))


def _sb_fwd(proj, comm=None):
    s = proj.shape[0]
    nblk = s // SB_BLOCK
    scale = HEAD_DIM ** -0.5

    def body(q_ref, k_ref, v_ref, o_ref, qs, ks, vs):
        qs[...] = (q_ref[...] * scale).astype(bf16)
        ks[...] = k_ref[...].astype(bf16)
        vs[...] = v_ref[...].astype(bf16)
        tri, u_excl, _ = _sb_masks()

        def block(q_t, s_idx, carry, d):
            rows = pl.ds(pl.multiple_of(s_idx * SB_BLOCK, SB_BLOCK), SB_BLOCK)
            mask = None if d is None else tri[d]
            out = []
            for hd, (ra, acc) in enumerate(carry):
                a, b = _sb_scores(q_t[hd], ks[rows, _head_lanes(hd)], mask)
                w = jnp.exp(b + _hilo_dot(a, u_excl) + ra)
                if mask is not None:
                    w = jnp.where(mask, w, 0.0)
                acc = acc + lax.dot_general(w.astype(bf16), vs[rows, _head_lanes(hd)], NN, preferred_element_type=f32)
                out.append((ra + jnp.sum(a, axis=1, keepdims=True), acc))
            return tuple(out)

        def query_tile(t, _):
            rows = pl.ds(pl.multiple_of(t * SB_QROWS, SB_QROWS), SB_QROWS)
            q_t = [qs[rows, _head_lanes(hd)] for hd in range(HEADS_PER_STEP)]
            init = ((jnp.zeros((SB_QROWS, 1), f32), jnp.zeros((SB_QROWS, HEAD_DIM), f32)),) * HEADS_PER_STEP
            carry = _sb_walk(functools.partial(block, q_t), t, init)
            for hd in range(HEADS_PER_STEP):
                o_ref[rows, _head_lanes(hd)] = carry[hd][1]
            return 0

        lax.fori_loop(0, s // SB_QROWS, query_tile, 0)

    col = _head_cols(s)
    (o,), sent = _grid_call(
        body, "sb_fwd", HEADS // HEADS_PER_STEP, [jax.ShapeDtypeStruct((s, HEADS * HEAD_DIM), f32)],
        [col(0), col(HEADS), col(2 * HEADS)], [col(0)], [pltpu.VMEM((s, HEADS_PER_STEP * HEAD_DIM), bf16)] * 3,
        (proj, proj, proj), comm)
    return o, sent


def _sb_bwd(proj, o, do, comm=None):
    s = proj.shape[0]
    nblk = s // SB_BLOCK
    scale = HEAD_DIM ** -0.5

    def body(q_ref, k_ref, v_ref, o_ref, do_ref, dq_ref, dk_ref, dv_ref, qs, ks, vs, dos, dk_acc, dv_acc):
        qs[...] = (q_ref[...] * scale).astype(bf16)
        ks[...] = k_ref[...].astype(bf16)
        vs[...] = v_ref[...].astype(bf16)
        dos[...] = do_ref[...].astype(bf16)
        dk_acc[...] = jnp.zeros_like(dk_acc)
        dv_acc[...] = jnp.zeros_like(dv_acc)
        tri, u_excl, u_incl = _sb_masks()

        def block(q_t, do_t, d_t, s_idx, carry, d):
            rows = pl.ds(pl.multiple_of(s_idx * SB_BLOCK, SB_BLOCK), SB_BLOCK)
            mask = None if d is None else tri[d]
            out = []
            for hd, (ra, rg, dq) in enumerate(carry):
                lanes = _head_lanes(hd)
                k_s = ks[rows, lanes]
                v_s = vs[rows, lanes]
                a, b = _sb_scores(q_t[hd], k_s, mask)
                w = jnp.exp(b + _hilo_dot(a, u_excl) + ra)
                if mask is not None:
                    w = jnp.where(mask, w, 0.0)
                dw = lax.dot_general(do_t[hd], v_s, NT, preferred_element_type=f32)
                wb = w.astype(bf16)
                g = wb.astype(f32) * dw
                da = d_t[hd] - (_hilo_dot(g, u_incl) + rg)
                dz = g * jnp.exp(a) - da * jnp.exp(b)
                if mask is not None:
                    dz = jnp.where(mask, dz, 0.0)
                dzb = dz.astype(bf16)
                dq = dq + lax.dot_general(dzb, k_s, NN, preferred_element_type=f32)
                dk_acc[rows, lanes] += lax.dot_general(dzb, q_t[hd], TN, preferred_element_type=f32)
                dv_acc[rows, lanes] += lax.dot_general(wb, do_t[hd], TN, preferred_element_type=f32)
                out.append((ra + jnp.sum(a, axis=1, keepdims=True), rg + jnp.sum(g, axis=1, keepdims=True), dq))
            return tuple(out)

        def query_tile(t, _):
            rows = pl.ds(pl.multiple_of(t * SB_QROWS, SB_QROWS), SB_QROWS)
            heads = range(HEADS_PER_STEP)
            q_t = [qs[rows, _head_lanes(hd)] for hd in heads]
            do_t = [dos[rows, _head_lanes(hd)] for hd in heads]
            d_t = [jnp.sum(do_t[hd].astype(f32) * o_ref[rows, _head_lanes(hd)], axis=1, keepdims=True) for hd in heads]
            zero = jnp.zeros((SB_QROWS, 1), f32)
            init = ((zero, zero, jnp.zeros((SB_QROWS, HEAD_DIM), f32)),) * HEADS_PER_STEP
            carry = _sb_walk(functools.partial(block, q_t, do_t, d_t), t, init)
            for hd in heads:
                dq_ref[rows, _head_lanes(hd)] = (carry[hd][2] * scale).astype(dq_ref.dtype)
            return 0

        lax.fori_loop(0, s // SB_QROWS, query_tile, 0)
        dk_ref[...] = dk_acc[...].astype(dk_ref.dtype)
        dv_ref[...] = dv_acc[...].astype(dv_ref.dtype)

    col = _head_cols(s)
    out = jax.ShapeDtypeStruct((s, HEADS * HEAD_DIM), bf16)
    width = HEADS_PER_STEP * HEAD_DIM
    return _grid_call(
        body, "sb_bwd", HEADS // HEADS_PER_STEP, [out, out, out], [col(0), col(HEADS), col(2 * HEADS), col(0), col(0)],
        [col(0), col(0), col(0)], [pltpu.VMEM((s, width), bf16)] * 4 + [pltpu.VMEM((s, width), f32)] * 2,
        (proj, proj, proj, o, do), comm)


def _tri_dot(x, lower):
    n = x.shape[0]
    r = lax.broadcasted_iota(jnp.int32, (n, n), 0)
    c = lax.broadcasted_iota(jnp.int32, (n, n), 1)
    tri = ((c <= r) if lower else (c >= r)).astype(bf16)
    hi = x.astype(bf16)
    rest = x - hi.astype(f32)
    mid = rest.astype(bf16)
    lo = (rest - mid.astype(f32)).astype(bf16)
    return sum(lax.dot_general(tri, part, NN, preferred_element_type=f32) for part in (hi, mid, lo))


@jax.custom_vjp
def _cumsum_rows(x):
    return _tri_dot(x, True)


_cumsum_rows.defvjp(lambda x: (_tri_dot(x, True), None), lambda _, ct: (_tri_dot(ct, False),))


@functools.partial(jax.custom_vjp, nondiff_argnums=(2,))
def _bf16_dot(a, b, dims):
    return lax.dot_general(a.astype(bf16), b.astype(bf16), dims, preferred_element_type=f32)


def _bf16_dot_bwd(dims, res, ct):
    a, b, ct = (v.astype(bf16) for v in (*res, ct))
    dot = lambda u, v, d: lax.dot_general(u, v, d, preferred_element_type=f32)
    if dims == NT:
        return dot(ct, b, NN), dot(ct, a, TN)
    return dot(b, ct, NT), dot(a, ct, NN)


_bf16_dot.defvjp(lambda a, b, dims: (_bf16_dot(a, b, dims), (a, b)), _bf16_dot_bwd)


def _hg_sub(state_t, hq, hf, hi, lb):
    n = HG_SUB
    f = lb + (1.0 - lb) * _sigmoid(hf)
    q = _silu(hq)
    k = 1.0 - f
    lf = jnp.log(f)
    cum = _cumsum_rows(lf)
    o = _bf16_dot(q * jnp.exp(cum), state_t, NT)
    causal = lax.broadcasted_iota(jnp.int32, (n, n, 1), 1) >= lax.broadcasted_iota(jnp.int32, (n, n, 1), 0)
    dec = jnp.exp(jnp.where(causal, cum[None] - cum[:, None, :], 0.0))
    score = jnp.sum(jnp.where(causal, q[None] * k[:, None, :] * dec, 0.0), axis=2, keepdims=True)
    o = o + jnp.sum(score * hi[:, None, :], axis=0)
    last = jnp.sum(lf, axis=0, keepdims=True)
    k_dec = k * jnp.exp(last - cum)
    new_state = state_t * jnp.exp(last) + _bf16_dot(hi, k_dec, TN)
    return new_state, o


def _hg_chunk(state_t, hq, hf, hi, hg, l0, l1, ng):
    lb = _sigmoid(l0 - l1)
    outs = []
    for j in range(len(hq)):
        state_t, o = _hg_sub(state_t, hq[j], hf[j], hi[j], lb)
        outs.append(_rms(o, ng) * _silu(hg[j]))
    return state_t, outs


def _hg_rows(c, j):
    return pl.ds(pl.multiple_of(c * HG_CHUNK + j * HG_SUB, HG_SUB), HG_SUB)


def _hg_pieces(ref, c, hd):
    return [ref[_hg_rows(c, j), _head_lanes(hd)] for j in range(HG_CHUNK // HG_SUB)]


def _hg_specs(s):
    lrow = pl.BlockSpec((1, HEADS_PER_STEP * HEAD_DIM), lambda h: (0, h))
    ng = pl.BlockSpec((1, HEAD_DIM), lambda h: (0, 0))
    return _head_cols(s), lrow, ng


def _hg_fwd(proj, logits, ng, comm=None):
    s = proj.shape[0]
    nch = s // HG_CHUNK
    n = HG_CHUNK // HG_SUB

    def body(hq_ref, hf_ref, hi_ref, hg_ref, l0_ref, l1_ref, ng_ref, o_ref, st_ref):
        def chunk(c, states):
            new = []
            for hd, state_t in enumerate(states):
                st_ref[hd, c] = state_t
                state_t, outs = _hg_chunk(
                    state_t, _hg_pieces(hq_ref, c, hd), _hg_pieces(hf_ref, c, hd), _hg_pieces(hi_ref, c, hd),
                    _hg_pieces(hg_ref, c, hd), l0_ref[:, _head_lanes(hd)], l1_ref[:, _head_lanes(hd)], ng_ref[...])
                for j in range(n):
                    o_ref[_hg_rows(c, j), _head_lanes(hd)] = outs[j]
                new.append(state_t)
            return tuple(new)

        lax.fori_loop(0, nch, chunk, (jnp.zeros((HEAD_DIM, HEAD_DIM), f32),) * HEADS_PER_STEP)

    col, lrow, ngs = _hg_specs(s)
    (o, states), sent = _grid_call(
        body, "hg_fwd", HEADS // HEADS_PER_STEP,
        [jax.ShapeDtypeStruct((s, HEADS * HEAD_DIM), f32), jax.ShapeDtypeStruct((HEADS, nch, HEAD_DIM, HEAD_DIM), f32)],
        [col(3 * HEADS), col(4 * HEADS), col(5 * HEADS), col(6 * HEADS), lrow, lrow, ngs],
        [col(0), pl.BlockSpec((HEADS_PER_STEP, nch, HEAD_DIM, HEAD_DIM), lambda h: (h, 0, 0, 0))], [],
        (proj, proj, proj, proj, logits[0:1], logits[1:2], ng), comm)
    return o, states, sent


def _hg_bwd(proj, logits, ng, states, do, do_off, comm=None):
    s = proj.shape[0]
    nch = s // HG_CHUNK
    n = HG_CHUNK // HG_SUB

    def body(hq_ref, hf_ref, hi_ref, hg_ref, l0_ref, l1_ref, ng_ref, st_ref, do_ref,
             dq_ref, df_ref, di_ref, dg_ref, dl0_ref, dl1_ref, dng_ref):
        dl0_ref[...] = jnp.zeros_like(dl0_ref)
        dl1_ref[...] = jnp.zeros_like(dl1_ref)
        dng_ref[...] = jnp.zeros_like(dng_ref)

        def chunk(i, dstates):
            c = nch - 1 - i
            new = []
            for hd, dstate in enumerate(dstates):
                lanes = _head_lanes(hd)
                args = (st_ref[hd, c], _hg_pieces(hq_ref, c, hd), _hg_pieces(hf_ref, c, hd), _hg_pieces(hi_ref, c, hd),
                        _hg_pieces(hg_ref, c, hd), l0_ref[:, lanes], l1_ref[:, lanes], ng_ref[...])
                _, vjp = jax.vjp(_hg_chunk, *args)
                dst, dq, df, di, dg, dl0, dl1, dng = vjp((dstate, _hg_pieces(do_ref, c, hd)))
                for j in range(n):
                    rows = _hg_rows(c, j)
                    dq_ref[rows, lanes] = dq[j].astype(dq_ref.dtype)
                    df_ref[rows, lanes] = df[j].astype(df_ref.dtype)
                    di_ref[rows, lanes] = di[j].astype(di_ref.dtype)
                    dg_ref[rows, lanes] = dg[j].astype(dg_ref.dtype)
                dl0_ref[:, lanes] += dl0
                dl1_ref[:, lanes] += dl1
                dng_ref[:, lanes] += dng
                new.append(dst)
            return tuple(new)

        lax.fori_loop(0, nch, chunk, (jnp.zeros((HEAD_DIM, HEAD_DIM), f32),) * HEADS_PER_STEP)

    col, lrow, ngs = _hg_specs(s)
    big = jax.ShapeDtypeStruct((s, HEADS * HEAD_DIM), bf16)
    vec = jax.ShapeDtypeStruct((1, HEADS * HEAD_DIM), f32)
    (dq, df, di, dg, dl0, dl1, dng), sent = _grid_call(
        body, "hg_bwd", HEADS // HEADS_PER_STEP, [big, big, big, big, vec, vec, vec],
        [col(3 * HEADS), col(4 * HEADS), col(5 * HEADS), col(6 * HEADS), lrow, lrow, ngs,
         pl.BlockSpec((HEADS_PER_STEP, nch, HEAD_DIM, HEAD_DIM), lambda h: (h, 0, 0, 0)), col(do_off)],
        [col(0), col(0), col(0), col(0), lrow, lrow, lrow], [],
        (proj, proj, proj, proj, logits[0:1], logits[1:2], ng, states, do), comm)
    return (jnp.concatenate([dq, df, di, dg], axis=1), jnp.concatenate([dl0, dl1], axis=0),
            dng.reshape(HEADS, HEAD_DIM), sent)


def _shift_rows(x, k, down):
    if k == 0:
        return x
    s = x.shape[0]
    row = lax.broadcasted_iota(jnp.int32, x.shape, 0)
    if down:
        return jnp.where(row < k, 0.0, pltpu.roll(x, k, 0))
    return jnp.where(row >= s - k, 0.0, pltpu.roll(x, s - k, 0))


def _conv_fwd(proj, w, b):
    s, w2 = proj.shape
    width = w2 // 2
    nb = width // RG_BLOCK

    def body(x_ref, w_ref, b_ref, y_ref, yb_ref):
        x = x_ref[...]
        y = jnp.broadcast_to(b_ref[...], x.shape)
        for k in range(CONV_WIDTH):
            y = y + _shift_rows(x, k, True) * w_ref[pl.ds(k, 1), :]
        y_ref[...] = y
        yb_ref[...] = y.astype(bf16)

    blk = pl.BlockSpec((s, RG_BLOCK), lambda j: (0, j))
    return pl.pallas_call(
        body, name="conv_fwd", grid=(nb,),
        out_shape=[jax.ShapeDtypeStruct((s, width), f32), jax.ShapeDtypeStruct((s, width), bf16)],
        in_specs=[pl.BlockSpec((s, RG_BLOCK), lambda j: (0, nb + j)),
                  pl.BlockSpec((CONV_WIDTH, RG_BLOCK), lambda j: (0, j)), pl.BlockSpec((1, RG_BLOCK), lambda j: (0, j))],
        out_specs=[blk, blk],
        compiler_params=_params("parallel"),
    )(proj, w, b)


def _conv_bwd(proj, w, dy_a, dy_b):
    s, w2 = proj.shape
    width = w2 // 2
    nb = width // RG_BLOCK

    def body(x_ref, w_ref, dya_ref, dyb_ref, dx_ref, dw_ref, db_ref):
        x = x_ref[...]
        dy_v = dya_ref[...] + dyb_ref[...]
        dx = jnp.zeros_like(x)
        for k in range(CONV_WIDTH):
            dx = dx + _shift_rows(dy_v, k, False) * w_ref[pl.ds(k, 1), :]
            dw_ref[pl.ds(k, 1), :] = jnp.sum(dy_v * _shift_rows(x, k, True), axis=0, keepdims=True)
        dx_ref[...] = dx.astype(dx_ref.dtype)
        db_ref[...] = jnp.sum(dy_v, axis=0, keepdims=True)

    blk = pl.BlockSpec((s, RG_BLOCK), lambda j: (0, j))
    wblk = pl.BlockSpec((CONV_WIDTH, RG_BLOCK), lambda j: (0, j))
    bblk = pl.BlockSpec((1, RG_BLOCK), lambda j: (0, j))
    return pl.pallas_call(
        body, name="conv_bwd", grid=(nb,),
        out_shape=[jax.ShapeDtypeStruct((s, width), bf16), jax.ShapeDtypeStruct((CONV_WIDTH, width), f32),
                   jax.ShapeDtypeStruct((1, width), f32)],
        in_specs=[pl.BlockSpec((s, RG_BLOCK), lambda j: (0, nb + j)), wblk, blk, blk],
        out_specs=[blk, wblk, bblk],
        compiler_params=_params("parallel"),
    )(proj, w, dy_a, dy_b)


def _rg_rows(w_ref):
    return jnp.concatenate([w_ref[q] for q in range(N_CHIPS)], axis=0)


def _bd_fwd(xb, wa, wx):
    s, width = xb.shape
    nb = width // RG_BLOCK

    def body(x_ref, wa_ref, wx_ref, ra_ref, ix_ref):
        x = x_ref[...]
        ra_ref[...] = lax.dot_general(x, _rg_rows(wa_ref), NN, preferred_element_type=f32)
        ix_ref[...] = lax.dot_general(x, _rg_rows(wx_ref), NN, preferred_element_type=f32)

    blk = pl.BlockSpec((s, RG_BLOCK), lambda j: (0, j))
    wblk = pl.BlockSpec((N_CHIPS, None, RG_BLOCK // N_CHIPS, RG_BLOCK), lambda j: (0, j, 0, 0))
    out = jax.ShapeDtypeStruct((s, width), f32)
    return pl.pallas_call(
        body, name="bd_fwd", grid=(nb,), out_shape=[out, out],
        in_specs=[blk, wblk, wblk], out_specs=[blk, blk],
        compiler_params=_params("parallel"),
    )(xb, wa, wx)


def _bd_bwd(xb, wa, wx, dra, dix):
    s, width = xb.shape
    nb = width // RG_BLOCK
    rq = RG_BLOCK // N_CHIPS

    def body(x_ref, wa_ref, wx_ref, dra_ref, dix_ref, dx_ref, dwa_ref, dwx_ref):
        x = x_ref[...]
        dra_v = dra_ref[...]
        dix_v = dix_ref[...]
        dx_ref[...] = (lax.dot_general(dra_v, _rg_rows(wa_ref), NT, preferred_element_type=f32)
                       + lax.dot_general(dix_v, _rg_rows(wx_ref), NT, preferred_element_type=f32))
        dwa = lax.dot_general(x, dra_v, TN, preferred_element_type=f32)
        dwx = lax.dot_general(x, dix_v, TN, preferred_element_type=f32)
        for q in range(N_CHIPS):
            dwa_ref[q] = dwa[q * rq:(q + 1) * rq, :].astype(dwa_ref.dtype)
            dwx_ref[q] = dwx[q * rq:(q + 1) * rq, :].astype(dwx_ref.dtype)

    blk = pl.BlockSpec((s, RG_BLOCK), lambda j: (0, j))
    wblk = pl.BlockSpec((N_CHIPS, None, rq, RG_BLOCK), lambda j: (0, j, 0, 0))
    wout = jax.ShapeDtypeStruct((N_CHIPS, nb, rq, RG_BLOCK), bf16)
    return pl.pallas_call(
        body, name="bd_bwd", grid=(nb,),
        out_shape=[jax.ShapeDtypeStruct((s, width), f32), wout, wout],
        in_specs=[blk, wblk, wblk, blk, blk], out_specs=[blk, wblk, wblk],
        compiler_params=_params("parallel"),
    )(xb, wa, wx, dra, dix)


def _scan_tiles(a_ref, u_ref, out_ref, n_tiles, reverse):
    cb = a_ref.shape[1]
    row = lax.broadcasted_iota(jnp.int32, (SUBLANES, cb), 0)

    def tile(i, carry):
        t = (n_tiles - 1 - i) if reverse else i
        rows = pl.ds(pl.multiple_of(t * SUBLANES, SUBLANES), SUBLANES)
        a = a_ref[rows, :]
        b = u_ref[rows, :]
        for d in (1, 2, 4):
            if reverse:
                valid = row < SUBLANES - d
                shift = SUBLANES - d
            else:
                valid = row >= d
                shift = d
            a_n = jnp.where(valid, pltpu.roll(a, shift, 0), 1.0)
            b_n = jnp.where(valid, pltpu.roll(b, shift, 0), 0.0)
            b = a * b_n + b
            a = a * a_n
        out = a * carry + b
        out_ref[rows, :] = out
        edge = t * SUBLANES + (0 if reverse else SUBLANES - 1)
        return jnp.broadcast_to(out_ref[pl.ds(edge, 1), :], (SUBLANES, cb))

    lax.fori_loop(0, n_tiles, tile, jnp.zeros((SUBLANES, cb), f32))


def _scan_fwd(a, u):
    s, width = a.shape

    def body(a_ref, u_ref, h_ref):
        _scan_tiles(a_ref, u_ref, h_ref, s // SUBLANES, False)

    blk = pl.BlockSpec((s, RG_BLOCK), lambda j: (0, j))
    return pl.pallas_call(
        body, name="scan_fwd", grid=(width // RG_BLOCK,), out_shape=jax.ShapeDtypeStruct((s, width), f32),
        in_specs=[blk, blk], out_specs=blk, compiler_params=_params("parallel"),
    )(a, u)


def _scan_bwd(a, h, dh):
    s, width = a.shape

    def body(a_ref, h_ref, dh_ref, du_ref, da_ref, a_next):
        a_next[...] = _shift_rows(a_ref[...], 1, False)
        _scan_tiles(a_next, dh_ref, du_ref, s // SUBLANES, True)
        da_ref[...] = du_ref[...] * _shift_rows(h_ref[...], 1, True)

    blk = pl.BlockSpec((s, RG_BLOCK), lambda j: (0, j))
    out = jax.ShapeDtypeStruct((s, width), f32)
    return pl.pallas_call(
        body, name="scan_bwd", grid=(width // RG_BLOCK,), out_shape=[out, out],
        in_specs=[blk, blk, blk], out_specs=[blk, blk],
        scratch_shapes=[pltpu.VMEM((s, RG_BLOCK), f32)],
        compiler_params=_params("parallel"),
    )(a, h, dh)


EW_TILE_BYTES = 2 << 20


def _ew_rows(rows, width):
    limit = max(16, EW_TILE_BYTES // (4 * width))
    t = (min(limit, rows) // 16) * 16
    while t >= 16:
        if rows % t == 0:
            return t
        t -= 16
    return rows


def _as2d(a):
    return a.reshape(-1, a.shape[-1])


def _ew(fn, ins, out_dtypes, name):
    shape = ins[0].shape
    ins2 = [_as2d(a) for a in ins]
    rows, width = ins2[0].shape
    res = _rowwise(fn, ins2, [], [(width, dt) for dt in out_dtypes], name, _ew_rows(rows, width))
    return [r.reshape(shape) for r in res]


def _f_adamw(w, g, m, v):
    m = ADAM_B1 * m + (1.0 - ADAM_B1) * g
    v = ADAM_B2 * v + (1.0 - ADAM_B2) * (g * g)
    m_hat = m / (1.0 - ADAM_B1 ** ADAM_STEP)
    v_hat = v / (1.0 - ADAM_B2 ** ADAM_STEP)
    delta = -ADAM_LR * (m_hat / (jnp.sqrt(v_hat) + ADAM_EPS) + ADAM_WD * w)
    return delta, m, v


ANY = pl.BlockSpec(memory_space=pl.ANY)


def _place():
    x, y, c = lax.axis_index("x"), lax.axis_index("y"), lax.axis_index("c")
    return x, y, c, [(1 - x, y), (x, 1 - y), (1 - x, 1 - y)]


def _rcopy(src, dst, send_sem, recv_sem, dev):
    return pltpu.make_async_remote_copy(src_ref=src, dst_ref=dst, send_sem=send_sem, recv_sem=recv_sem,
                                        device_id=dev, device_id_type=MESH)


def _half(ref, h, kind, lead=0):
    ax = lead + (1 if kind == "rg" else 0)
    n = ref.shape[ax] // 2
    idx = [slice(None)] * ref.ndim
    idx[ax] = pl.ds(h * n, n)
    return ref.at[tuple(idx)]


def _gather_weights(slots, kinds):
    n = len(slots)

    def first(out, send_sems, recv_sems):
        x, y, c, chips = _place()
        return [_rcopy(mine, mine, send_sems.at[6 * e + j], recv_sems.at[6 * e + j], (*chip, c))
                for e in range(n) for j, chip in enumerate(chips)
                for mine in [_half(out[e].at[2 * x + y], c, kinds[e])]]

    def start(ins, out, send_sems, recv_sems):
        for cp in first(out, send_sems, recv_sems):
            cp.start()

    def finish(ins, out, send_sems, recv_sems):
        x, y, c, chips = _place()
        sibling = (x, y, 1 - c)
        passed_on = []
        for e in range(n):
            for j, chip in enumerate(chips):
                landed = _half(out[e].at[2 * chip[0] + chip[1]], c, kinds[e])
                _rcopy(landed, landed, send_sems.at[6 * e + j], recv_sems.at[6 * e + j], (*chip, c)).wait_recv()
                cp = _rcopy(landed, landed, send_sems.at[6 * e + 3 + j], recv_sems.at[6 * e + 3 + j], sibling)
                cp.start()
                passed_on.append(cp)
        for e in range(n):
            for j, chip in enumerate(chips):
                passed = _half(out[e].at[2 * chip[0] + chip[1]], 1 - c, kinds[e])
                _rcopy(passed, passed, send_sems.at[6 * e + 3 + j], recv_sems.at[6 * e + 3 + j], sibling).wait_recv()
        for cp in first(out, send_sems, recv_sems) + passed_on:
            cp.wait_send()

    return _Comm(slots, [jax.ShapeDtypeStruct(s.shape, s.dtype) for s in slots], {e: e for e in range(n)}, 6 * n,
                 start, finish)


def _exchange_halves(grads, kinds):
    n = len(grads)

    def copies(g, theirs, send_sems, recv_sems):
        x, y, c, _ = _place()
        return [_rcopy(_half(g[e], 1 - c, kinds[e], 1), theirs[e], send_sems.at[e], recv_sems.at[e], (x, y, 1 - c))
                for e in range(n)]

    def start(*refs):
        for cp in copies(*refs):
            cp.start()

    def finish(*refs):
        for cp in copies(*refs):
            cp.wait()

    def half_shape(s, kind):
        shp = list(s.shape)
        shp[2 if kind == "rg" else 1] //= 2
        return jax.ShapeDtypeStruct(tuple(shp), s.dtype)

    return _Comm(grads, [half_shape(s, k) for s, k in zip(grads, kinds)], {}, n, start, finish)


def _exchange_chips(parts):
    n = len(parts)

    def copies(p, got, send_sems, recv_sems):
        x, y, c, chips = _place()
        return [_rcopy(p[e].at[2 * chip[0] + chip[1]], got[e].at[j], send_sems.at[3 * e + j], recv_sems.at[3 * e + j],
                       (*chip, c))
                for e in range(n) for j, chip in enumerate(chips)]

    def start(*refs):
        for cp in copies(*refs):
            cp.start()

    def finish(*refs):
        for cp in copies(*refs):
            cp.wait()

    return _Comm(parts, [jax.ShapeDtypeStruct((N_CHIPS - 1,) + s.shape[1:], s.dtype) for s in parts], {}, 3 * n,
                 start, finish)


def _share_halves(shards, kinds):
    n = len(shards)

    def copies(out, send_sems, recv_sems, h):
        x, y, c, _ = _place()
        return [_rcopy(part, part, send_sems.at[e], recv_sems.at[e], (x, y, 1 - c))
                for e in range(n) for part in [_half(out[e], c if h == "mine" else 1 - c, kinds[e])]]

    def start(ins, out, send_sems, recv_sems):
        for cp in copies(out, send_sems, recv_sems, "mine"):
            cp.start()

    def finish(ins, out, send_sems, recv_sems):
        for cp in copies(out, send_sems, recv_sems, "other"):
            cp.wait_recv()
        for cp in copies(out, send_sems, recv_sems, "mine"):
            cp.wait_send()

    return _Comm(shards, [jax.ShapeDtypeStruct(s.shape, s.dtype) for s in shards], {e: e for e in range(n)}, n,
                 start, finish)


def _tiles_call(fn, scalars, ins, outs, steps, name):
    n_in = len(ins)

    def body(s_ref, *refs):
        res = fn(*[r[...] for r in refs[:n_in]])
        for r, o in zip(res, refs[n_in:]):
            o[...] = r.astype(o.dtype)

    spec = lambda blk, idx: pl.BlockSpec(blk, lambda i, s, idx=idx: (idx(i, s), 0))
    return pl.pallas_call(
        body, name=name, out_shape=[o[0] for o in outs],
        grid_spec=pltpu.PrefetchScalarGridSpec(
            num_scalar_prefetch=1, grid=(steps,),
            in_specs=[spec(blk, idx) for _, blk, idx in ins], out_specs=[spec(blk, idx) for _, blk, idx in outs]),
        compiler_params=_params("arbitrary"),
    )(scalars, *[a for a, _, _ in ins])


def _groups(kind, shard_shape):
    if kind == "rg":
        return shard_shape[0], shard_shape[1], shard_shape[2]
    return 1, shard_shape[0], shard_shape[1]


def _half_tiles(kind, shard_shape):
    g, rg, width = _groups(kind, shard_shape)
    tr = _ew_rows(rg // 2, width)
    nt = (rg // 2) // tr
    in_whole = lambda i, c: (i // nt) * (rg // tr) + c * nt + i % nt
    return g, rg, width, tr, nt, in_whole


def _cast_into_slot(w, layer, kind, scalars):
    shard_shape = w.shape[1:]
    g, rg, width, tr, nt, _ = _half_tiles(kind, shard_shape)
    per_shard = g * rg // tr
    (buf,) = _tiles_call(
        lambda v: (v,), scalars, [(_as2d(w), (tr, width), lambda i, s: layer * per_shard + i)],
        [(jax.ShapeDtypeStruct((N_CHIPS * g * rg, width), bf16), (tr, width), lambda i, s: s[0] * per_shard + i)],
        per_shard, "cast_" + kind)
    return buf.reshape((N_CHIPS,) + shard_shape)


def _add_cores(grad, theirs, kind, scalars):
    g, rg, width, tr, nt, in_whole = _half_tiles(kind, grad.shape[1:])
    steps = N_CHIPS * g * nt
    (out,) = _tiles_call(
        lambda u, v: (u.astype(f32) + v.astype(f32),), scalars,
        [(_as2d(grad), (tr, width), lambda i, s: in_whole(i, s[1])), (_as2d(theirs), (tr, width), lambda i, s: i)],
        [(jax.ShapeDtypeStruct((steps * tr, width), bf16), (tr, width), lambda i, s: i)], steps, "rs_add_cores")
    return out.reshape(theirs.shape)


def _sum_chips(part, got, kind, scalars):
    shard_shape = list(part.shape[1:])
    shard_shape[1 if kind == "rg" else 0] *= 2
    g, rg, width, tr, nt, in_whole = _half_tiles(kind, shard_shape)
    steps = g * nt
    up = lambda t: t.astype(f32)
    g2 = _as2d(got)
    ins = [(_as2d(part), (tr, width), lambda i, s: s[0] * steps + i)]
    ins += [(g2, (tr, width), lambda i, s, j=j: j * steps + i) for j in range(N_CHIPS - 1)]
    (out,) = _tiles_call(
        lambda a, b, c, e: (((up(a) + up(b)) + up(c)) + up(e),), scalars, ins,
        [(jax.ShapeDtypeStruct((g * rg, width), f32), (tr, width), lambda i, s: in_whole(i, s[1]))], steps, "rs_add_chips")
    return out.reshape(shard_shape)


def _all_gather_rows(v, reduce):
    rows = v.shape[0]

    def body(v_ref, out_ref, *rest):
        if reduce:
            sum_ref, send_sems, recv_sems, local_sem = rest
        else:
            send_sems, recv_sems, local_sem = rest
        x, y, c, chips = _place()
        me, sibling = (x, y, c), (x, y, 1 - c)

        def blk(px, py, pc):
            return out_ref.at[pl.ds((4 * px + 2 * py + pc) * rows, rows), :]

        def copy(k, block, to, src=None):
            return _rcopy(blk(*block) if src is None else src, blk(*block), send_sems.at[k], recv_sems.at[k], to)

        mine = pltpu.make_async_copy(v_ref, blk(*me), local_sem)
        mine.start()
        first = [copy(0, me, sibling, src=v_ref)] + [copy(1 + j, me, (*chip, c), src=v_ref) for j, chip in enumerate(chips)]
        for cp in first:
            cp.start()
        passed = [copy(4 + j, (*chip, c), sibling) for j, chip in enumerate(chips)]
        for j, chip in enumerate(chips):
            copy(1 + j, (*chip, c), me).wait_recv()
            passed[j].start()
        copy(0, sibling, me).wait_recv()
        for j, chip in enumerate(chips):
            copy(4 + j, (*chip, 1 - c), me).wait_recv()
        for cp in first + passed:
            cp.wait_send()
        mine.wait()
        if reduce:
            acc = out_ref[pl.ds(0, rows), :]
            for d in range(1, 8):
                acc = acc + out_ref[pl.ds(d * rows, rows), :]
            sum_ref[...] = acc

    vm = pl.BlockSpec(memory_space=pltpu.VMEM)
    gathered = jax.ShapeDtypeStruct((8 * rows, LANES), f32)
    res = pl.pallas_call(
        body, name="all_reduce_small" if reduce else "all_gather_small",
        out_shape=[gathered, jax.ShapeDtypeStruct((rows, LANES), f32)] if reduce else [gathered],
        in_specs=[vm], out_specs=[vm, vm] if reduce else [vm],
        scratch_shapes=[pltpu.SemaphoreType.DMA((7,)), pltpu.SemaphoreType.DMA((7,)), pltpu.SemaphoreType.DMA],
    )(v)
    return res[1] if reduce else res[0]


def _pack(parts):
    flat = jnp.concatenate([p.reshape(-1).astype(f32) for p in parts])
    rows = -(-flat.shape[0] // (SUBLANES * LANES)) * SUBLANES
    return jnp.pad(flat, (0, rows * LANES - flat.shape[0])).reshape(rows, LANES)


def _unpack(packed, shapes):
    flat = packed.reshape(-1)
    out, off = [], 0
    for shp in shapes:
        size = 1
        for d in shp:
            size *= d
        out.append(flat[off:off + size].reshape(shp))
        off += size
    return out


WEIGHTS = ["mix_pre_g", "mix_post_g", "ffn_pre_g", "ffn_post_g", "ple_norm_g", "w_in_even", "w_out_even",
           "hg_lb_logits", "hg_norm_g", "w_in_odd", "conv_w", "conv_b", "rg_wa", "rg_ba", "rg_wx", "rg_bx",
           "rg_lambda", "w_out_odd", "w_gate_up", "w_down", "w_ple_up", "w_ple_gate"]
BIG = [("w_in_even", 0, "col"), ("w_out_even", 0, "row"), ("w_gate_up", 0, "col"), ("w_down", 0, "row"),
       ("w_ple_up", 0, "col"), ("w_ple_gate", 0, "row"), ("w_in_odd", 0, "col"), ("rg_wa", 0, "rg"),
       ("rg_wx", 0, "rg"), ("w_out_odd", 0, "row"), ("w_gate_up", 1, "col"), ("w_down", 1, "row"),
       ("w_ple_up", 1, "col"), ("w_ple_gate", 1, "row")]
GATHER_FIRST, GATHER_BEHIND_SB, GATHER_BEHIND_HG = [0, 1], [2, 3, 4, 5], [6, 7, 8, 9, 10, 11, 12, 13]
REDUCE_BEHIND_SB, REDUCE_BEHIND_HG, REDUCE_LAST = [6, 7, 8, 9, 10, 11, 12, 13], [2, 3, 4, 5], [0, 1]
BIG_NAMES = ["w_in_even", "w_out_even", "w_gate_up", "w_down", "w_ple_up", "w_ple_gate", "w_in_odd", "rg_wa", "rg_wx",
             "w_out_odd"]
SMALL_SHARDED = ["conv_w", "conv_b", "rg_ba", "rg_bx", "rg_lambda"]
REPLICATED = ["mix_pre_g", "mix_post_g", "ffn_pre_g", "ffn_post_g", "ple_norm_g", "hg_lb_logits", "hg_norm_g"]
TR = 256
TR_WIDE = 128


def _ffn_ple_fwd(h, m, l, gains, w_gu, w_d, w_pu, w_pg, pb, next_pre):
    d = h.shape[1]
    ff = w_d.shape[0]
    hb, nf = _rowwise(_f_post_pre, [h, m], [gains["mix_post_g"][l], gains["ffn_pre_g"][l]], [(d, f32), (d, bf16)],
                      f"post_mix{l}", TR)
    gu = _mm(nf, w_gu, "nn", f32, f"ffn_up{l}", b_shards=True)
    (act,) = _rowwise(_f_swiglu, [(gu, ff, 0), (gu, ff, 1)], [], [(ff, bf16)], f"swiglu{l}", TR_WIDE)
    f = _mm(act, w_d, "nn", f32, f"ffn_down{l}")
    hc, hcb = _rowwise(_f_post, [hb, f], [gains["ffn_post_g"][l]], [(d, f32), (d, bf16)], f"post_ffn{l}", TR)
    t = _mm(hcb, w_pg, "nn", f32, f"ple_gate{l}")
    e = _mm(pb, w_pu, "nn", f32, f"ple_up{l}", b_shards=True)
    if next_pre is None:
        (hd,) = _rowwise(_f_ple, [hc, t, e], [gains["ple_norm_g"][l]], [(d, f32)], f"ple{l}", TR)
        n_next = None
    else:
        hd, n_next = _rowwise(_f_ple_pre, [hc, t, e], [gains["ple_norm_g"][l], next_pre], [(d, f32), (d, bf16)],
                              f"ple{l}", TR)
    saved = dict(h=h, m=m, hb=hb, nf=nf, gu=gu, act=act, f=f, hc=hc, hcb=hcb, t=t, e=e)
    return hd, n_next, saved


def _ffn_ple_bwd(sv, l, gains, w_gu, w_d, w_pg, pb, next_pre, dhd, dn_next):
    ff = w_d.shape[0]
    if next_pre is None:
        (dhc, dt, de), (dg_ple,) = _rowwise_vjp(_f_ple, [sv["hc"], sv["t"], sv["e"]], [gains["ple_norm_g"][l]], [dhd],
                                                [(0, f32), (1, bf16), (2, bf16)], f"ple_bwd{l}", TR)
        dg_next = None
    else:
        (dhc, dt, de), (dg_ple, dg_next) = _rowwise_vjp(
            _f_ple_pre, [sv["hc"], sv["t"], sv["e"]], [gains["ple_norm_g"][l], next_pre], [dhd, dn_next],
            [(0, f32), (1, bf16), (2, bf16)], f"ple_bwd{l}", TR)
    gw = {}
    gw["w_ple_gate"] = _mm(sv["hcb"], dt, "tn", bf16, f"ple_gate_wgrad{l}")
    gw["w_ple_up"] = _mm(pb, de, "tn", bf16, f"ple_up_wgrad{l}", out_shards=True)
    dhcb = _mm(dt, w_pg, "nt", f32, f"ple_gate_dgrad{l}")
    (dhb, df), (dg_fpost,) = _rowwise_vjp(_f_post, [sv["hb"], sv["f"]], [gains["ffn_post_g"][l]], [dhc, dhcb],
                                          [(0, f32), (1, bf16)], f"post_ffn_bwd{l}", TR)
    gw["w_down"] = _mm(sv["act"], df, "tn", bf16, f"ffn_down_wgrad{l}")
    dact = _mm(df, w_d, "nt", f32, f"ffn_down_dgrad{l}")
    (dgu,), _ = _rowwise_vjp(_f_swiglu, [(sv["gu"], ff, 0), (sv["gu"], ff, 1)], [], [dact], [(0, bf16), (1, bf16)],
                             f"swiglu_bwd{l}", TR_WIDE, concat=True)
    gw["w_gate_up"] = _mm(sv["nf"], dgu, "tn", bf16, f"ffn_up_wgrad{l}", out_shards=True)
    dnf = _mm(dgu, w_gu, "nt", f32, f"ffn_up_dgrad{l}", b_shards=True)
    (dh, dm), (dg_post, dg_fpre) = _rowwise_vjp(
        _f_post_pre, [sv["h"], sv["m"]], [gains["mix_post_g"][l], gains["ffn_pre_g"][l]], [dhb, dnf],
        [(0, f32), (1, bf16)], f"post_mix_bwd{l}", TR)
    gg = dict(ple_norm_g=dg_ple, ffn_post_g=dg_fpost, mix_post_g=dg_post, ffn_pre_g=dg_fpre)
    return dh, dm, gw, gg, dg_next


def kernel(x, p, mix_pre_g, mix_post_g, ffn_pre_g, ffn_post_g, ple_norm_g, w_in_even, w_out_even, hg_lb_logits, hg_norm_g, w_in_odd, conv_w, conv_b, rg_wa, rg_ba, rg_wx, rg_bx, rg_lambda, w_out_odd, w_gate_up, w_down, w_ple_up, w_ple_gate, loss_target, m_mix_pre_g, m_mix_post_g, m_ffn_pre_g, m_ffn_post_g, m_ple_norm_g, m_w_in_even, m_w_out_even, m_hg_lb_logits, m_hg_norm_g, m_w_in_odd, m_conv_w, m_conv_b, m_rg_wa, m_rg_ba, m_rg_wx, m_rg_bx, m_rg_lambda, m_w_out_odd, m_w_gate_up, m_w_down, m_w_ple_up, m_w_ple_gate, v_mix_pre_g, v_mix_post_g, v_ffn_pre_g, v_ffn_post_g, v_ple_norm_g, v_w_in_even, v_w_out_even, v_hg_lb_logits, v_hg_norm_g, v_w_in_odd, v_conv_w, v_conv_b, v_rg_wa, v_rg_ba, v_rg_wx, v_rg_bx, v_rg_lambda, v_w_out_odd, v_w_gate_up, v_w_down, v_w_ple_up, v_w_ple_gate):
    given = dict(locals())
    w = {n: given[n] for n in WEIGHTS}
    mom = {n: given["m_" + n] for n in WEIGHTS}
    var = {n: given["v_" + n] for n in WEIGHTS}
    xi, yi = lax.axis_index("x"), lax.axis_index("y")
    my_q = 2 * xi + yi
    h0 = x[0]
    s, d = h0.shape
    n_layers = p.shape[0]
    gains = {n: [w[n][l:l + 1] for l in range(n_layers)]
             for n in ("mix_pre_g", "mix_post_g", "ffn_pre_g", "ffn_post_g", "ple_norm_g")}

    kinds = [k for _, _, k in BIG]
    place = jnp.stack([my_q, lax.axis_index("c")]).astype(jnp.int32)
    slots = [_cast_into_slot(w[n], l, k, place) for n, l, k in BIG]
    full = {}

    def gather(group):
        return _gather_weights([slots[i] for i in group], [kinds[i] for i in group])

    def gathered(group, bufs):
        full.update({BIG[i][:2]: b for i, b in zip(group, bufs)})

    gathered(GATHER_FIRST, _comm_call(gather(GATHER_FIRST), "gather_first"))

    def rows_full(name, l):
        g = full[(name, l)]
        return g.reshape(g.shape[0] * g.shape[1], g.shape[2])

    small_shapes = [w[n].shape[1:] for n in SMALL_SHARDED]
    small_rows = _all_gather_rows(_pack([w[n][0] for n in SMALL_SHARDED]), False)
    per_chip = [_unpack(small_rows.reshape(8, -1)[2 * q], small_shapes) for q in range(N_CHIPS)]
    conv_w_f, conv_b_f, rg_ba_f, rg_bx_f, rg_lam_f = [
        jnp.concatenate([per_chip[q][i] for q in range(N_CHIPS)], axis=-1) for i in range(len(SMALL_SHARDED))]
    lru = conv_b_f.shape[0]
    conv_b_f, rg_ba_f, rg_bx_f, rg_lam_f = [v.reshape(1, lru) for v in (conv_b_f, rg_ba_f, rg_bx_f, rg_lam_f)]
    pb = [_ew(lambda v: (v,), [p[l, 0]], [bf16], f"cast_p{l}")[0] for l in range(n_layers)]

    (n0,) = _rowwise(_f_pre, [h0], [gains["mix_pre_g"][0]], [(d, bf16)], "pre_mix0", TR)
    proj0 = _mm(n0, full[("w_in_even", 0)], "nn", f32, "in_even", b_shards=True)
    a_out, bufs = _sb_fwd(proj0, gather(GATHER_BEHIND_SB))
    gathered(GATHER_BEHIND_SB, bufs)
    b_out, hg_states, bufs = _hg_fwd(proj0, w["hg_lb_logits"], w["hg_norm_g"], gather(GATHER_BEHIND_HG))
    gathered(GATHER_BEHIND_HG, bufs)
    mix0 = jnp.concatenate([a_out, b_out], axis=1).astype(bf16)
    m0 = _mm(mix0, rows_full("w_out_even", 0), "nn", f32, "out_even")
    h1, n1, sv0 = _ffn_ple_fwd(h0, m0, 0, gains, full[("w_gate_up", 0)], rows_full("w_down", 0), full[("w_ple_up", 0)],
                               rows_full("w_ple_gate", 0), pb[0], gains["mix_pre_g"][1])

    proj1 = _mm(n1, full[("w_in_odd", 0)], "nn", f32, "in_odd", b_shards=True)
    xc, xcb = _conv_fwd(proj1, conv_w_f, conv_b_f)
    ra, ix = _bd_fwd(xcb, full[("rg_wa", 0)], full[("rg_wx", 0)])
    gate_in = [ra, ix, xc], [rg_ba_f, rg_bx_f, rg_lam_f]
    a_rec, u_rec = _rowwise(_f_rg_gate, *gate_in, [(lru, f32), (lru, f32)], "rg_gate", TR_WIDE, row_ids=True)
    hs = _scan_fwd(a_rec, u_rec)
    out_in = [(proj1, lru, 0), hs]
    (yb,) = _rowwise(_f_rg_out, out_in, [], [(lru, bf16)], "rg_out", TR_WIDE)
    m1 = _mm(yb, rows_full("w_out_odd", 0), "nn", f32, "out_odd")
    h2, _, sv1 = _ffn_ple_fwd(h1, m1, 1, gains, full[("w_gate_up", 1)], rows_full("w_down", 1), full[("w_ple_up", 1)],
                              rows_full("w_ple_gate", 1), pb[1], None)
    dy, loss_local = _loss_head(h2, loss_target[0])
    loss = lax.psum(loss_local, ("x", "y", "c"))

    gbig, chip_part, got = {}, {}, {}

    def core_sums(group, tag):
        gl = []
        for i in group:
            n_, l_, k_ = BIG[i]
            g_ = gbig[(n_, l_)]
            gl.append(g_.reshape(N_CHIPS, g_.shape[0] // N_CHIPS, g_.shape[1]) if k_ == "row" else g_)
        theirs = _comm_call(_exchange_halves(gl, [kinds[i] for i in group]), "rs_core_exchange_" + tag)
        chip_part.update({i: _add_cores(g_, t_, kinds[i], place) for i, g_, t_ in zip(group, gl, theirs)})
        return _exchange_chips([chip_part[i] for i in group])

    dh1, dm1, gw, gg1, _ = _ffn_ple_bwd(sv1, 1, gains, full[("w_gate_up", 1)], rows_full("w_down", 1),
                                        rows_full("w_ple_gate", 1), pb[1], None, dy, None)
    gbig.update({(n, 1): g for n, g in gw.items()})
    gbig[("w_out_odd", 0)] = _mm(yb, dm1, "tn", bf16, "out_odd_wgrad")
    dyo = _mm(dm1, rows_full("w_out_odd", 0), "nt", f32, "out_odd_dgrad")
    (dgate, dhs), _ = _rowwise_vjp(_f_rg_out, out_in, [], [dyo], [(0, bf16), (1, f32)], "rg_out_bwd", TR_WIDE)
    du, da = _scan_bwd(a_rec, hs, dhs)
    (dra, dix, dxc_direct), (g_ba, g_bx, g_lam) = _rowwise_vjp(
        _f_rg_gate, *gate_in, [da, du], [(0, bf16), (1, bf16), (2, f32)], "rg_gate_bwd", TR_WIDE, row_ids=True)
    dxc_gates, gbig[("rg_wa", 0)], gbig[("rg_wx", 0)] = _bd_bwd(xcb, full[("rg_wa", 0)], full[("rg_wx", 0)], dra, dix)
    dxb, g_conv_w, g_conv_b = _conv_bwd(proj1, conv_w_f, dxc_direct, dxc_gates)
    dproj1 = jnp.concatenate([dgate, dxb], axis=1)
    gbig[("w_in_odd", 0)] = _mm(n1, dproj1, "tn", bf16, "in_odd_wgrad", out_shards=True)
    dn1 = _mm(dproj1, full[("w_in_odd", 0)], "nt", f32, "in_odd_dgrad", b_shards=True)
    chips_layer1 = core_sums(REDUCE_BEHIND_SB, "layer1")

    dh0, dm0, gw, gg0, g_pre1 = _ffn_ple_bwd(sv0, 0, gains, full[("w_gate_up", 0)], rows_full("w_down", 0),
                                             rows_full("w_ple_gate", 0), pb[0], gains["mix_pre_g"][1], dh1, dn1)
    gbig.update({(n, 0): g for n, g in gw.items()})
    chips_ffn0 = core_sums(REDUCE_BEHIND_HG, "ffn0")
    gbig[("w_out_even", 0)] = _mm(mix0, dm0, "tn", bf16, "out_even_wgrad")
    dmix = _mm(dm0, rows_full("w_out_even", 0), "nt", f32, "out_even_dgrad")
    (dq, dk, dv), bufs = _sb_bwd(proj0, a_out, dmix, chips_layer1)
    got.update(zip(REDUCE_BEHIND_SB, bufs))
    dhg, g_logits, g_ng, bufs = _hg_bwd(proj0, w["hg_lb_logits"], w["hg_norm_g"], hg_states, dmix, HEADS, chips_ffn0)
    got.update(zip(REDUCE_BEHIND_HG, bufs))
    dproj0 = jnp.concatenate([dq, dk, dv, dhg], axis=1)
    gbig[("w_in_even", 0)] = _mm(n0, dproj0, "tn", bf16, "in_even_wgrad", out_shards=True)
    dn0 = _mm(dproj0, full[("w_in_even", 0)], "nt", f32, "in_even_dgrad", b_shards=True)
    (grad_x,), (g_pre0,) = _rowwise_vjp(_f_pre_id, [h0], [gains["mix_pre_g"][0]], [dh0, dn0], [(0, f32)], "pre_mix0_bwd", TR)
    got.update(zip(REDUCE_LAST, _comm_call(core_sums(REDUCE_LAST, "mix0"), "rs_chip_exchange_mix0")))
    reduced = _comm_call(_share_halves([_sum_chips(chip_part[i], got[i], kinds[i], place) for i in range(len(BIG))],
                                       kinds), "rs_share_halves")
    by_layer = {(n, l): r for (n, l, _), r in zip(BIG, reduced)}
    grad_big = {n: jnp.stack([by_layer[(n, l)] for l in range(w[n].shape[0])]) for n in BIG_NAMES}

    g_small = dict(
        mix_pre_g=jnp.concatenate([g_pre0, g_pre1], 0), mix_post_g=jnp.concatenate([gg0["mix_post_g"], gg1["mix_post_g"]], 0),
        ffn_pre_g=jnp.concatenate([gg0["ffn_pre_g"], gg1["ffn_pre_g"]], 0),
        ffn_post_g=jnp.concatenate([gg0["ffn_post_g"], gg1["ffn_post_g"]], 0),
        ple_norm_g=jnp.concatenate([gg0["ple_norm_g"], gg1["ple_norm_g"]], 0),
        hg_lb_logits=g_logits, hg_norm_g=jnp.sum(g_ng, axis=0, keepdims=True),
        conv_w=g_conv_w, conv_b=g_conv_b, rg_ba=g_ba, rg_bx=g_bx, rg_lambda=g_lam)
    order = REPLICATED + SMALL_SHARDED
    summed = dict(zip(order, _unpack(_all_reduce_rows(_pack([g_small[n] for n in order])),
                                     [g_small[n].shape for n in order])))
    grad_small = {n: summed[n].reshape(w[n].shape) for n in REPLICATED}
    for n in SMALL_SHARDED:
        width = w[n].shape[-1]
        if n in ("rg_ba", "rg_bx"):
            full_g = summed[n].reshape(1, -1, RG_BLOCK)
        else:
            full_g = summed[n].reshape((1,) + g_small[n].shape) if n == "conv_w" else summed[n]
        grad_small[n] = lax.dynamic_slice_in_dim(full_g, my_q * width, width, axis=full_g.ndim - 1).reshape(w[n].shape)

    grads = {**grad_big, **grad_small}
    delta, new_m, new_v = {}, {}, {}
    for n in BIG_NAMES:
        delta[n], new_m[n], new_v[n] = _ew(_f_adamw, [w[n], grads[n], mom[n], var[n]], [f32, f32, f32], "adamw_" + n)
    small = REPLICATED + SMALL_SHARDED
    packed = [_pack([src[n] for n in small]) for src in (w, grads, mom, var)]
    outs = _ew(_f_adamw, packed, [f32, f32, f32], "adamw_small")
    for dst, o in zip((delta, new_m, new_v), outs):
        dst.update(zip(small, _unpack(o, [w[n].shape for n in small])))
    return (loss, grad_x.reshape(x.shape), *[grads[n] for n in WEIGHTS], *[delta[n] for n in WEIGHTS],
            *[new_m[n] for n in WEIGHTS], *[new_v[n] for n in WEIGHTS])


def _all_reduce_rows(v):
    return _all_gather_rows(v, True)
```

```python
import functools

import jax
import jax.numpy as jnp
from jax import lax
from jax.experimental import pallas as pl
from jax.experimental.pallas import tpu as pltpu

f32 = jnp.float32
bf16 = jnp.bfloat16
HIGHEST = lax.Precision.HIGHEST
MESH = pl.DeviceIdType.MESH

LANES = 128
SUBLANES = 8
MXU_WIDTH = 256
VMEM_LIMIT_BYTES = 56 * 1024 * 1024

RMS_EPS = 1e-6
HEADS = 8
HEAD_DIM = 128
SB_BLOCK = 128
HG_SUB = 16
HG_CHUNK = 64
RG_BLOCK = 256
CONV_WIDTH = 4
RG_C = 8.0
ADAM_LR, ADAM_B1, ADAM_B2, ADAM_EPS, ADAM_WD, ADAM_STEP = 0.001, 0.9, 0.999, 1e-08, 0.01, 10
N_CHIPS = 4

NN = (((1,), (0,)), ((), ()))
NT = (((1,), (1,)), ((), ()))
TN = (((0,), (0,)), ((), ()))


def _params(*sem):
    return pltpu.CompilerParams(dimension_semantics=sem, vmem_limit_bytes=VMEM_LIMIT_BYTES)


class _Comm:
    def __init__(self, arrays, out_shapes, aliases, n_sems, start, finish):
        self.arrays, self.out_shapes, self.aliases, self.n_sems = list(arrays), list(out_shapes), dict(aliases), n_sems
        self.start, self.finish = start, finish


def _grid_call(body, name, steps, out_shape, in_specs, out_specs, scratch_shapes, args, comm=None):
    n_in, n_out, n_scr = len(in_specs), len(out_shape), len(scratch_shapes)
    if comm is None:
        res = pl.pallas_call(body, name=name, grid=(steps,), out_shape=list(out_shape), in_specs=list(in_specs),
                             out_specs=list(out_specs), scratch_shapes=list(scratch_shapes),
                             compiler_params=_params("parallel"))(*args)
        return list(res), []
    c_in, c_out = len(comm.arrays), len(comm.out_shapes)
    any_spec = pl.BlockSpec(memory_space=pl.ANY)

    def wrapped(*refs):
        ins, refs = refs[:n_in], refs[n_in:]
        cins, refs = refs[:c_in], refs[c_in:]
        outs, refs = refs[:n_out], refs[n_out:]
        couts, refs = refs[:c_out], refs[c_out:]
        scr, (send_sems, recv_sems) = refs[:n_scr], refs[n_scr:]

        @pl.when(pl.program_id(0) == 0)
        def _():
            comm.start(cins, couts, send_sems, recv_sems)

        body(*ins, *outs, *scr)

        @pl.when(pl.program_id(0) == steps - 1)
        def _():
            comm.finish(cins, couts, send_sems, recv_sems)

    res = pl.pallas_call(
        wrapped, name=name, grid=(steps,), out_shape=list(out_shape) + comm.out_shapes,
        in_specs=list(in_specs) + [any_spec] * c_in, out_specs=list(out_specs) + [any_spec] * c_out,
        scratch_shapes=list(scratch_shapes) + [pltpu.SemaphoreType.DMA((comm.n_sems,))] * 2,
        input_output_aliases={n_in + i: n_out + o for i, o in comm.aliases.items()},
        compiler_params=_params("arbitrary"),
    )(*args, *comm.arrays)
    return list(res[:n_out]), list(res[n_out:])


def _comm_call(comm, name):
    c_in, c_out = len(comm.arrays), len(comm.out_shapes)
    any_spec = pl.BlockSpec(memory_space=pl.ANY)

    def body(*refs):
        ins, outs, (send_sems, recv_sems) = refs[:c_in], refs[c_in:c_in + c_out], refs[c_in + c_out:]
        comm.start(ins, outs, send_sems, recv_sems)
        comm.finish(ins, outs, send_sems, recv_sems)

    return pl.pallas_call(
        body, name=name, out_shape=comm.out_shapes, in_specs=[any_spec] * c_in, out_specs=[any_spec] * c_out,
        input_output_aliases=comm.aliases, scratch_shapes=[pltpu.SemaphoreType.DMA((comm.n_sems,))] * 2,
    )(*comm.arrays)


def _copies_start(make_copies, arrays, n_sems, name, after):
    n = len(arrays)
    hbm = pl.BlockSpec(memory_space=pltpu.HBM)
    sem = pl.BlockSpec(memory_space=pltpu.SEMAPHORE)

    def body(*refs):
        thru = refs[n + 3:2 * n + 3]
        for cp in make_copies(thru, refs[n + 1], refs[n + 2]):
            cp.start()
        refs[2 * n + 3][...] = jnp.zeros_like(refs[2 * n + 3])

    res = pl.pallas_call(
        body, name=name,
        out_shape=[pltpu.SemaphoreType.DMA((n_sems,)), pltpu.SemaphoreType.DMA((n_sems,))]
        + [pltpu.HBM(a.shape, a.dtype) for a in arrays] + [jax.ShapeDtypeStruct((SUBLANES, LANES), f32)],
        in_specs=[hbm] * n + [pl.BlockSpec(memory_space=pl.ANY)],
        out_specs=[sem, sem] + [hbm] * n + [pl.BlockSpec(memory_space=pltpu.VMEM)],
        input_output_aliases={i: 2 + i for i in range(n)},
        compiler_params=pltpu.CompilerParams(has_side_effects=pltpu.SideEffectType.DATAFLOW_SIDE_EFFECTING),
    )(*[pltpu.with_memory_space_constraint(a, pltpu.HBM) for a in arrays], after)
    return res[0], res[1], list(res[2:2 + n]), res[2 + n]


def _copies_wait(make_copies, send_sems, recv_sems, arrays, name, after):
    n = len(arrays)
    hbm = pl.BlockSpec(memory_space=pltpu.HBM)
    sem = pl.BlockSpec(memory_space=pltpu.SEMAPHORE)

    def body(*refs):
        for cp in make_copies(refs[:n], refs[n], refs[n + 1]):
            cp.wait_send()
            cp.wait_recv()

    return list(pl.pallas_call(
        body, name=name, out_shape=[pltpu.HBM(a.shape, a.dtype) for a in arrays],
        in_specs=[hbm] * n + [sem, sem] + [pl.BlockSpec(memory_space=pl.ANY)] * len(after), out_specs=[hbm] * n,
        input_output_aliases={i: i for i in range(n)},
        compiler_params=pltpu.CompilerParams(has_side_effects=pltpu.SideEffectType.DATAFLOW_SIDE_EFFECTING),
    )(*arrays, send_sems, recv_sems, *after))


def _pick(dim, target, align=LANES):
    if dim <= target:
        return dim
    t = (target // align) * align
    while t >= align:
        if dim % t == 0:
            return t
        t -= align
    return dim


MM_FULL_K = 2816
MM_CHUNK_K = 1408
MM_VMEM_BUDGET = 44 * 1024 * 1024


def _mm_tiles(m, n_unit, k_unit, k_whole, out_bytes):
    tk = k_unit if (k_whole and k_unit <= MM_FULL_K) else _pick(k_unit, MM_CHUNK_K)
    tk = k_unit if tk < MXU_WIDTH else tk
    tm = m if m <= 2048 else _pick(m, 512)
    tn = _pick(n_unit, 2048 if (m > 2048 or tk < k_unit or not k_whole) else 1024)
    tn = n_unit if tn < MXU_WIDTH else tn
    need = lambda: 2 * (2 * tm * tk + 2 * tk * tn + out_bytes * tm * tn)
    while need() > MM_VMEM_BUDGET:
        if tn > 512 and (tn // 2) % LANES == 0 and n_unit % (tn // 2) == 0:
            tn //= 2
        elif tm > 256:
            tm //= 2
        else:
            break
    return tm, tn, tk


def _mm(a, b, mode, out_dtype, name, *, b_shards=False, out_shards=False, after=None):
    if mode == "nn":
        m, kc = a.shape
        n = b.shape[-1] * (N_CHIPS if b_shards else 1)
    elif mode == "nt":
        m, kc = a.shape
        n = b.shape[-2]
    else:
        kc, m = a.shape
        n = b.shape[-1]
    n_sh = n // N_CHIPS
    kc_sh = kc // N_CHIPS
    n_unit = n_sh if (out_shards or (b_shards and mode == "nn")) else n
    k_split = b_shards and mode == "nt"
    tm, tn, tk = _mm_tiles(m, n_unit, kc_sh if k_split else kc, not k_split, jnp.dtype(out_dtype).itemsize)
    nk = kc // tk
    assert nk == 1 or out_dtype == f32, "a chunked contraction accumulates in the f32 output block"
    dims = {"nn": NN, "nt": NT, "tn": TN}[mode]

    def body(a_ref, b_ref, *rest):
        o_ref = rest[-1]
        part = lax.dot_general(a_ref[...], b_ref[...], dims, preferred_element_type=f32)
        if nk == 1:
            o_ref[...] = part.astype(o_ref.dtype)
        else:
            k = pl.program_id(2)

            @pl.when(k == 0)
            def _():
                o_ref[...] = part

            @pl.when(k > 0)
            def _():
                o_ref[...] += part

    if mode == "tn":
        a_spec = pl.BlockSpec((tk, tm), lambda i, j, k: (k, i))
    else:
        a_spec = pl.BlockSpec((tm, tk), lambda i, j, k: (i, k))
    nb_sh = n_sh // tn
    if mode == "nn":
        if b_shards:
            b_spec = pl.BlockSpec((None, tk, tn), lambda i, j, k: (j // nb_sh, k, j % nb_sh))
        else:
            b_spec = pl.BlockSpec((tk, tn), lambda i, j, k: (k, j))
    elif mode == "nt":
        if b_shards:
            kb_sh = kc_sh // tk
            b_spec = pl.BlockSpec((None, tn, tk), lambda i, j, k: (k // kb_sh, j, k % kb_sh))
        else:
            b_spec = pl.BlockSpec((tn, tk), lambda i, j, k: (j, k))
    else:
        b_spec = pl.BlockSpec((tk, tn), lambda i, j, k: (k, j))
    if out_shards:
        out_shape = jax.ShapeDtypeStruct((N_CHIPS, m, n_sh), out_dtype)
        o_spec = pl.BlockSpec((None, tm, tn), lambda i, j, k: (j // nb_sh, i, j % nb_sh))
    else:
        out_shape = jax.ShapeDtypeStruct((m, n), out_dtype)
        o_spec = pl.BlockSpec((tm, tn), lambda i, j, k: (i, j))
    order = [] if after is None else [after]
    return pl.pallas_call(
        body, name=name, out_shape=out_shape, grid=(m // tm, n // tn, nk),
        in_specs=[a_spec, b_spec] + [pl.BlockSpec(memory_space=pl.ANY)] * len(order), out_specs=o_spec,
        compiler_params=_params("parallel", "parallel", "arbitrary"),
    )(a, b, *order)


def _row_spec(entry, tr):
    if isinstance(entry, tuple):
        arr, width, cb = entry[:3]
        rb = (entry[3] // tr) if len(entry) > 3 else 0
        return arr, pl.BlockSpec((tr, width), lambda i, cb=cb, rb=rb: (i + rb, cb))
    return entry, pl.BlockSpec((tr, entry.shape[1]), lambda i: (i, 0))


def _whole_spec(arr):
    return pl.BlockSpec(arr.shape, lambda i: (0,) * arr.ndim)


def _rowwise(fn, rows, bcast, outs, name, tr, row_ids=False, m=None):
    arrs, specs = zip(*[_row_spec(e, tr) for e in rows])
    m = arrs[0].shape[0] if m is None else m
    nr, nb = len(rows), len(bcast)

    def body(*refs):
        vals = [r[...] for r in refs[:nr + nb]]
        if row_ids:
            rid = pl.program_id(0) * tr + lax.broadcasted_iota(jnp.int32, (tr, 1), 0)
            res = fn(rid, *vals)
        else:
            res = fn(*vals)
        for r, o in zip(res, refs[nr + nb:]):
            o[...] = r.astype(o.dtype)

    return pl.pallas_call(
        body, name=name, grid=(m // tr,),
        out_shape=[jax.ShapeDtypeStruct((m, w), dt) for w, dt in outs],
        in_specs=list(specs) + [_whole_spec(b) for b in bcast],
        out_specs=[pl.BlockSpec((tr, w), lambda i: (i, 0)) for w, _ in outs],
        compiler_params=_params("parallel"),
    )(*arrs, *bcast)


def _rowwise_vjp(fn, rows, bcast, cts, row_grads, name, tr, row_ids=False, concat=False):
    arrs, specs = zip(*[_row_spec(e, tr) for e in rows])
    ct_arrs, ct_specs = zip(*[_row_spec(e, tr) for e in cts])
    m = arrs[0].shape[0]
    nr, nb, nc, ng = len(rows), len(bcast), len(cts), len(row_grads)
    widths = [specs[idx].block_shape[1] for idx, _ in row_grads]

    def body(*refs):
        vals = [r[...] for r in refs[:nr + nb]]
        ctv = [r[...] for r in refs[nr + nb:nr + nb + nc]]
        out_refs = refs[nr + nb + nc:]
        if row_ids:
            rid = pl.program_id(0) * tr + lax.broadcasted_iota(jnp.int32, (tr, 1), 0)
            f = functools.partial(fn, rid)
        else:
            f = fn
        res, vjp = jax.vjp(f, *vals)
        grads = vjp(tuple(c.astype(r.dtype) for c, r in zip(ctv, res)))
        if concat:
            off = 0
            for (idx, _), w in zip(row_grads, widths):
                out_refs[0][:, off:off + w] = grads[idx].astype(out_refs[0].dtype)
                off += w
            b_refs = out_refs[1:]
        else:
            for (idx, _), o in zip(row_grads, out_refs[:ng]):
                o[...] = grads[idx].astype(o.dtype)
            b_refs = out_refs[ng:]

        @pl.when(pl.program_id(0) == 0)
        def _():
            for o in b_refs:
                o[...] = jnp.zeros_like(o)

        for j, o in enumerate(b_refs):
            o[...] += grads[nr + j].astype(f32)

    if concat:
        dt = row_grads[0][1]
        g_shapes = [jax.ShapeDtypeStruct((m, sum(widths)), dt)]
        g_specs = [pl.BlockSpec((tr, sum(widths)), lambda i: (i, 0))]
    else:
        g_shapes = [jax.ShapeDtypeStruct((m, w), dt) for (_, dt), w in zip(row_grads, widths)]
        g_specs = [pl.BlockSpec((tr, w), lambda i: (i, 0)) for w in widths]
    res = pl.pallas_call(
        body, name=name, grid=(m // tr,),
        out_shape=g_shapes + [jax.ShapeDtypeStruct(b.shape, f32) for b in bcast],
        in_specs=list(specs) + [_whole_spec(b) for b in bcast] + list(ct_specs),
        out_specs=g_specs + [_whole_spec(b) for b in bcast],
        compiler_params=_params("arbitrary"),
    )(*arrs, *bcast, *ct_arrs)
    n_g = len(g_shapes)
    return list(res[:n_g]), list(res[n_g:])


def _rms(x, g):
    return x * lax.rsqrt(jnp.mean(x * x, axis=-1, keepdims=True) + RMS_EPS) * g


def _sigmoid(x):
    return 1.0 / (1.0 + jnp.exp(-x))


def _silu(x):
    return x * _sigmoid(x)


def _gelu_tanh(x):
    return 0.5 * x * (1.0 + jnp.tanh(0.7978845608028654 * (x + 0.044715 * (x * x * x))))


def _softplus(x):
    return jnp.maximum(x, 0.0) + jnp.log(1.0 + jnp.exp(-jnp.abs(x)))


def _expm1(x):
    small = jnp.abs(x) < 0.01
    series = x * (1.0 + x * (0.5 + x * (1.0 / 6.0 + x * (1.0 / 24.0))))
    return jnp.where(small, series, jnp.exp(jnp.where(small, 0.0, x)) - 1.0)


def _f_pre(h, g):
    return (_rms(h, g),)


def _f_pre_id(h, g):
    return h, _rms(h, g)


def _f_post_pre(h, m, g_post, g_pre):
    hb = h + _rms(m, g_post)
    return hb, _rms(hb, g_pre)


def _f_post(h, m, g_post):
    hc = h + _rms(m, g_post)
    return hc, hc


def _f_swiglu(g, u):
    return (_silu(g) * u,)


def _f_ple_pre(h, t, e, g_ple, g_pre):
    hd = h + _rms(_sigmoid(t) * e, g_ple)
    return hd, _rms(hd, g_pre)


def _f_ple(h, t, e, g_ple):
    return (h + _rms(_sigmoid(t) * e, g_ple),)


def _f_rg_gate(rid, ra, ix, xc, ba, bx, lam):
    r = _sigmoid(ra + ba)
    i = _sigmoid(ix + bx)
    log_a = -RG_C * r * _softplus(-lam)
    a = jnp.exp(log_a)
    mult = jnp.sqrt(-_expm1(2.0 * log_a))
    mult = jnp.where(rid == 0, 1.0, mult)
    return a, xc * i * mult


def _f_rg_out(gate, hs):
    return (_gelu_tanh(gate) * hs,)


def _loss_head(y, target, tr=256):
    m, d = y.shape

    def body(y_ref, t_ref, dy_ref, l_ref):
        err = y_ref[...] - t_ref[...]
        dy_ref[...] = err * (1.0 / d)

        @pl.when(pl.program_id(0) == 0)
        def _():
            l_ref[...] = jnp.zeros_like(l_ref)

        part = 0.5 * jnp.sum(jnp.mean(err * err, axis=-1, keepdims=True), axis=0, keepdims=True)
        l_ref[...] += jnp.broadcast_to(part, l_ref.shape)

    dy, l = pl.pallas_call(
        body, name="loss_head", grid=(m // tr,),
        out_shape=[jax.ShapeDtypeStruct((m, d), f32), jax.ShapeDtypeStruct((SUBLANES, LANES), f32)],
        in_specs=[pl.BlockSpec((tr, d), lambda i: (i, 0))] * 2,
        out_specs=[pl.BlockSpec((tr, d), lambda i: (i, 0)), pl.BlockSpec((SUBLANES, LANES), lambda i: (0, 0))],
        compiler_params=_params("arbitrary"),
    )(y, target)
    return dy, l[0, 0]


def _hilo_dot(a, u):
    hi = a.astype(bf16)
    lo = (a - hi.astype(f32)).astype(bf16)
    return (lax.dot_general(hi, u, NN, preferred_element_type=f32)
            + lax.dot_general(lo, u, NN, preferred_element_type=f32))


def _sb_scores(q_t, k_s, mask):
    z = lax.dot_general(q_t, k_s, NT, preferred_element_type=f32)
    sp = jnp.log(1.0 + jnp.exp(-jnp.abs(z)))
    a = -(jnp.maximum(z, 0.0) + sp)
    b = a + z
    if mask is not None:
        a = jnp.where(mask, a, 0.0)
    return a, b


SB_QROWS = 512
SB_DIAG = SB_QROWS // SB_BLOCK


def _sb_masks():
    row = lax.broadcasted_iota(jnp.int32, (SB_QROWS, SB_BLOCK), 0)
    col = lax.broadcasted_iota(jnp.int32, (SB_QROWS, SB_BLOCK), 1)
    tri = [d * SB_BLOCK + col < row for d in range(SB_DIAG)]
    r = lax.broadcasted_iota(jnp.int32, (SB_BLOCK, SB_BLOCK), 0)
    c = lax.broadcasted_iota(jnp.int32, (SB_BLOCK, SB_BLOCK), 1)
    u_excl = (r > c).astype(bf16)
    u_incl = (r >= c).astype(bf16)
    return tri, u_excl, u_incl


def _sb_walk(block, t, init):
    carry = init
    for d in reversed(range(SB_DIAG)):
        carry = block(t * SB_DIAG + d, carry, d)
    return lax.fori_loop(0, t * SB_DIAG, lambda i, c: block(t * SB_DIAG - 1 - i, c, None), carry)


HEADS_PER_STEP = 2


def _head_lanes(hd):
    return slice(hd * HEAD_DIM, (hd + 1) * HEAD_DIM)


def _head_cols(s):
    width = HEADS_PER_STEP * HEAD_DIM
    return lambda off: pl.BlockSpec((s, width), lambda h, off=off: (0, off // HEADS_PER_STEP + h))


def _sb_fwd(proj, comm=None):
    s = proj.shape[0]
    nblk = s // SB_BLOCK
    scale = HEAD_DIM ** -0.5

    def body(q_ref, k_ref, v_ref, o_ref, qs, ks, vs):
        qs[...] = (q_ref[...] * scale).astype(bf16)
        ks[...] = k_ref[...].astype(bf16)
        vs[...] = v_ref[...].astype(bf16)
        tri, u_excl, _ = _sb_masks()

        def block(q_t, s_idx, carry, d):
            rows = pl.ds(pl.multiple_of(s_idx * SB_BLOCK, SB_BLOCK), SB_BLOCK)
            mask = None if d is None else tri[d]
            out = []
            for hd, (ra, acc) in enumerate(carry):
                a, b = _sb_scores(q_t[hd], ks[rows, _head_lanes(hd)], mask)
                w = jnp.exp(b + _hilo_dot(a, u_excl) + ra)
                if mask is not None:
                    w = jnp.where(mask, w, 0.0)
                acc = acc + lax.dot_general(w.astype(bf16), vs[rows, _head_lanes(hd)], NN, preferred_element_type=f32)
                out.append((ra + jnp.sum(a, axis=1, keepdims=True), acc))
            return tuple(out)

        def query_tile(t, _):
            rows = pl.ds(pl.multiple_of(t * SB_QROWS, SB_QROWS), SB_QROWS)
            q_t = [qs[rows, _head_lanes(hd)] for hd in range(HEADS_PER_STEP)]
            init = ((jnp.zeros((SB_QROWS, 1), f32), jnp.zeros((SB_QROWS, HEAD_DIM), f32)),) * HEADS_PER_STEP
            carry = _sb_walk(functools.partial(block, q_t), t, init)
            for hd in range(HEADS_PER_STEP):
                o_ref[rows, _head_lanes(hd)] = carry[hd][1]
            return 0

        lax.fori_loop(0, s // SB_QROWS, query_tile, 0)

    col = _head_cols(s)
    (o,), sent = _grid_call(
        body, "sb_fwd", HEADS // HEADS_PER_STEP, [jax.ShapeDtypeStruct((s, HEADS * HEAD_DIM), f32)],
        [col(0), col(HEADS), col(2 * HEADS)], [col(0)], [pltpu.VMEM((s, HEADS_PER_STEP * HEAD_DIM), bf16)] * 3,
        (proj, proj, proj), comm)
    return o, sent


def _sb_bwd(proj, o, do, comm=None):
    s = proj.shape[0]
    nblk = s // SB_BLOCK
    scale = HEAD_DIM ** -0.5

    def body(q_ref, k_ref, v_ref, o_ref, do_ref, dq_ref, dk_ref, dv_ref, qs, ks, vs, dos, dk_acc, dv_acc):
        qs[...] = (q_ref[...] * scale).astype(bf16)
        ks[...] = k_ref[...].astype(bf16)
        vs[...] = v_ref[...].astype(bf16)
        dos[...] = do_ref[...].astype(bf16)
        dk_acc[...] = jnp.zeros_like(dk_acc)
        dv_acc[...] = jnp.zeros_like(dv_acc)
        tri, u_excl, u_incl = _sb_masks()

        def block(q_t, do_t, d_t, s_idx, carry, d):
            rows = pl.ds(pl.multiple_of(s_idx * SB_BLOCK, SB_BLOCK), SB_BLOCK)
            mask = None if d is None else tri[d]
            out = []
            for hd, (ra, rg, dq) in enumerate(carry):
                lanes = _head_lanes(hd)
                k_s = ks[rows, lanes]
                v_s = vs[rows, lanes]
                a, b = _sb_scores(q_t[hd], k_s, mask)
                w = jnp.exp(b + _hilo_dot(a, u_excl) + ra)
                if mask is not None:
                    w = jnp.where(mask, w, 0.0)
                dw = lax.dot_general(do_t[hd], v_s, NT, preferred_element_type=f32)
                wb = w.astype(bf16)
                g = wb.astype(f32) * dw
                da = d_t[hd] - (_hilo_dot(g, u_incl) + rg)
                dz = g * jnp.exp(a) - da * jnp.exp(b)
                if mask is not None:
                    dz = jnp.where(mask, dz, 0.0)
                dzb = dz.astype(bf16)
                dq = dq + lax.dot_general(dzb, k_s, NN, preferred_element_type=f32)
                dk_acc[rows, lanes] += lax.dot_general(dzb, q_t[hd], TN, preferred_element_type=f32)
                dv_acc[rows, lanes] += lax.dot_general(wb, do_t[hd], TN, preferred_element_type=f32)
                out.append((ra + jnp.sum(a, axis=1, keepdims=True), rg + jnp.sum(g, axis=1, keepdims=True), dq))
            return tuple(out)

        def query_tile(t, _):
            rows = pl.ds(pl.multiple_of(t * SB_QROWS, SB_QROWS), SB_QROWS)
            heads = range(HEADS_PER_STEP)
            q_t = [qs[rows, _head_lanes(hd)] for hd in heads]
            do_t = [dos[rows, _head_lanes(hd)] for hd in heads]
            d_t = [jnp.sum(do_t[hd].astype(f32) * o_ref[rows, _head_lanes(hd)], axis=1, keepdims=True) for hd in heads]
            zero = jnp.zeros((SB_QROWS, 1), f32)
            init = ((zero, zero, jnp.zeros((SB_QROWS, HEAD_DIM), f32)),) * HEADS_PER_STEP
            carry = _sb_walk(functools.partial(block, q_t, do_t, d_t), t, init)
            for hd in heads:
                dq_ref[rows, _head_lanes(hd)] = (carry[hd][2] * scale).astype(dq_ref.dtype)
            return 0

        lax.fori_loop(0, s // SB_QROWS, query_tile, 0)
        dk_ref[...] = dk_acc[...].astype(dk_ref.dtype)
        dv_ref[...] = dv_acc[...].astype(dv_ref.dtype)

    col = _head_cols(s)
    out = jax.ShapeDtypeStruct((s, HEADS * HEAD_DIM), bf16)
    width = HEADS_PER_STEP * HEAD_DIM
    return _grid_call(
        body, "sb_bwd", HEADS // HEADS_PER_STEP, [out, out, out], [col(0), col(HEADS), col(2 * HEADS), col(0), col(0)],
        [col(0), col(0), col(0)], [pltpu.VMEM((s, width), bf16)] * 4 + [pltpu.VMEM((s, width), f32)] * 2,
        (proj, proj, proj, o, do), comm)


def _tri_dot(x, lower):
    n = x.shape[0]
    r = lax.broadcasted_iota(jnp.int32, (n, n), 0)
    c = lax.broadcasted_iota(jnp.int32, (n, n), 1)
    tri = ((c <= r) if lower else (c >= r)).astype(bf16)
    hi = x.astype(bf16)
    rest = x - hi.astype(f32)
    mid = rest.astype(bf16)
    lo = (rest - mid.astype(f32)).astype(bf16)
    return sum(lax.dot_general(tri, part, NN, preferred_element_type=f32) for part in (hi, mid, lo))


@jax.custom_vjp
def _cumsum_rows(x):
    return _tri_dot(x, True)


_cumsum_rows.defvjp(lambda x: (_tri_dot(x, True), None), lambda _, ct: (_tri_dot(ct, False),))


@functools.partial(jax.custom_vjp, nondiff_argnums=(2,))
def _bf16_dot(a, b, dims):
    return lax.dot_general(a.astype(bf16), b.astype(bf16), dims, preferred_element_type=f32)


def _bf16_dot_bwd(dims, res, ct):
    a, b, ct = (v.astype(bf16) for v in (*res, ct))
    dot = lambda u, v, d: lax.dot_general(u, v, d, preferred_element_type=f32)
    if dims == NT:
        return dot(ct, b, NN), dot(ct, a, TN)
    return dot(b, ct, NT), dot(a, ct, NN)


_bf16_dot.defvjp(lambda a, b, dims: (_bf16_dot(a, b, dims), (a, b)), _bf16_dot_bwd)


def _hg_sub(state_t, hq, hf, hi, lb):
    n = HG_SUB
    f = lb + (1.0 - lb) * _sigmoid(hf)
    q = _silu(hq)
    k = 1.0 - f
    lf = jnp.log(f)
    cum = _cumsum_rows(lf)
    o = _bf16_dot(q * jnp.exp(cum), state_t, NT)
    causal = lax.broadcasted_iota(jnp.int32, (n, n, 1), 1) >= lax.broadcasted_iota(jnp.int32, (n, n, 1), 0)
    dec = jnp.exp(jnp.where(causal, cum[None] - cum[:, None, :], 0.0))
    score = jnp.sum(jnp.where(causal, q[None] * k[:, None, :] * dec, 0.0), axis=2, keepdims=True)
    o = o + jnp.sum(score * hi[:, None, :], axis=0)
    last = jnp.sum(lf, axis=0, keepdims=True)
    k_dec = k * jnp.exp(last - cum)
    new_state = state_t * jnp.exp(last) + _bf16_dot(hi, k_dec, TN)
    return new_state, o


def _hg_chunk(state_t, hq, hf, hi, hg, l0, l1, ng):
    lb = _sigmoid(l0 - l1)
    outs = []
    for j in range(len(hq)):
        state_t, o = _hg_sub(state_t, hq[j], hf[j], hi[j], lb)
        outs.append(_rms(o, ng) * _silu(hg[j]))
    return state_t, outs


def _hg_rows(c, j):
    return pl.ds(pl.multiple_of(c * HG_CHUNK + j * HG_SUB, HG_SUB), HG_SUB)


def _hg_pieces(ref, c, hd):
    return [ref[_hg_rows(c, j), _head_lanes(hd)] for j in range(HG_CHUNK // HG_SUB)]


def _hg_specs(s):
    lrow = pl.BlockSpec((1, HEADS_PER_STEP * HEAD_DIM), lambda h: (0, h))
    ng = pl.BlockSpec((1, HEAD_DIM), lambda h: (0, 0))
    return _head_cols(s), lrow, ng


def _hg_fwd(proj, logits, ng, comm=None):
    s = proj.shape[0]
    nch = s // HG_CHUNK
    n = HG_CHUNK // HG_SUB

    def body(hq_ref, hf_ref, hi_ref, hg_ref, l0_ref, l1_ref, ng_ref, o_ref, st_ref):
        def chunk(c, states):
            new = []
            for hd, state_t in enumerate(states):
                st_ref[hd, c] = state_t
                state_t, outs = _hg_chunk(
                    state_t, _hg_pieces(hq_ref, c, hd), _hg_pieces(hf_ref, c, hd), _hg_pieces(hi_ref, c, hd),
                    _hg_pieces(hg_ref, c, hd), l0_ref[:, _head_lanes(hd)], l1_ref[:, _head_lanes(hd)], ng_ref[...])
                for j in range(n):
                    o_ref[_hg_rows(c, j), _head_lanes(hd)] = outs[j]
                new.append(state_t)
            return tuple(new)

        lax.fori_loop(0, nch, chunk, (jnp.zeros((HEAD_DIM, HEAD_DIM), f32),) * HEADS_PER_STEP)

    col, lrow, ngs = _hg_specs(s)
    (o, states), sent = _grid_call(
        body, "hg_fwd", HEADS // HEADS_PER_STEP,
        [jax.ShapeDtypeStruct((s, HEADS * HEAD_DIM), f32), jax.ShapeDtypeStruct((HEADS, nch, HEAD_DIM, HEAD_DIM), f32)],
        [col(3 * HEADS), col(4 * HEADS), col(5 * HEADS), col(6 * HEADS), lrow, lrow, ngs],
        [col(0), pl.BlockSpec((HEADS_PER_STEP, nch, HEAD_DIM, HEAD_DIM), lambda h: (h, 0, 0, 0))], [],
        (proj, proj, proj, proj, logits[0:1], logits[1:2], ng), comm)
    return o, states, sent


def _hg_bwd(proj, logits, ng, states, do, do_off, comm=None):
    s = proj.shape[0]
    nch = s // HG_CHUNK
    n = HG_CHUNK // HG_SUB

    def body(hq_ref, hf_ref, hi_ref, hg_ref, l0_ref, l1_ref, ng_ref, st_ref, do_ref,
             dq_ref, df_ref, di_ref, dg_ref, dl0_ref, dl1_ref, dng_ref):
        dl0_ref[...] = jnp.zeros_like(dl0_ref)
        dl1_ref[...] = jnp.zeros_like(dl1_ref)
        dng_ref[...] = jnp.zeros_like(dng_ref)

        def chunk(i, dstates):
            c = nch - 1 - i
            new = []
            for hd, dstate in enumerate(dstates):
                lanes = _head_lanes(hd)
                args = (st_ref[hd, c], _hg_pieces(hq_ref, c, hd), _hg_pieces(hf_ref, c, hd), _hg_pieces(hi_ref, c, hd),
                        _hg_pieces(hg_ref, c, hd), l0_ref[:, lanes], l1_ref[:, lanes], ng_ref[...])
                _, vjp = jax.vjp(_hg_chunk, *args)
                dst, dq, df, di, dg, dl0, dl1, dng = vjp((dstate, _hg_pieces(do_ref, c, hd)))
                for j in range(n):
                    rows = _hg_rows(c, j)
                    dq_ref[rows, lanes] = dq[j].astype(dq_ref.dtype)
                    df_ref[rows, lanes] = df[j].astype(df_ref.dtype)
                    di_ref[rows, lanes] = di[j].astype(di_ref.dtype)
                    dg_ref[rows, lanes] = dg[j].astype(dg_ref.dtype)
                dl0_ref[:, lanes] += dl0
                dl1_ref[:, lanes] += dl1
                dng_ref[:, lanes] += dng
                new.append(dst)
            return tuple(new)

        lax.fori_loop(0, nch, chunk, (jnp.zeros((HEAD_DIM, HEAD_DIM), f32),) * HEADS_PER_STEP)

    col, lrow, ngs = _hg_specs(s)
    big = jax.ShapeDtypeStruct((s, HEADS * HEAD_DIM), bf16)
    vec = jax.ShapeDtypeStruct((1, HEADS * HEAD_DIM), f32)
    (dq, df, di, dg, dl0, dl1, dng), sent = _grid_call(
        body, "hg_bwd", HEADS // HEADS_PER_STEP, [big, big, big, big, vec, vec, vec],
        [col(3 * HEADS), col(4 * HEADS), col(5 * HEADS), col(6 * HEADS), lrow, lrow, ngs,
         pl.BlockSpec((HEADS_PER_STEP, nch, HEAD_DIM, HEAD_DIM), lambda h: (h, 0, 0, 0)), col(do_off)],
        [col(0), col(0), col(0), col(0), lrow, lrow, lrow], [],
        (proj, proj, proj, proj, logits[0:1], logits[1:2], ng, states, do), comm)
    return (jnp.concatenate([dq, df, di, dg], axis=1), jnp.concatenate([dl0, dl1], axis=0),
            dng.reshape(HEADS, HEAD_DIM), sent)


def _shift_rows(x, k, down):
    if k == 0:
        return x
    s = x.shape[0]
    row = lax.broadcasted_iota(jnp.int32, x.shape, 0)
    if down:
        return jnp.where(row < k, 0.0, pltpu.roll(x, k, 0))
    return jnp.where(row >= s - k, 0.0, pltpu.roll(x, s - k, 0))


def _conv_fwd(proj, w, b):
    s, w2 = proj.shape
    width = w2 // 2
    nb = width // RG_BLOCK

    def body(x_ref, w_ref, b_ref, y_ref, yb_ref):
        x = x_ref[...]
        y = jnp.broadcast_to(b_ref[...], x.shape)
        for k in range(CONV_WIDTH):
            y = y + _shift_rows(x, k, True) * w_ref[pl.ds(k, 1), :]
        y_ref[...] = y
        yb_ref[...] = y.astype(bf16)

    blk = pl.BlockSpec((s, RG_BLOCK), lambda j: (0, j))
    return pl.pallas_call(
        body, name="conv_fwd", grid=(nb,),
        out_shape=[jax.ShapeDtypeStruct((s, width), f32), jax.ShapeDtypeStruct((s, width), bf16)],
        in_specs=[pl.BlockSpec((s, RG_BLOCK), lambda j: (0, nb + j)),
                  pl.BlockSpec((CONV_WIDTH, RG_BLOCK), lambda j: (0, j)), pl.BlockSpec((1, RG_BLOCK), lambda j: (0, j))],
        out_specs=[blk, blk],
        compiler_params=_params("parallel"),
    )(proj, w, b)


def _conv_bwd(proj, w, dy_a, dy_b):
    s, w2 = proj.shape
    width = w2 // 2
    nb = width // RG_BLOCK

    def body(x_ref, w_ref, dya_ref, dyb_ref, dx_ref, dw_ref, db_ref):
        x = x_ref[...]
        dy_v = dya_ref[...] + dyb_ref[...]
        dx = jnp.zeros_like(x)
        for k in range(CONV_WIDTH):
            dx = dx + _shift_rows(dy_v, k, False) * w_ref[pl.ds(k, 1), :]
            dw_ref[pl.ds(k, 1), :] = jnp.sum(dy_v * _shift_rows(x, k, True), axis=0, keepdims=True)
        dx_ref[...] = dx.astype(dx_ref.dtype)
        db_ref[...] = jnp.sum(dy_v, axis=0, keepdims=True)

    blk = pl.BlockSpec((s, RG_BLOCK), lambda j: (0, j))
    wblk = pl.BlockSpec((CONV_WIDTH, RG_BLOCK), lambda j: (0, j))
    bblk = pl.BlockSpec((1, RG_BLOCK), lambda j: (0, j))
    return pl.pallas_call(
        body, name="conv_bwd", grid=(nb,),
        out_shape=[jax.ShapeDtypeStruct((s, width), bf16), jax.ShapeDtypeStruct((CONV_WIDTH, width), f32),
                   jax.ShapeDtypeStruct((1, width), f32)],
        in_specs=[pl.BlockSpec((s, RG_BLOCK), lambda j: (0, nb + j)), wblk, blk, blk],
        out_specs=[blk, wblk, bblk],
        compiler_params=_params("parallel"),
    )(proj, w, dy_a, dy_b)


def _rg_rows(w_ref):
    return jnp.concatenate([w_ref[q] for q in range(N_CHIPS)], axis=0)


def _bd_fwd(xb, wa, wx):
    s, width = xb.shape
    nb = width // RG_BLOCK

    def body(x_ref, wa_ref, wx_ref, ra_ref, ix_ref):
        x = x_ref[...]
        ra_ref[...] = lax.dot_general(x, _rg_rows(wa_ref), NN, preferred_element_type=f32)
        ix_ref[...] = lax.dot_general(x, _rg_rows(wx_ref), NN, preferred_element_type=f32)

    blk = pl.BlockSpec((s, RG_BLOCK), lambda j: (0, j))
    wblk = pl.BlockSpec((N_CHIPS, None, RG_BLOCK // N_CHIPS, RG_BLOCK), lambda j: (0, j, 0, 0))
    out = jax.ShapeDtypeStruct((s, width), f32)
    return pl.pallas_call(
        body, name="bd_fwd", grid=(nb,), out_shape=[out, out],
        in_specs=[blk, wblk, wblk], out_specs=[blk, blk],
        compiler_params=_params("parallel"),
    )(xb, wa, wx)


def _bd_bwd(xb, wa, wx, dra, dix):
    s, width = xb.shape
    nb = width // RG_BLOCK
    rq = RG_BLOCK // N_CHIPS

    def body(x_ref, wa_ref, wx_ref, dra_ref, dix_ref, dx_ref, dwa_ref, dwx_ref):
        x = x_ref[...]
        dra_v = dra_ref[...]
        dix_v = dix_ref[...]
        dx_ref[...] = (lax.dot_general(dra_v, _rg_rows(wa_ref), NT, preferred_element_type=f32)
                       + lax.dot_general(dix_v, _rg_rows(wx_ref), NT, preferred_element_type=f32))
        dwa = lax.dot_general(x, dra_v, TN, preferred_element_type=f32)
        dwx = lax.dot_general(x, dix_v, TN, preferred_element_type=f32)
        for q in range(N_CHIPS):
            dwa_ref[q] = dwa[q * rq:(q + 1) * rq, :].astype(dwa_ref.dtype)
            dwx_ref[q] = dwx[q * rq:(q + 1) * rq, :].astype(dwx_ref.dtype)

    blk = pl.BlockSpec((s, RG_BLOCK), lambda j: (0, j))
    wblk = pl.BlockSpec((N_CHIPS, None, rq, RG_BLOCK), lambda j: (0, j, 0, 0))
    wout = jax.ShapeDtypeStruct((N_CHIPS, nb, rq, RG_BLOCK), bf16)
    return pl.pallas_call(
        body, name="bd_bwd", grid=(nb,),
        out_shape=[jax.ShapeDtypeStruct((s, width), f32), wout, wout],
        in_specs=[blk, wblk, wblk, blk, blk], out_specs=[blk, wblk, wblk],
        compiler_params=_params("parallel"),
    )(xb, wa, wx, dra, dix)


def _scan_tiles(a_ref, u_ref, out_ref, n_tiles, reverse):
    cb = a_ref.shape[1]
    row = lax.broadcasted_iota(jnp.int32, (SUBLANES, cb), 0)

    def tile(i, carry):
        t = (n_tiles - 1 - i) if reverse else i
        rows = pl.ds(pl.multiple_of(t * SUBLANES, SUBLANES), SUBLANES)
        a = a_ref[rows, :]
        b = u_ref[rows, :]
        for d in (1, 2, 4):
            if reverse:
                valid = row < SUBLANES - d
                shift = SUBLANES - d
            else:
                valid = row >= d
                shift = d
            a_n = jnp.where(valid, pltpu.roll(a, shift, 0), 1.0)
            b_n = jnp.where(valid, pltpu.roll(b, shift, 0), 0.0)
            b = a * b_n + b
            a = a * a_n
        out = a * carry + b
        out_ref[rows, :] = out
        edge = t * SUBLANES + (0 if reverse else SUBLANES - 1)
        return jnp.broadcast_to(out_ref[pl.ds(edge, 1), :], (SUBLANES, cb))

    lax.fori_loop(0, n_tiles, tile, jnp.zeros((SUBLANES, cb), f32))


def _scan_fwd(a, u):
    s, width = a.shape

    def body(a_ref, u_ref, h_ref):
        _scan_tiles(a_ref, u_ref, h_ref, s // SUBLANES, False)

    blk = pl.BlockSpec((s, RG_BLOCK), lambda j: (0, j))
    return pl.pallas_call(
        body, name="scan_fwd", grid=(width // RG_BLOCK,), out_shape=jax.ShapeDtypeStruct((s, width), f32),
        in_specs=[blk, blk], out_specs=blk, compiler_params=_params("parallel"),
    )(a, u)


def _scan_bwd(a, h, dh):
    s, width = a.shape

    def body(a_ref, h_ref, dh_ref, du_ref, da_ref, a_next):
        a_next[...] = _shift_rows(a_ref[...], 1, False)
        _scan_tiles(a_next, dh_ref, du_ref, s // SUBLANES, True)
        da_ref[...] = du_ref[...] * _shift_rows(h_ref[...], 1, True)

    blk = pl.BlockSpec((s, RG_BLOCK), lambda j: (0, j))
    out = jax.ShapeDtypeStruct((s, width), f32)
    return pl.pallas_call(
        body, name="scan_bwd", grid=(width // RG_BLOCK,), out_shape=[out, out],
        in_specs=[blk, blk, blk], out_specs=[blk, blk],
        scratch_shapes=[pltpu.VMEM((s, RG_BLOCK), f32)],
        compiler_params=_params("parallel"),
    )(a, h, dh)


EW_TILE_BYTES = 2 << 20


def _ew_rows(rows, width):
    limit = max(16, EW_TILE_BYTES // (4 * width))
    t = (min(limit, rows) // 16) * 16
    while t >= 16:
        if rows % t == 0:
            return t
        t -= 16
    return rows


def _as2d(a):
    return a.reshape(-1, a.shape[-1])


def _ew(fn, ins, out_dtypes, name):
    shape = ins[0].shape
    ins2 = [_as2d(a) for a in ins]
    rows, width = ins2[0].shape
    res = _rowwise(fn, ins2, [], [(width, dt) for dt in out_dtypes], name, _ew_rows(rows, width))
    return [r.reshape(shape) for r in res]


def _f_adamw(w, g, m, v):
    m = ADAM_B1 * m + (1.0 - ADAM_B1) * g
    v = ADAM_B2 * v + (1.0 - ADAM_B2) * (g * g)
    m_hat = m / (1.0 - ADAM_B1 ** ADAM_STEP)
    v_hat = v / (1.0 - ADAM_B2 ** ADAM_STEP)
    delta = -ADAM_LR * (m_hat / (jnp.sqrt(v_hat) + ADAM_EPS) + ADAM_WD * w)
    return delta, m, v


ANY = pl.BlockSpec(memory_space=pl.ANY)


def _place():
    x, y, c = lax.axis_index("x"), lax.axis_index("y"), lax.axis_index("c")
    return x, y, c, [(1 - x, y), (x, 1 - y), (1 - x, 1 - y)]


def _rcopy(src, dst, send_sem, recv_sem, dev):
    return pltpu.make_async_remote_copy(src_ref=src, dst_ref=dst, send_sem=send_sem, recv_sem=recv_sem,
                                        device_id=dev, device_id_type=MESH)


def _half(ref, h, kind, lead=0):
    ax = lead + (1 if kind == "rg" else 0)
    n = ref.shape[ax] // 2
    idx = [slice(None)] * ref.ndim
    idx[ax] = pl.ds(h * n, n)
    return ref.at[tuple(idx)]


def _gather_over_ici(kinds):
    def make(out, send_sems, recv_sems):
        x, y, c, chips = _place()
        return [_rcopy(mine, mine, send_sems.at[3 * e + j], recv_sems.at[3 * e + j], (*chip, c))
                for e in range(len(kinds)) for j, chip in enumerate(chips)
                for mine in [_half(out[e].at[2 * x + y], c, kinds[e])]]

    return make


def _gather_hand_on(slots, kinds):
    n = len(slots)

    def copies(out, send_sems, recv_sems, which):
        x, y, c, chips = _place()
        h = c if which == "landed" else 1 - c
        return [_rcopy(part, part, send_sems.at[3 * e + j], recv_sems.at[3 * e + j], (x, y, 1 - c))
                for e in range(n) for j, chip in enumerate(chips)
                for part in [_half(out[e].at[2 * chip[0] + chip[1]], h, kinds[e])]]

    def start(ins, out, send_sems, recv_sems):
        for cp in copies(out, send_sems, recv_sems, "landed"):
            cp.start()

    def finish(ins, out, send_sems, recv_sems):
        for cp in copies(out, send_sems, recv_sems, "handed"):
            cp.wait_recv()
        for cp in copies(out, send_sems, recv_sems, "landed"):
            cp.wait_send()

    return _Comm(slots, [jax.ShapeDtypeStruct(s.shape, s.dtype) for s in slots], {e: e for e in range(n)}, 3 * n,
                 start, finish)


def _exchange_halves(grads, kinds):
    n = len(grads)

    def copies(g, theirs, send_sems, recv_sems):
        x, y, c, _ = _place()
        return [_rcopy(_half(g[e], 1 - c, kinds[e], 1), theirs[e], send_sems.at[e], recv_sems.at[e], (x, y, 1 - c))
                for e in range(n)]

    def start(*refs):
        for cp in copies(*refs):
            cp.start()

    def finish(*refs):
        for cp in copies(*refs):
            cp.wait()

    def half_shape(s, kind):
        shp = list(s.shape)
        shp[2 if kind == "rg" else 1] //= 2
        return jax.ShapeDtypeStruct(tuple(shp), s.dtype)

    return _Comm(grads, [half_shape(s, k) for s, k in zip(grads, kinds)], {}, n, start, finish)


def _exchange_chips(parts):
    n = len(parts)

    def make(refs, send_sems, recv_sems):
        p, got = refs[:n], refs[n:]
        x, y, c, chips = _place()
        return [_rcopy(p[e].at[2 * chip[0] + chip[1]], got[e].at[j], send_sems.at[3 * e + j], recv_sems.at[3 * e + j],
                       (*chip, c))
                for e in range(n) for j, chip in enumerate(chips)]

    landing = [lax.empty((N_CHIPS - 1,) + s.shape[1:], s.dtype) for s in parts]
    return make, list(parts) + landing, 3 * n


def _share_halves(shards, kinds):
    n = len(shards)

    def copies(out, send_sems, recv_sems, h):
        x, y, c, _ = _place()
        return [_rcopy(part, part, send_sems.at[e], recv_sems.at[e], (x, y, 1 - c))
                for e in range(n) for part in [_half(out[e], c if h == "mine" else 1 - c, kinds[e])]]

    def start(ins, out, send_sems, recv_sems):
        for cp in copies(out, send_sems, recv_sems, "mine"):
            cp.start()

    def finish(ins, out, send_sems, recv_sems):
        for cp in copies(out, send_sems, recv_sems, "other"):
            cp.wait_recv()
        for cp in copies(out, send_sems, recv_sems, "mine"):
            cp.wait_send()

    return _Comm(shards, [jax.ShapeDtypeStruct(s.shape, s.dtype) for s in shards], {e: e for e in range(n)}, n,
                 start, finish)


def _tiles_call(fn, scalars, ins, outs, steps, name):
    n_in = len(ins)

    def body(s_ref, *refs):
        res = fn(*[r[...] for r in refs[:n_in]])
        for r, o in zip(res, refs[n_in:]):
            o[...] = r.astype(o.dtype)

    spec = lambda blk, idx: pl.BlockSpec(blk, lambda i, s, idx=idx: (idx(i, s), 0))
    return pl.pallas_call(
        body, name=name, out_shape=[o[0] for o in outs],
        grid_spec=pltpu.PrefetchScalarGridSpec(
            num_scalar_prefetch=1, grid=(steps,),
            in_specs=[spec(blk, idx) for _, blk, idx in ins], out_specs=[spec(blk, idx) for _, blk, idx in outs]),
        compiler_params=_params("arbitrary"),
    )(scalars, *[a for a, _, _ in ins])


def _groups(kind, shard_shape):
    if kind == "rg":
        return shard_shape[0], shard_shape[1], shard_shape[2]
    return 1, shard_shape[0], shard_shape[1]


def _half_tiles(kind, shard_shape):
    g, rg, width = _groups(kind, shard_shape)
    tr = _ew_rows(rg // 2, width)
    nt = (rg // 2) // tr
    in_whole = lambda i, c: (i // nt) * (rg // tr) + c * nt + i % nt
    return g, rg, width, tr, nt, in_whole


def _cast_into_slot(w, layer, kind, scalars):
    shard_shape = w.shape[1:]
    g, rg, width, tr, nt, _ = _half_tiles(kind, shard_shape)
    per_shard = g * rg // tr
    (buf,) = _tiles_call(
        lambda v: (v,), scalars, [(_as2d(w), (tr, width), lambda i, s: layer * per_shard + i)],
        [(jax.ShapeDtypeStruct((N_CHIPS * g * rg, width), bf16), (tr, width), lambda i, s: s[0] * per_shard + i)],
        per_shard, "cast_" + kind)
    return buf.reshape((N_CHIPS,) + shard_shape)


def _add_cores(grad, theirs, kind, scalars):
    g, rg, width, tr, nt, in_whole = _half_tiles(kind, grad.shape[1:])
    steps = N_CHIPS * g * nt
    (out,) = _tiles_call(
        lambda u, v: (u.astype(f32) + v.astype(f32),), scalars,
        [(_as2d(grad), (tr, width), lambda i, s: in_whole(i, s[1])), (_as2d(theirs), (tr, width), lambda i, s: i)],
        [(jax.ShapeDtypeStruct((steps * tr, width), bf16), (tr, width), lambda i, s: i)], steps, "rs_add_cores")
    return out.reshape(theirs.shape)


def _sum_chips(part, got, kind, scalars):
    shard_shape = list(part.shape[1:])
    shard_shape[1 if kind == "rg" else 0] *= 2
    g, rg, width, tr, nt, in_whole = _half_tiles(kind, shard_shape)
    steps = g * nt
    up = lambda t: t.astype(f32)
    g2 = _as2d(got)
    ins = [(_as2d(part), (tr, width), lambda i, s: s[0] * steps + i)]
    ins += [(g2, (tr, width), lambda i, s, j=j: j * steps + i) for j in range(N_CHIPS - 1)]
    (out,) = _tiles_call(
        lambda a, b, c, e: (((up(a) + up(b)) + up(c)) + up(e),), scalars, ins,
        [(jax.ShapeDtypeStruct((g * rg, width), f32), (tr, width), lambda i, s: in_whole(i, s[1]))], steps, "rs_add_chips")
    return out.reshape(shard_shape)


def _all_gather_rows(v, reduce):
    rows = v.shape[0]

    def body(v_ref, out_ref, *rest):
        if reduce:
            sum_ref, send_sems, recv_sems, local_sem = rest
        else:
            send_sems, recv_sems, local_sem = rest
        x, y, c, chips = _place()
        me, sibling = (x, y, c), (x, y, 1 - c)

        def blk(px, py, pc):
            return out_ref.at[pl.ds((4 * px + 2 * py + pc) * rows, rows), :]

        def copy(k, block, to, src=None):
            return _rcopy(blk(*block) if src is None else src, blk(*block), send_sems.at[k], recv_sems.at[k], to)

        mine = pltpu.make_async_copy(v_ref, blk(*me), local_sem)
        mine.start()
        first = [copy(0, me, sibling, src=v_ref)] + [copy(1 + j, me, (*chip, c), src=v_ref) for j, chip in enumerate(chips)]
        for cp in first:
            cp.start()
        passed = [copy(4 + j, (*chip, c), sibling) for j, chip in enumerate(chips)]
        for j, chip in enumerate(chips):
            copy(1 + j, (*chip, c), me).wait_recv()
            passed[j].start()
        copy(0, sibling, me).wait_recv()
        for j, chip in enumerate(chips):
            copy(4 + j, (*chip, 1 - c), me).wait_recv()
        for cp in first + passed:
            cp.wait_send()
        mine.wait()
        if reduce:
            acc = out_ref[pl.ds(0, rows), :]
            for d in range(1, 8):
                acc = acc + out_ref[pl.ds(d * rows, rows), :]
            sum_ref[...] = acc

    vm = pl.BlockSpec(memory_space=pltpu.VMEM)
    gathered = jax.ShapeDtypeStruct((8 * rows, LANES), f32)
    res = pl.pallas_call(
        body, name="all_reduce_small" if reduce else "all_gather_small",
        out_shape=[gathered, jax.ShapeDtypeStruct((rows, LANES), f32)] if reduce else [gathered],
        in_specs=[vm], out_specs=[vm, vm] if reduce else [vm],
        scratch_shapes=[pltpu.SemaphoreType.DMA((7,)), pltpu.SemaphoreType.DMA((7,)), pltpu.SemaphoreType.DMA],
    )(v)
    return res[1] if reduce else res[0]


def _pack(parts):
    flat = jnp.concatenate([p.reshape(-1).astype(f32) for p in parts])
    rows = -(-flat.shape[0] // (SUBLANES * LANES)) * SUBLANES
    return jnp.pad(flat, (0, rows * LANES - flat.shape[0])).reshape(rows, LANES)


def _unpack(packed, shapes):
    flat = packed.reshape(-1)
    out, off = [], 0
    for shp in shapes:
        size = 1
        for d in shp:
            size *= d
        out.append(flat[off:off + size].reshape(shp))
        off += size
    return out


WEIGHTS = ["mix_pre_g", "mix_post_g", "ffn_pre_g", "ffn_post_g", "ple_norm_g", "w_in_even", "w_out_even",
           "hg_lb_logits", "hg_norm_g", "w_in_odd", "conv_w", "conv_b", "rg_wa", "rg_ba", "rg_wx", "rg_bx",
           "rg_lambda", "w_out_odd", "w_gate_up", "w_down", "w_ple_up", "w_ple_gate"]
BIG = [("w_in_even", 0, "col"), ("w_out_even", 0, "row"), ("w_gate_up", 0, "col"), ("w_down", 0, "row"),
       ("w_ple_up", 0, "col"), ("w_ple_gate", 0, "row"), ("w_in_odd", 0, "col"), ("rg_wa", 0, "rg"),
       ("rg_wx", 0, "rg"), ("w_out_odd", 0, "row"), ("w_gate_up", 1, "col"), ("w_down", 1, "row"),
       ("w_ple_up", 1, "col"), ("w_ple_gate", 1, "row")]
GATHER_GROUPS = [("in0", [0]), ("out0", [1]), ("ffn0", [2, 3, 4, 5]), ("mix1", [6, 7, 8, 9]), ("ffn1", [10, 11, 12, 13])]
REDUCE_LAYER1, REDUCE_FFN0, REDUCE_MIX0 = [6, 7, 8, 9, 10, 11, 12, 13], [2, 3, 4, 5], [0, 1]
BIG_NAMES = ["w_in_even", "w_out_even", "w_gate_up", "w_down", "w_ple_up", "w_ple_gate", "w_in_odd", "rg_wa", "rg_wx",
             "w_out_odd"]
SMALL_SHARDED = ["conv_w", "conv_b", "rg_ba", "rg_bx", "rg_lambda"]
REPLICATED = ["mix_pre_g", "mix_post_g", "ffn_pre_g", "ffn_post_g", "ple_norm_g", "hg_lb_logits", "hg_norm_g"]
TR = 256
TR_WIDE = 128


def _ffn_ple_fwd(h, m, l, gains, w_gu, w_d, w_pu, w_pg, pb, next_pre):
    d = h.shape[1]
    ff = w_d.shape[0]
    hb, nf = _rowwise(_f_post_pre, [h, m], [gains["mix_post_g"][l], gains["ffn_pre_g"][l]], [(d, f32), (d, bf16)],
                      f"post_mix{l}", TR)
    gu = _mm(nf, w_gu, "nn", f32, f"ffn_up{l}", b_shards=True)
    (act,) = _rowwise(_f_swiglu, [(gu, ff, 0), (gu, ff, 1)], [], [(ff, bf16)], f"swiglu{l}", TR_WIDE)
    f = _mm(act, w_d, "nn", f32, f"ffn_down{l}")
    hc, hcb = _rowwise(_f_post, [hb, f], [gains["ffn_post_g"][l]], [(d, f32), (d, bf16)], f"post_ffn{l}", TR)
    t = _mm(hcb, w_pg, "nn", f32, f"ple_gate{l}")
    e = _mm(pb, w_pu, "nn", f32, f"ple_up{l}", b_shards=True)
    if next_pre is None:
        (hd,) = _rowwise(_f_ple, [hc, t, e], [gains["ple_norm_g"][l]], [(d, f32)], f"ple{l}", TR)
        n_next = None
    else:
        hd, n_next = _rowwise(_f_ple_pre, [hc, t, e], [gains["ple_norm_g"][l], next_pre], [(d, f32), (d, bf16)],
                              f"ple{l}", TR)
    saved = dict(h=h, m=m, hb=hb, nf=nf, gu=gu, act=act, f=f, hc=hc, hcb=hcb, t=t, e=e)
    return hd, n_next, saved


def _ffn_ple_bwd(sv, l, gains, w_gu, w_d, w_pg, pb, next_pre, dhd, dn_next, after=None):
    ff = w_d.shape[0]
    if next_pre is None:
        (dhc, dt, de), (dg_ple,) = _rowwise_vjp(_f_ple, [sv["hc"], sv["t"], sv["e"]], [gains["ple_norm_g"][l]], [dhd],
                                                [(0, f32), (1, bf16), (2, bf16)], f"ple_bwd{l}", TR)
        dg_next = None
    else:
        (dhc, dt, de), (dg_ple, dg_next) = _rowwise_vjp(
            _f_ple_pre, [sv["hc"], sv["t"], sv["e"]], [gains["ple_norm_g"][l], next_pre], [dhd, dn_next],
            [(0, f32), (1, bf16), (2, bf16)], f"ple_bwd{l}", TR)
    gw = {}
    gw["w_ple_gate"] = _mm(sv["hcb"], dt, "tn", bf16, f"ple_gate_wgrad{l}")
    gw["w_ple_up"] = _mm(pb, de, "tn", bf16, f"ple_up_wgrad{l}", out_shards=True)
    dhcb = _mm(dt, w_pg, "nt", f32, f"ple_gate_dgrad{l}", after=after)
    (dhb, df), (dg_fpost,) = _rowwise_vjp(_f_post, [sv["hb"], sv["f"]], [gains["ffn_post_g"][l]], [dhc, dhcb],
                                          [(0, f32), (1, bf16)], f"post_ffn_bwd{l}", TR)
    gw["w_down"] = _mm(sv["act"], df, "tn", bf16, f"ffn_down_wgrad{l}")
    dact = _mm(df, w_d, "nt", f32, f"ffn_down_dgrad{l}")
    (dgu,), _ = _rowwise_vjp(_f_swiglu, [(sv["gu"], ff, 0), (sv["gu"], ff, 1)], [], [dact], [(0, bf16), (1, bf16)],
                             f"swiglu_bwd{l}", TR_WIDE, concat=True)
    gw["w_gate_up"] = _mm(sv["nf"], dgu, "tn", bf16, f"ffn_up_wgrad{l}", out_shards=True)
    dnf = _mm(dgu, w_gu, "nt", f32, f"ffn_up_dgrad{l}", b_shards=True)
    (dh, dm), (dg_post, dg_fpre) = _rowwise_vjp(
        _f_post_pre, [sv["h"], sv["m"]], [gains["mix_post_g"][l], gains["ffn_pre_g"][l]], [dhb, dnf],
        [(0, f32), (1, bf16)], f"post_mix_bwd{l}", TR)
    gg = dict(ple_norm_g=dg_ple, ffn_post_g=dg_fpost, mix_post_g=dg_post, ffn_pre_g=dg_fpre)
    return dh, dm, gw, gg, dg_next


def kernel(x, p, mix_pre_g, mix_post_g, ffn_pre_g, ffn_post_g, ple_norm_g, w_in_even, w_out_even, hg_lb_logits, hg_norm_g, w_in_odd, conv_w, conv_b, rg_wa, rg_ba, rg_wx, rg_bx, rg_lambda, w_out_odd, w_gate_up, w_down, w_ple_up, w_ple_gate, loss_target, m_mix_pre_g, m_mix_post_g, m_ffn_pre_g, m_ffn_post_g, m_ple_norm_g, m_w_in_even, m_w_out_even, m_hg_lb_logits, m_hg_norm_g, m_w_in_odd, m_conv_w, m_conv_b, m_rg_wa, m_rg_ba, m_rg_wx, m_rg_bx, m_rg_lambda, m_w_out_odd, m_w_gate_up, m_w_down, m_w_ple_up, m_w_ple_gate, v_mix_pre_g, v_mix_post_g, v_ffn_pre_g, v_ffn_post_g, v_ple_norm_g, v_w_in_even, v_w_out_even, v_hg_lb_logits, v_hg_norm_g, v_w_in_odd, v_conv_w, v_conv_b, v_rg_wa, v_rg_ba, v_rg_wx, v_rg_bx, v_rg_lambda, v_w_out_odd, v_w_gate_up, v_w_down, v_w_ple_up, v_w_ple_gate):
    given = dict(locals())
    w = {n: given[n] for n in WEIGHTS}
    mom = {n: given["m_" + n] for n in WEIGHTS}
    var = {n: given["v_" + n] for n in WEIGHTS}
    xi, yi = lax.axis_index("x"), lax.axis_index("y")
    my_q = 2 * xi + yi
    h0 = x[0]
    s, d = h0.shape
    n_layers = p.shape[0]
    gains = {n: [w[n][l:l + 1] for l in range(n_layers)]
             for n in ("mix_pre_g", "mix_post_g", "ffn_pre_g", "ffn_post_g", "ple_norm_g")}

    kinds = [k for _, _, k in BIG]
    place = jnp.stack([my_q, lax.axis_index("c")]).astype(jnp.int32)
    full, in_flight, order = {}, {}, place
    for tag, group in GATHER_GROUPS:
        kinds_g = [kinds[i] for i in group]
        slots = [_cast_into_slot(w[BIG[i][0]], BIG[i][1], BIG[i][2], place) for i in group]
        send_sems, recv_sems, slots, order = _copies_start(_gather_over_ici(kinds_g), slots, 3 * len(group),
                                                           "gather_start_" + tag, order)
        in_flight[tag] = (group, kinds_g, send_sems, recv_sems, slots)
    gather_started = order

    def arrive(tag, after):
        group, kinds_g, send_sems, recv_sems, slots = in_flight[tag]
        slots = _copies_wait(_gather_over_ici(kinds_g), send_sems, recv_sems, slots, "gather_wait_" + tag, [after])
        slots = _comm_call(_gather_hand_on(slots, kinds_g), "gather_hand_on_" + tag)
        full.update({BIG[i][:2]: b for i, b in zip(group, slots)})

    def rows_full(name, l):
        g = full[(name, l)]
        return g.reshape(g.shape[0] * g.shape[1], g.shape[2])

    small_shapes = [w[n].shape[1:] for n in SMALL_SHARDED]
    small_rows = _all_gather_rows(_pack([w[n][0] for n in SMALL_SHARDED]), False)
    per_chip = [_unpack(small_rows.reshape(8, -1)[2 * q], small_shapes) for q in range(N_CHIPS)]
    conv_w_f, conv_b_f, rg_ba_f, rg_bx_f, rg_lam_f = [
        jnp.concatenate([per_chip[q][i] for q in range(N_CHIPS)], axis=-1) for i in range(len(SMALL_SHARDED))]
    lru = conv_b_f.shape[0]
    conv_b_f, rg_ba_f, rg_bx_f, rg_lam_f = [v.reshape(1, lru) for v in (conv_b_f, rg_ba_f, rg_bx_f, rg_lam_f)]
    pb = [_ew(lambda v: (v,), [p[l, 0]], [bf16], f"cast_p{l}")[0] for l in range(n_layers)]

    (n0,) = _rowwise(_f_pre, [h0], [gains["mix_pre_g"][0]], [(d, bf16)], "pre_mix0", TR)
    arrive("in0", n0)
    proj0 = _mm(n0, full[("w_in_even", 0)], "nn", f32, "in_even", b_shards=True, after=gather_started)
    a_out, _ = _sb_fwd(proj0)
    b_out, hg_states, _ = _hg_fwd(proj0, w["hg_lb_logits"], w["hg_norm_g"])
    arrive("out0", b_out)
    mix0 = jnp.concatenate([a_out, b_out], axis=1).astype(bf16)
    m0 = _mm(mix0, rows_full("w_out_even", 0), "nn", f32, "out_even")
    arrive("ffn0", m0)
    h1, n1, sv0 = _ffn_ple_fwd(h0, m0, 0, gains, full[("w_gate_up", 0)], rows_full("w_down", 0), full[("w_ple_up", 0)],
                               rows_full("w_ple_gate", 0), pb[0], gains["mix_pre_g"][1])

    arrive("mix1", n1)
    proj1 = _mm(n1, full[("w_in_odd", 0)], "nn", f32, "in_odd", b_shards=True)
    xc, xcb = _conv_fwd(proj1, conv_w_f, conv_b_f)
    ra, ix = _bd_fwd(xcb, full[("rg_wa", 0)], full[("rg_wx", 0)])
    gate_in = [ra, ix, xc], [rg_ba_f, rg_bx_f, rg_lam_f]
    a_rec, u_rec = _rowwise(_f_rg_gate, *gate_in, [(lru, f32), (lru, f32)], "rg_gate", TR_WIDE, row_ids=True)
    hs = _scan_fwd(a_rec, u_rec)
    out_in = [(proj1, lru, 0), hs]
    (yb,) = _rowwise(_f_rg_out, out_in, [], [(lru, bf16)], "rg_out", TR_WIDE)
    m1 = _mm(yb, rows_full("w_out_odd", 0), "nn", f32, "out_odd")
    arrive("ffn1", m1)
    h2, _, sv1 = _ffn_ple_fwd(h1, m1, 1, gains, full[("w_gate_up", 1)], rows_full("w_down", 1), full[("w_ple_up", 1)],
                              rows_full("w_ple_gate", 1), pb[1], None)
    dy, loss_local = _loss_head(h2, loss_target[0])
    loss = lax.psum(loss_local, ("x", "y", "c"))

    gbig, chip_part, got, reducing = {}, {}, {}, {}

    def core_sums(group, tag):
        gl = []
        for i in group:
            n_, l_, k_ = BIG[i]
            g_ = gbig[(n_, l_)]
            gl.append(g_.reshape(N_CHIPS, g_.shape[0] // N_CHIPS, g_.shape[1]) if k_ == "row" else g_)
        theirs = _comm_call(_exchange_halves(gl, [kinds[i] for i in group]), "rs_core_exchange_" + tag)
        chip_part.update({i: _add_cores(g_, t_, kinds[i], place) for i, g_, t_ in zip(group, gl, theirs)})
        make, arrays, n_sems = _exchange_chips([chip_part[i] for i in group])
        send_sems, recv_sems, arrays, started = _copies_start(make, arrays, n_sems, "rs_chip_start_" + tag, place)
        reducing[tag] = (group, make, send_sems, recv_sems, arrays)
        return started

    def chips_arrive(tag, after):
        group, make, send_sems, recv_sems, arrays = reducing[tag]
        arrays = _copies_wait(make, send_sems, recv_sems, arrays, "rs_chip_wait_" + tag, after)
        got.update(zip(group, arrays[len(group):]))

    def reduced_shards(group, tag):
        sums = [_sum_chips(chip_part[i], got[i], kinds[i], place) for i in group]
        whole = _comm_call(_share_halves(sums, [kinds[i] for i in group]), "rs_share_halves_" + tag)
        return {BIG[i][:2]: r for i, r in zip(group, whole)}

    dh1, dm1, gw, gg1, _ = _ffn_ple_bwd(sv1, 1, gains, full[("w_gate_up", 1)], rows_full("w_down", 1),
                                        rows_full("w_ple_gate", 1), pb[1], None, dy, None)
    gbig.update({(n, 1): g for n, g in gw.items()})
    gbig[("w_out_odd", 0)] = _mm(yb, dm1, "tn", bf16, "out_odd_wgrad")
    dyo = _mm(dm1, rows_full("w_out_odd", 0), "nt", f32, "out_odd_dgrad")
    (dgate, dhs), _ = _rowwise_vjp(_f_rg_out, out_in, [], [dyo], [(0, bf16), (1, f32)], "rg_out_bwd", TR_WIDE)
    du, da = _scan_bwd(a_rec, hs, dhs)
    (dra, dix, dxc_direct), (g_ba, g_bx, g_lam) = _rowwise_vjp(
        _f_rg_gate, *gate_in, [da, du], [(0, bf16), (1, bf16), (2, f32)], "rg_gate_bwd", TR_WIDE, row_ids=True)
    dxc_gates, gbig[("rg_wa", 0)], gbig[("rg_wx", 0)] = _bd_bwd(xcb, full[("rg_wa", 0)], full[("rg_wx", 0)], dra, dix)
    dxb, g_conv_w, g_conv_b = _conv_bwd(proj1, conv_w_f, dxc_direct, dxc_gates)
    dproj1 = jnp.concatenate([dgate, dxb], axis=1)
    gbig[("w_in_odd", 0)] = _mm(n1, dproj1, "tn", bf16, "in_odd_wgrad", out_shards=True)
    dn1 = _mm(dproj1, full[("w_in_odd", 0)], "nt", f32, "in_odd_dgrad", b_shards=True)
    layer1_started = core_sums(REDUCE_LAYER1, "layer1")

    dh0, dm0, gw, gg0, g_pre1 = _ffn_ple_bwd(sv0, 0, gains, full[("w_gate_up", 0)], rows_full("w_down", 0),
                                             rows_full("w_ple_gate", 0), pb[0], gains["mix_pre_g"][1], dh1, dn1,
                                             after=layer1_started)
    gbig.update({(n, 0): g for n, g in gw.items()})
    ffn0_started = core_sums(REDUCE_FFN0, "ffn0")
    gbig[("w_out_even", 0)] = _mm(mix0, dm0, "tn", bf16, "out_even_wgrad")
    dmix = _mm(dm0, rows_full("w_out_even", 0), "nt", f32, "out_even_dgrad", after=ffn0_started)
    (dq, dk, dv), _ = _sb_bwd(proj0, a_out, dmix)
    dhg, g_logits, g_ng, _ = _hg_bwd(proj0, w["hg_lb_logits"], w["hg_norm_g"], hg_states, dmix, HEADS)
    dproj0 = jnp.concatenate([dq, dk, dv, dhg], axis=1)
    gbig[("w_in_even", 0)] = _mm(n0, dproj0, "tn", bf16, "in_even_wgrad", out_shards=True)
    dn0 = _mm(dproj0, full[("w_in_even", 0)], "nt", f32, "in_even_dgrad", b_shards=True)
    (grad_x,), (g_pre0,) = _rowwise_vjp(_f_pre_id, [h0], [gains["mix_pre_g"][0]], [dh0, dn0], [(0, f32)], "pre_mix0_bwd", TR)
    mix0_started = core_sums(REDUCE_MIX0, "mix0")

    g_small = dict(
        mix_pre_g=jnp.concatenate([g_pre0, g_pre1], 0), mix_post_g=jnp.concatenate([gg0["mix_post_g"], gg1["mix_post_g"]], 0),
        ffn_pre_g=jnp.concatenate([gg0["ffn_pre_g"], gg1["ffn_pre_g"]], 0),
        ffn_post_g=jnp.concatenate([gg0["ffn_post_g"], gg1["ffn_post_g"]], 0),
        ple_norm_g=jnp.concatenate([gg0["ple_norm_g"], gg1["ple_norm_g"]], 0),
        hg_lb_logits=g_logits, hg_norm_g=jnp.sum(g_ng, axis=0, keepdims=True),
        conv_w=g_conv_w, conv_b=g_conv_b, rg_ba=g_ba, rg_bx=g_bx, rg_lambda=g_lam)
    order = REPLICATED + SMALL_SHARDED
    summed = dict(zip(order, _unpack(_all_reduce_rows(_pack([g_small[n] for n in order])),
                                     [g_small[n].shape for n in order])))
    grad_small = {n: summed[n].reshape(w[n].shape) for n in REPLICATED}
    for n in SMALL_SHARDED:
        width = w[n].shape[-1]
        if n in ("rg_ba", "rg_bx"):
            full_g = summed[n].reshape(1, -1, RG_BLOCK)
        else:
            full_g = summed[n].reshape((1,) + g_small[n].shape) if n == "conv_w" else summed[n]
        grad_small[n] = lax.dynamic_slice_in_dim(full_g, my_q * width, width, axis=full_g.ndim - 1).reshape(w[n].shape)

    grads = dict(grad_small)
    delta, new_m, new_v = {}, {}, {}

    def update(names, by_layer):
        for n in names:
            grads[n] = jnp.stack([by_layer[(n, l)] for l in range(w[n].shape[0])])
            delta[n], new_m[n], new_v[n] = _ew(_f_adamw, [w[n], grads[n], mom[n], var[n]], [f32, f32, f32], "adamw_" + n)

    chips_arrive("layer1", [mix0_started])
    chips_arrive("ffn0", [mix0_started])
    rest = [n for n in BIG_NAMES if all(BIG[i][0] != n for i in REDUCE_MIX0)]
    update(rest, {**reduced_shards(REDUCE_LAYER1, "layer1"), **reduced_shards(REDUCE_FFN0, "ffn0")})
    chips_arrive("mix0", [delta[n] for n in rest])
    update([n for n in BIG_NAMES if n not in rest], reduced_shards(REDUCE_MIX0, "mix0"))
    small = REPLICATED + SMALL_SHARDED
    packed = [_pack([src[n] for n in small]) for src in (w, grads, mom, var)]
    outs = _ew(_f_adamw, packed, [f32, f32, f32], "adamw_small")
    for dst, o in zip((delta, new_m, new_v), outs):
        dst.update(zip(small, _unpack(o, [w[n].shape for n in small])))
    return (loss, grad_x.reshape(x.shape), *[grads[n] for n in WEIGHTS], *[delta[n] for n in WEIGHTS],
            *[new_m[n] for n in WEIGHTS], *[new_v[n] for n in WEIGHTS])


def _all_reduce_rows(v):
    return _all_gather_rows(v, True)
```

```python
import functools

import jax
import jax.numpy as jnp
from jax import lax
from jax.experimental import pallas as pl
from jax.experimental.pallas import tpu as pltpu

f32 = jnp.float32
bf16 = jnp.bfloat16
HIGHEST = lax.Precision.HIGHEST
MESH = pl.DeviceIdType.MESH

LANES = 128
SUBLANES = 8
MXU_WIDTH = 256
VMEM_LIMIT_BYTES = 56 * 1024 * 1024

RMS_EPS = 1e-6
HEADS = 8
HEAD_DIM = 128
SB_BLOCK = 128
HG_SUB = 16
HG_CHUNK = 64
RG_BLOCK = 256
CONV_WIDTH = 4
RG_C = 8.0
ADAM_LR, ADAM_B1, ADAM_B2, ADAM_EPS, ADAM_WD, ADAM_STEP = 0.001, 0.9, 0.999, 1e-08, 0.01, 10
N_CHIPS = 4

NN = (((1,), (0,)), ((), ()))
NT = (((1,), (1,)), ((), ()))
TN = (((0,), (0,)), ((), ()))


def _params(*sem):
    return pltpu.CompilerParams(dimension_semantics=sem, vmem_limit_bytes=VMEM_LIMIT_BYTES)


class _Comm:
    def __init__(self, arrays, out_shapes, aliases, n_sems, start, finish):
        self.arrays, self.out_shapes, self.aliases, self.n_sems = list(arrays), list(out_shapes), dict(aliases), n_sems
        self.start, self.finish = start, finish


def _grid_call(body, name, steps, out_shape, in_specs, out_specs, scratch_shapes, args, comm=None):
    n_in, n_out, n_scr = len(in_specs), len(out_shape), len(scratch_shapes)
    if comm is None:
        res = pl.pallas_call(body, name=name, grid=(steps,), out_shape=list(out_shape), in_specs=list(in_specs),
                             out_specs=list(out_specs), scratch_shapes=list(scratch_shapes),
                             compiler_params=_params("parallel"))(*args)
        return list(res), []
    c_in, c_out = len(comm.arrays), len(comm.out_shapes)
    any_spec = pl.BlockSpec(memory_space=pl.ANY)

    def wrapped(*refs):
        ins, refs = refs[:n_in], refs[n_in:]
        cins, refs = refs[:c_in], refs[c_in:]
        outs, refs = refs[:n_out], refs[n_out:]
        couts, refs = refs[:c_out], refs[c_out:]
        scr, (send_sems, recv_sems) = refs[:n_scr], refs[n_scr:]

        @pl.when(pl.program_id(0) == 0)
        def _():
            comm.start(cins, couts, send_sems, recv_sems)

        body(*ins, *outs, *scr)

        @pl.when(pl.program_id(0) == steps - 1)
        def _():
            comm.finish(cins, couts, send_sems, recv_sems)

    res = pl.pallas_call(
        wrapped, name=name, grid=(steps,), out_shape=list(out_shape) + comm.out_shapes,
        in_specs=list(in_specs) + [any_spec] * c_in, out_specs=list(out_specs) + [any_spec] * c_out,
        scratch_shapes=list(scratch_shapes) + [pltpu.SemaphoreType.DMA((comm.n_sems,))] * 2,
        input_output_aliases={n_in + i: n_out + o for i, o in comm.aliases.items()},
        compiler_params=_params("arbitrary"),
    )(*args, *comm.arrays)
    return list(res[:n_out]), list(res[n_out:])


def _comm_call(comm, name):
    c_in, c_out = len(comm.arrays), len(comm.out_shapes)
    any_spec = pl.BlockSpec(memory_space=pl.ANY)

    def body(*refs):
        ins, outs, (send_sems, recv_sems) = refs[:c_in], refs[c_in:c_in + c_out], refs[c_in + c_out:]
        comm.start(ins, outs, send_sems, recv_sems)
        comm.finish(ins, outs, send_sems, recv_sems)

    return pl.pallas_call(
        body, name=name, out_shape=comm.out_shapes, in_specs=[any_spec] * c_in, out_specs=[any_spec] * c_out,
        input_output_aliases=comm.aliases, scratch_shapes=[pltpu.SemaphoreType.DMA((comm.n_sems,))] * 2,
    )(*comm.arrays)


def _copies_start(make_copies, arrays, n_sems, name, after):
    n = len(arrays)
    hbm = pl.BlockSpec(memory_space=pltpu.HBM)
    sem = pl.BlockSpec(memory_space=pltpu.SEMAPHORE)

    def body(*refs):
        thru = refs[n + 3:2 * n + 3]
        for cp in make_copies(thru, refs[n + 1], refs[n + 2]):
            cp.start()
        refs[2 * n + 3][...] = jnp.zeros_like(refs[2 * n + 3])

    res = pl.pallas_call(
        body, name=name,
        out_shape=[pltpu.SemaphoreType.DMA((n_sems,)), pltpu.SemaphoreType.DMA((n_sems,))]
        + [pltpu.HBM(a.shape, a.dtype) for a in arrays] + [jax.ShapeDtypeStruct((SUBLANES, LANES), f32)],
        in_specs=[hbm] * n + [pl.BlockSpec(memory_space=pl.ANY)],
        out_specs=[sem, sem] + [hbm] * n + [pl.BlockSpec(memory_space=pltpu.VMEM)],
        input_output_aliases={i: 2 + i for i in range(n)},
        compiler_params=pltpu.CompilerParams(has_side_effects=pltpu.SideEffectType.DATAFLOW_SIDE_EFFECTING),
    )(*[pltpu.with_memory_space_constraint(a, pltpu.HBM) for a in arrays], after)
    return res[0], res[1], list(res[2:2 + n]), res[2 + n]


def _copies_wait(make_copies, send_sems, recv_sems, arrays, name, after):
    n = len(arrays)
    hbm = pl.BlockSpec(memory_space=pltpu.HBM)
    sem = pl.BlockSpec(memory_space=pltpu.SEMAPHORE)

    def body(*refs):
        for cp in make_copies(refs[:n], refs[n], refs[n + 1]):
            cp.wait_send()
            cp.wait_recv()

    return list(pl.pallas_call(
        body, name=name, out_shape=[pltpu.HBM(a.shape, a.dtype) for a in arrays],
        in_specs=[hbm] * n + [sem, sem] + [pl.BlockSpec(memory_space=pl.ANY)] * len(after), out_specs=[hbm] * n,
        input_output_aliases={i: i for i in range(n)},
        compiler_params=pltpu.CompilerParams(has_side_effects=pltpu.SideEffectType.DATAFLOW_SIDE_EFFECTING),
    )(*arrays, send_sems, recv_sems, *after))


def _pick(dim, target, align=LANES):
    if dim <= target:
        return dim
    t = (target // align) * align
    while t >= align:
        if dim % t == 0:
            return t
        t -= align
    return dim


MM_FULL_K = 2816
MM_CHUNK_K = 1408
MM_VMEM_BUDGET = 44 * 1024 * 1024


def _mm_tiles(m, n_unit, k_unit, k_whole, out_bytes):
    tk = k_unit if (k_whole and k_unit <= MM_FULL_K) else _pick(k_unit, MM_CHUNK_K)
    tk = k_unit if tk < MXU_WIDTH else tk
    tm = m if m <= 2048 else _pick(m, 512)
    tn = _pick(n_unit, 2048 if (m > 2048 or tk < k_unit or not k_whole) else 1024)
    tn = n_unit if tn < MXU_WIDTH else tn
    need = lambda: 2 * (2 * tm * tk + 2 * tk * tn + out_bytes * tm * tn)
    while need() > MM_VMEM_BUDGET:
        if tn > 512 and (tn // 2) % LANES == 0 and n_unit % (tn // 2) == 0:
            tn //= 2
        elif tm > 256:
            tm //= 2
        else:
            break
    return tm, tn, tk


def _mm(a, b, mode, out_dtype, name, *, b_shards=False, out_shards=False, after=None):
    if mode == "nn":
        m, kc = a.shape
        n = b.shape[-1] * (N_CHIPS if b_shards else 1)
    elif mode == "nt":
        m, kc = a.shape
        n = b.shape[-2]
    else:
        kc, m = a.shape
        n = b.shape[-1]
    n_sh = n // N_CHIPS
    kc_sh = kc // N_CHIPS
    n_unit = n_sh if (out_shards or (b_shards and mode == "nn")) else n
    k_split = b_shards and mode == "nt"
    tm, tn, tk = _mm_tiles(m, n_unit, kc_sh if k_split else kc, not k_split, jnp.dtype(out_dtype).itemsize)
    nk = kc // tk
    assert nk == 1 or out_dtype == f32, "a chunked contraction accumulates in the f32 output block"
    dims = {"nn": NN, "nt": NT, "tn": TN}[mode]

    def body(a_ref, b_ref, *rest):
        o_ref = rest[-1]
        part = lax.dot_general(a_ref[...], b_ref[...], dims, preferred_element_type=f32)
        if nk == 1:
            o_ref[...] = part.astype(o_ref.dtype)
        else:
            k = pl.program_id(2)

            @pl.when(k == 0)
            def _():
                o_ref[...] = part

            @pl.when(k > 0)
            def _():
                o_ref[...] += part

    if mode == "tn":
        a_spec = pl.BlockSpec((tk, tm), lambda i, j, k: (k, i))
    else:
        a_spec = pl.BlockSpec((tm, tk), lambda i, j, k: (i, k))
    nb_sh = n_sh // tn
    if mode == "nn":
        if b_shards:
            b_spec = pl.BlockSpec((None, tk, tn), lambda i, j, k: (j // nb_sh, k, j % nb_sh))
        else:
            b_spec = pl.BlockSpec((tk, tn), lambda i, j, k: (k, j))
    elif mode == "nt":
        if b_shards:
            kb_sh = kc_sh // tk
            b_spec = pl.BlockSpec((None, tn, tk), lambda i, j, k: (k // kb_sh, j, k % kb_sh))
        else:
            b_spec = pl.BlockSpec((tn, tk), lambda i, j, k: (j, k))
    else:
        b_spec = pl.BlockSpec((tk, tn), lambda i, j, k: (k, j))
    if out_shards:
        out_shape = jax.ShapeDtypeStruct((N_CHIPS, m, n_sh), out_dtype)
        o_spec = pl.BlockSpec((None, tm, tn), lambda i, j, k: (j // nb_sh, i, j % nb_sh))
    else:
        out_shape = jax.ShapeDtypeStruct((m, n), out_dtype)
        o_spec = pl.BlockSpec((tm, tn), lambda i, j, k: (i, j))
    order = [] if after is None else [after]
    return pl.pallas_call(
        body, name=name, out_shape=out_shape, grid=(m // tm, n // tn, nk),
        in_specs=[a_spec, b_spec] + [pl.BlockSpec(memory_space=pl.ANY)] * len(order), out_specs=o_spec,
        compiler_params=_params("parallel", "parallel", "arbitrary"),
    )(a, b, *order)


def _row_spec(entry, tr):
    if isinstance(entry, tuple):
        arr, width, cb = entry[:3]
        rb = (entry[3] // tr) if len(entry) > 3 else 0
        return arr, pl.BlockSpec((tr, width), lambda i, cb=cb, rb=rb: (i + rb, cb))
    return entry, pl.BlockSpec((tr, entry.shape[1]), lambda i: (i, 0))


def _whole_spec(arr):
    return pl.BlockSpec(arr.shape, lambda i: (0,) * arr.ndim)


def _rowwise(fn, rows, bcast, outs, name, tr, row_ids=False, m=None):
    arrs, specs = zip(*[_row_spec(e, tr) for e in rows])
    m = arrs[0].shape[0] if m is None else m
    nr, nb = len(rows), len(bcast)

    def body(*refs):
        vals = [r[...] for r in refs[:nr + nb]]
        if row_ids:
            rid = pl.program_id(0) * tr + lax.broadcasted_iota(jnp.int32, (tr, 1), 0)
            res = fn(rid, *vals)
        else:
            res = fn(*vals)
        for r, o in zip(res, refs[nr + nb:]):
            o[...] = r.astype(o.dtype)

    return pl.pallas_call(
        body, name=name, grid=(m // tr,),
        out_shape=[jax.ShapeDtypeStruct((m, w), dt) for w, dt in outs],
        in_specs=list(specs) + [_whole_spec(b) for b in bcast],
        out_specs=[pl.BlockSpec((tr, w), lambda i: (i, 0)) for w, _ in outs],
        compiler_params=_params("parallel"),
    )(*arrs, *bcast)


def _rowwise_vjp(fn, rows, bcast, cts, row_grads, name, tr, row_ids=False, concat=False):
    arrs, specs = zip(*[_row_spec(e, tr) for e in rows])
    ct_arrs, ct_specs = zip(*[_row_spec(e, tr) for e in cts])
    m = arrs[0].shape[0]
    nr, nb, nc, ng = len(rows), len(bcast), len(cts), len(row_grads)
    widths = [specs[idx].block_shape[1] for idx, _ in row_grads]

    def body(*refs):
        vals = [r[...] for r in refs[:nr + nb]]
        ctv = [r[...] for r in refs[nr + nb:nr + nb + nc]]
        out_refs = refs[nr + nb + nc:]
        if row_ids:
            rid = pl.program_id(0) * tr + lax.broadcasted_iota(jnp.int32, (tr, 1), 0)
            f = functools.partial(fn, rid)
        else:
            f = fn
        res, vjp = jax.vjp(f, *vals)
        grads = vjp(tuple(c.astype(r.dtype) for c, r in zip(ctv, res)))
        if concat:
            off = 0
            for (idx, _), w in zip(row_grads, widths):
                out_refs[0][:, off:off + w] = grads[idx].astype(out_refs[0].dtype)
                off += w
            b_refs = out_refs[1:]
        else:
            for (idx, _), o in zip(row_grads, out_refs[:ng]):
                o[...] = grads[idx].astype(o.dtype)
            b_refs = out_refs[ng:]

        @pl.when(pl.program_id(0) == 0)
        def _():
            for o in b_refs:
                o[...] = jnp.zeros_like(o)

        for j, o in enumerate(b_refs):
            o[...] += grads[nr + j].astype(f32)

    if concat:
        dt = row_grads[0][1]
        g_shapes = [jax.ShapeDtypeStruct((m, sum(widths)), dt)]
        g_specs = [pl.BlockSpec((tr, sum(widths)), lambda i: (i, 0))]
    else:
        g_shapes = [jax.ShapeDtypeStruct((m, w), dt) for (_, dt), w in zip(row_grads, widths)]
        g_specs = [pl.BlockSpec((tr, w), lambda i: (i, 0)) for w in widths]
    res = pl.pallas_call(
        body, name=name, grid=(m // tr,),
        out_shape=g_shapes + [jax.ShapeDtypeStruct(b.shape, f32) for b in bcast],
        in_specs=list(specs) + [_whole_spec(b) for b in bcast] + list(ct_specs),
        out_specs=g_specs + [_whole_spec(b) for b in bcast],
        compiler_params=_params("arbitrary"),
    )(*arrs, *bcast, *ct_arrs)
    n_g = len(g_shapes)
    return list(res[:n_g]), list(res[n_g:])


def _rms(x, g):
    return x * lax.rsqrt(jnp.mean(x * x, axis=-1, keepdims=True) + RMS_EPS) * g


def _sigmoid(x):
    return 1.0 / (1.0 + jnp.exp(-x))


def _silu(x):
    return x * _sigmoid(x)


def _gelu_tanh(x):
    return 0.5 * x * (1.0 + jnp.tanh(0.7978845608028654 * (x + 0.044715 * (x * x * x))))


def _softplus(x):
    return jnp.maximum(x, 0.0) + jnp.log(1.0 + jnp.exp(-jnp.abs(x)))


def _expm1(x):
    small = jnp.abs(x) < 0.01
    series = x * (1.0 + x * (0.5 + x * (1.0 / 6.0 + x * (1.0 / 24.0))))
    return jnp.where(small, series, jnp.exp(jnp.where(small, 0.0, x)) - 1.0)


def _f_pre(h, g):
    return (_rms(h, g),)


def _f_pre_id(h, g):
    return h, _rms(h, g)


def _f_post_pre(h, m, g_post, g_pre):
    hb = h + _rms(m, g_post)
    return hb, _rms(hb, g_pre)


def _f_post(h, m, g_post):
    hc = h + _rms(m, g_post)
    return hc, hc


def _f_swiglu(g, u):
    return (_silu(g) * u,)


def _f_ple_pre(h, t, e, g_ple, g_pre):
    hd = h + _rms(_sigmoid(t) * e, g_ple)
    return hd, _rms(hd, g_pre)


def _f_ple(h, t, e, g_ple):
    return (h + _rms(_sigmoid(t) * e, g_ple),)


def _f_rg_gate(rid, ra, ix, xc, ba, bx, lam):
    r = _sigmoid(ra + ba)
    i = _sigmoid(ix + bx)
    log_a = -RG_C * r * _softplus(-lam)
    a = jnp.exp(log_a)
    mult = jnp.sqrt(-_expm1(2.0 * log_a))
    mult = jnp.where(rid == 0, 1.0, mult)
    return a, xc * i * mult


def _f_rg_out(gate, hs):
    return (_gelu_tanh(gate) * hs,)


def _loss_head(y, target, tr=256):
    m, d = y.shape

    def body(y_ref, t_ref, dy_ref, l_ref):
        err = y_ref[...] - t_ref[...]
        dy_ref[...] = err * (1.0 / d)

        @pl.when(pl.program_id(0) == 0)
        def _():
            l_ref[...] = jnp.zeros_like(l_ref)

        part = 0.5 * jnp.sum(jnp.mean(err * err, axis=-1, keepdims=True), axis=0, keepdims=True)
        l_ref[...] += jnp.broadcast_to(part, l_ref.shape)

    dy, l = pl.pallas_call(
        body, name="loss_head", grid=(m // tr,),
        out_shape=[jax.ShapeDtypeStruct((m, d), f32), jax.ShapeDtypeStruct((SUBLANES, LANES), f32)],
        in_specs=[pl.BlockSpec((tr, d), lambda i: (i, 0))] * 2,
        out_specs=[pl.BlockSpec((tr, d), lambda i: (i, 0)), pl.BlockSpec((SUBLANES, LANES), lambda i: (0, 0))],
        compiler_params=_params("arbitrary"),
    )(y, target)
    return dy, l[0, 0]


def _hilo_dot(a, u):
    hi = a.astype(bf16)
    lo = (a - hi.astype(f32)).astype(bf16)
    return (lax.dot_general(hi, u, NN, preferred_element_type=f32)
            + lax.dot_general(lo, u, NN, preferred_element_type=f32))


def _sb_scores(q_t, k_s, mask):
    z = lax.dot_general(q_t, k_s, NT, preferred_element_type=f32)
    sp = jnp.log(1.0 + jnp.exp(-jnp.abs(z)))
    a = -(jnp.maximum(z, 0.0) + sp)
    b = a + z
    if mask is not None:
        a = jnp.where(mask, a, 0.0)
    return a, b


SB_QROWS = 512
SB_DIAG = SB_QROWS // SB_BLOCK


def _sb_masks():
    row = lax.broadcasted_iota(jnp.int32, (SB_QROWS, SB_BLOCK), 0)
    col = lax.broadcasted_iota(jnp.int32, (SB_QROWS, SB_BLOCK), 1)
    tri = [d * SB_BLOCK + col < row for d in range(SB_DIAG)]
    r = lax.broadcasted_iota(jnp.int32, (SB_BLOCK, SB_BLOCK), 0)
    c = lax.broadcasted_iota(jnp.int32, (SB_BLOCK, SB_BLOCK), 1)
    u_excl = (r > c).astype(bf16)
    u_incl = (r >= c).astype(bf16)
    return tri, u_excl, u_incl


def _sb_walk(block, t, init):
    carry = init
    for d in reversed(range(SB_DIAG)):
        carry = block(t * SB_DIAG + d, carry, d)
    return lax.fori_loop(0, t * SB_DIAG, lambda i, c: block(t * SB_DIAG - 1 - i, c, None), carry)


HEADS_PER_STEP = 2


def _head_lanes(hd):
    return slice(hd * HEAD_DIM, (hd + 1) * HEAD_DIM)


def _head_cols(s):
    width = HEADS_PER_STEP * HEAD_DIM
    return lambda off: pl.BlockSpec((s, width), lambda h, off=off: (0, off // HEADS_PER_STEP + h))


def _sb_fwd(proj, comm=None):
    s = proj.shape[0]
    nblk = s // SB_BLOCK
    scale = HEAD_DIM ** -0.5

    def body(q_ref, k_ref, v_ref, o_ref, qs, ks, vs):
        qs[...] = (q_ref[...] * scale).astype(bf16)
        ks[...] = k_ref[...].astype(bf16)
        vs[...] = v_ref[...].astype(bf16)
        tri, u_excl, _ = _sb_masks()

        def block(q_t, s_idx, carry, d):
            rows = pl.ds(pl.multiple_of(s_idx * SB_BLOCK, SB_BLOCK), SB_BLOCK)
            mask = None if d is None else tri[d]
            out = []
            for hd, (ra, acc) in enumerate(carry):
                a, b = _sb_scores(q_t[hd], ks[rows, _head_lanes(hd)], mask)
                w = jnp.exp(b + _hilo_dot(a, u_excl) + ra)
                if mask is not None:
                    w = jnp.where(mask, w, 0.0)
                acc = acc + lax.dot_general(w.astype(bf16), vs[rows, _head_lanes(hd)], NN, preferred_element_type=f32)
                out.append((ra + jnp.sum(a, axis=1, keepdims=True), acc))
            return tuple(out)

        def query_tile(t, _):
            rows = pl.ds(pl.multiple_of(t * SB_QROWS, SB_QROWS), SB_QROWS)
            q_t = [qs[rows, _head_lanes(hd)] for hd in range(HEADS_PER_STEP)]
            init = ((jnp.zeros((SB_QROWS, 1), f32), jnp.zeros((SB_QROWS, HEAD_DIM), f32)),) * HEADS_PER_STEP
            carry = _sb_walk(functools.partial(block, q_t), t, init)
            for hd in range(HEADS_PER_STEP):
                o_ref[rows, _head_lanes(hd)] = carry[hd][1]
            return 0

        lax.fori_loop(0, s // SB_QROWS, query_tile, 0)

    col = _head_cols(s)
    (o,), sent = _grid_call(
        body, "sb_fwd", HEADS // HEADS_PER_STEP, [jax.ShapeDtypeStruct((s, HEADS * HEAD_DIM), f32)],
        [col(0), col(HEADS), col(2 * HEADS)], [col(0)], [pltpu.VMEM((s, HEADS_PER_STEP * HEAD_DIM), bf16)] * 3,
        (proj, proj, proj), comm)
    return o, sent


def _sb_bwd(proj, o, do, comm=None):
    s = proj.shape[0]
    nblk = s // SB_BLOCK
    scale = HEAD_DIM ** -0.5

    def body(q_ref, k_ref, v_ref, o_ref, do_ref, dq_ref, dk_ref, dv_ref, qs, ks, vs, dos, dk_acc, dv_acc):
        qs[...] = (q_ref[...] * scale).astype(bf16)
        ks[...] = k_ref[...].astype(bf16)
        vs[...] = v_ref[...].astype(bf16)
        dos[...] = do_ref[...].astype(bf16)
        dk_acc[...] = jnp.zeros_like(dk_acc)
        dv_acc[...] = jnp.zeros_like(dv_acc)
        tri, u_excl, u_incl = _sb_masks()

        def block(q_t, do_t, d_t, s_idx, carry, d):
            rows = pl.ds(pl.multiple_of(s_idx * SB_BLOCK, SB_BLOCK), SB_BLOCK)
            mask = None if d is None else tri[d]
            out = []
            for hd, (ra, rg, dq) in enumerate(carry):
                lanes = _head_lanes(hd)
                k_s = ks[rows, lanes]
                v_s = vs[rows, lanes]
                a, b = _sb_scores(q_t[hd], k_s, mask)
                w = jnp.exp(b + _hilo_dot(a, u_excl) + ra)
                if mask is not None:
                    w = jnp.where(mask, w, 0.0)
                dw = lax.dot_general(do_t[hd], v_s, NT, preferred_element_type=f32)
                wb = w.astype(bf16)
                g = wb.astype(f32) * dw
                da = d_t[hd] - (_hilo_dot(g, u_incl) + rg)
                dz = g * jnp.exp(a) - da * jnp.exp(b)
                if mask is not None:
                    dz = jnp.where(mask, dz, 0.0)
                dzb = dz.astype(bf16)
                dq = dq + lax.dot_general(dzb, k_s, NN, preferred_element_type=f32)
                dk_acc[rows, lanes] += lax.dot_general(dzb, q_t[hd], TN, preferred_element_type=f32)
                dv_acc[rows, lanes] += lax.dot_general(wb, do_t[hd], TN, preferred_element_type=f32)
                out.append((ra + jnp.sum(a, axis=1, keepdims=True), rg + jnp.sum(g, axis=1, keepdims=True), dq))
            return tuple(out)

        def query_tile(t, _):
            rows = pl.ds(pl.multiple_of(t * SB_QROWS, SB_QROWS), SB_QROWS)
            heads = range(HEADS_PER_STEP)
            q_t = [qs[rows, _head_lanes(hd)] for hd in heads]
            do_t = [dos[rows, _head_lanes(hd)] for hd in heads]
            d_t = [jnp.sum(do_t[hd].astype(f32) * o_ref[rows, _head_lanes(hd)], axis=1, keepdims=True) for hd in heads]
            zero = jnp.zeros((SB_QROWS, 1), f32)
            init = ((zero, zero, jnp.zeros((SB_QROWS, HEAD_DIM), f32)),) * HEADS_PER_STEP
            carry = _sb_walk(functools.partial(block, q_t, do_t, d_t), t, init)
            for hd in heads:
                dq_ref[rows, _head_lanes(hd)] = (carry[hd][2] * scale).astype(dq_ref.dtype)
            return 0

        lax.fori_loop(0, s // SB_QROWS, query_tile, 0)
        dk_ref[...] = dk_acc[...].astype(dk_ref.dtype)
        dv_ref[...] = dv_acc[...].astype(dv_ref.dtype)

    col = _head_cols(s)
    out = jax.ShapeDtypeStruct((s, HEADS * HEAD_DIM), bf16)
    width = HEADS_PER_STEP * HEAD_DIM
    return _grid_call(
        body, "sb_bwd", HEADS // HEADS_PER_STEP, [out, out, out], [col(0), col(HEADS), col(2 * HEADS), col(0), col(0)],
        [col(0), col(0), col(0)], [pltpu.VMEM((s, width), bf16)] * 4 + [pltpu.VMEM((s, width), f32)] * 2,
        (proj, proj, proj, o, do), comm)


def _tri_dot(x, lower):
    n = x.shape[0]
    r = lax.broadcasted_iota(jnp.int32, (n, n), 0)
    c = lax.broadcasted_iota(jnp.int32, (n, n), 1)
    tri = ((c <= r) if lower else (c >= r)).astype(bf16)
    hi = x.astype(bf16)
    rest = x - hi.astype(f32)
    mid = rest.astype(bf16)
    lo = (rest - mid.astype(f32)).astype(bf16)
    return sum(lax.dot_general(tri, part, NN, preferred_element_type=f32) for part in (hi, mid, lo))


@jax.custom_vjp
def _cumsum_rows(x):
    return _tri_dot(x, True)


_cumsum_rows.defvjp(lambda x: (_tri_dot(x, True), None), lambda _, ct: (_tri_dot(ct, False),))


@functools.partial(jax.custom_vjp, nondiff_argnums=(2,))
def _bf16_dot(a, b, dims):
    return lax.dot_general(a.astype(bf16), b.astype(bf16), dims, preferred_element_type=f32)


def _bf16_dot_bwd(dims, res, ct):
    a, b, ct = (v.astype(bf16) for v in (*res, ct))
    dot = lambda u, v, d: lax.dot_general(u, v, d, preferred_element_type=f32)
    if dims == NT:
        return dot(ct, b, NN), dot(ct, a, TN)
    return dot(b, ct, NT), dot(a, ct, NN)


_bf16_dot.defvjp(lambda a, b, dims: (_bf16_dot(a, b, dims), (a, b)), _bf16_dot_bwd)


def _hg_sub(state_t, hq, hf, hi, lb):
    n = HG_SUB
    f = lb + (1.0 - lb) * _sigmoid(hf)
    q = _silu(hq)
    k = 1.0 - f
    lf = jnp.log(f)
    cum = _cumsum_rows(lf)
    o = _bf16_dot(q * jnp.exp(cum), state_t, NT)
    causal = lax.broadcasted_iota(jnp.int32, (n, n, 1), 1) >= lax.broadcasted_iota(jnp.int32, (n, n, 1), 0)
    dec = jnp.exp(jnp.where(causal, cum[None] - cum[:, None, :], 0.0))
    score = jnp.sum(jnp.where(causal, q[None] * k[:, None, :] * dec, 0.0), axis=2, keepdims=True)
    o = o + jnp.sum(score * hi[:, None, :], axis=0)
    last = jnp.sum(lf, axis=0, keepdims=True)
    k_dec = k * jnp.exp(last - cum)
    new_state = state_t * jnp.exp(last) + _bf16_dot(hi, k_dec, TN)
    return new_state, o


def _hg_chunk(state_t, hq, hf, hi, hg, l0, l1, ng):
    lb = _sigmoid(l0 - l1)
    outs = []
    for j in range(len(hq)):
        state_t, o = _hg_sub(state_t, hq[j], hf[j], hi[j], lb)
        outs.append(_rms(o, ng) * _silu(hg[j]))
    return state_t, outs


def _hg_rows(c, j):
    return pl.ds(pl.multiple_of(c * HG_CHUNK + j * HG_SUB, HG_SUB), HG_SUB)


def _hg_pieces(ref, c, hd):
    return [ref[_hg_rows(c, j), _head_lanes(hd)] for j in range(HG_CHUNK // HG_SUB)]


def _hg_specs(s):
    lrow = pl.BlockSpec((1, HEADS_PER_STEP * HEAD_DIM), lambda h: (0, h))
    ng = pl.BlockSpec((1, HEAD_DIM), lambda h: (0, 0))
    return _head_cols(s), lrow, ng


def _hg_fwd(proj, logits, ng, comm=None):
    s = proj.shape[0]
    nch = s // HG_CHUNK
    n = HG_CHUNK // HG_SUB

    def body(hq_ref, hf_ref, hi_ref, hg_ref, l0_ref, l1_ref, ng_ref, o_ref, st_ref):
        def chunk(c, states):
            new = []
            for hd, state_t in enumerate(states):
                st_ref[hd, c] = state_t
                state_t, outs = _hg_chunk(
                    state_t, _hg_pieces(hq_ref, c, hd), _hg_pieces(hf_ref, c, hd), _hg_pieces(hi_ref, c, hd),
                    _hg_pieces(hg_ref, c, hd), l0_ref[:, _head_lanes(hd)], l1_ref[:, _head_lanes(hd)], ng_ref[...])
                for j in range(n):
                    o_ref[_hg_rows(c, j), _head_lanes(hd)] = outs[j]
                new.append(state_t)
            return tuple(new)

        lax.fori_loop(0, nch, chunk, (jnp.zeros((HEAD_DIM, HEAD_DIM), f32),) * HEADS_PER_STEP)

    col, lrow, ngs = _hg_specs(s)
    (o, states), sent = _grid_call(
        body, "hg_fwd", HEADS // HEADS_PER_STEP,
        [jax.ShapeDtypeStruct((s, HEADS * HEAD_DIM), f32), jax.ShapeDtypeStruct((HEADS, nch, HEAD_DIM, HEAD_DIM), f32)],
        [col(3 * HEADS), col(4 * HEADS), col(5 * HEADS), col(6 * HEADS), lrow, lrow, ngs],
        [col(0), pl.BlockSpec((HEADS_PER_STEP, nch, HEAD_DIM, HEAD_DIM), lambda h: (h, 0, 0, 0))], [],
        (proj, proj, proj, proj, logits[0:1], logits[1:2], ng), comm)
    return o, states, sent


def _hg_bwd(proj, logits, ng, states, do, do_off, after=None):
    s = proj.shape[0]
    nch = s // HG_CHUNK
    n = HG_CHUNK // HG_SUB

    def body(hq_ref, hf_ref, hi_ref, hg_ref, l0_ref, l1_ref, ng_ref, st_ref, do_ref, *rest):
        dq_ref, df_ref, di_ref, dg_ref, dl0_ref, dl1_ref, dng_ref = rest[-7:]
        dl0_ref[...] = jnp.zeros_like(dl0_ref)
        dl1_ref[...] = jnp.zeros_like(dl1_ref)
        dng_ref[...] = jnp.zeros_like(dng_ref)

        def chunk(i, dstates):
            c = nch - 1 - i
            new = []
            for hd, dstate in enumerate(dstates):
                lanes = _head_lanes(hd)
                args = (st_ref[hd, c], _hg_pieces(hq_ref, c, hd), _hg_pieces(hf_ref, c, hd), _hg_pieces(hi_ref, c, hd),
                        _hg_pieces(hg_ref, c, hd), l0_ref[:, lanes], l1_ref[:, lanes], ng_ref[...])
                _, vjp = jax.vjp(_hg_chunk, *args)
                dst, dq, df, di, dg, dl0, dl1, dng = vjp((dstate, _hg_pieces(do_ref, c, hd)))
                for j in range(n):
                    rows = _hg_rows(c, j)
                    dq_ref[rows, lanes] = dq[j].astype(dq_ref.dtype)
                    df_ref[rows, lanes] = df[j].astype(df_ref.dtype)
                    di_ref[rows, lanes] = di[j].astype(di_ref.dtype)
                    dg_ref[rows, lanes] = dg[j].astype(dg_ref.dtype)
                dl0_ref[:, lanes] += dl0
                dl1_ref[:, lanes] += dl1
                dng_ref[:, lanes] += dng
                new.append(dst)
            return tuple(new)

        lax.fori_loop(0, nch, chunk, (jnp.zeros((HEAD_DIM, HEAD_DIM), f32),) * HEADS_PER_STEP)

    col, lrow, ngs = _hg_specs(s)
    big = jax.ShapeDtypeStruct((s, HEADS * HEAD_DIM), bf16)
    vec = jax.ShapeDtypeStruct((1, HEADS * HEAD_DIM), f32)
    order = [] if after is None else [after]
    (dq, df, di, dg, dl0, dl1, dng), sent = _grid_call(
        body, "hg_bwd", HEADS // HEADS_PER_STEP, [big, big, big, big, vec, vec, vec],
        [col(3 * HEADS), col(4 * HEADS), col(5 * HEADS), col(6 * HEADS), lrow, lrow, ngs,
         pl.BlockSpec((HEADS_PER_STEP, nch, HEAD_DIM, HEAD_DIM), lambda h: (h, 0, 0, 0)), col(do_off)]
        + [pl.BlockSpec(memory_space=pl.ANY)] * len(order),
        [col(0), col(0), col(0), col(0), lrow, lrow, lrow], [],
        (proj, proj, proj, proj, logits[0:1], logits[1:2], ng, states, do, *order))
    return (jnp.concatenate([dq, df, di, dg], axis=1), jnp.concatenate([dl0, dl1], axis=0),
            dng.reshape(HEADS, HEAD_DIM), sent)


def _shift_rows(x, k, down):
    if k == 0:
        return x
    s = x.shape[0]
    row = lax.broadcasted_iota(jnp.int32, x.shape, 0)
    if down:
        return jnp.where(row < k, 0.0, pltpu.roll(x, k, 0))
    return jnp.where(row >= s - k, 0.0, pltpu.roll(x, s - k, 0))


def _conv_fwd(proj, w, b):
    s, w2 = proj.shape
    width = w2 // 2
    nb = width // RG_BLOCK

    def body(x_ref, w_ref, b_ref, y_ref, yb_ref):
        x = x_ref[...]
        y = jnp.broadcast_to(b_ref[...], x.shape)
        for k in range(CONV_WIDTH):
            y = y + _shift_rows(x, k, True) * w_ref[pl.ds(k, 1), :]
        y_ref[...] = y
        yb_ref[...] = y.astype(bf16)

    blk = pl.BlockSpec((s, RG_BLOCK), lambda j: (0, j))
    return pl.pallas_call(
        body, name="conv_fwd", grid=(nb,),
        out_shape=[jax.ShapeDtypeStruct((s, width), f32), jax.ShapeDtypeStruct((s, width), bf16)],
        in_specs=[pl.BlockSpec((s, RG_BLOCK), lambda j: (0, nb + j)),
                  pl.BlockSpec((CONV_WIDTH, RG_BLOCK), lambda j: (0, j)), pl.BlockSpec((1, RG_BLOCK), lambda j: (0, j))],
        out_specs=[blk, blk],
        compiler_params=_params("parallel"),
    )(proj, w, b)


def _conv_bwd(proj, w, dy_a, dy_b):
    s, w2 = proj.shape
    width = w2 // 2
    nb = width // RG_BLOCK

    def body(x_ref, w_ref, dya_ref, dyb_ref, dx_ref, dw_ref, db_ref):
        x = x_ref[...]
        dy_v = dya_ref[...] + dyb_ref[...]
        dx = jnp.zeros_like(x)
        for k in range(CONV_WIDTH):
            dx = dx + _shift_rows(dy_v, k, False) * w_ref[pl.ds(k, 1), :]
            dw_ref[pl.ds(k, 1), :] = jnp.sum(dy_v * _shift_rows(x, k, True), axis=0, keepdims=True)
        dx_ref[...] = dx.astype(dx_ref.dtype)
        db_ref[...] = jnp.sum(dy_v, axis=0, keepdims=True)

    blk = pl.BlockSpec((s, RG_BLOCK), lambda j: (0, j))
    wblk = pl.BlockSpec((CONV_WIDTH, RG_BLOCK), lambda j: (0, j))
    bblk = pl.BlockSpec((1, RG_BLOCK), lambda j: (0, j))
    return pl.pallas_call(
        body, name="conv_bwd", grid=(nb,),
        out_shape=[jax.ShapeDtypeStruct((s, width), bf16), jax.ShapeDtypeStruct((CONV_WIDTH, width), f32),
                   jax.ShapeDtypeStruct((1, width), f32)],
        in_specs=[pl.BlockSpec((s, RG_BLOCK), lambda j: (0, nb + j)), wblk, blk, blk],
        out_specs=[blk, wblk, bblk],
        compiler_params=_params("parallel"),
    )(proj, w, dy_a, dy_b)


def _rg_rows(w_ref):
    return jnp.concatenate([w_ref[q] for q in range(N_CHIPS)], axis=0)


def _bd_fwd(xb, wa, wx):
    s, width = xb.shape
    nb = width // RG_BLOCK

    def body(x_ref, wa_ref, wx_ref, ra_ref, ix_ref):
        x = x_ref[...]
        ra_ref[...] = lax.dot_general(x, _rg_rows(wa_ref), NN, preferred_element_type=f32)
        ix_ref[...] = lax.dot_general(x, _rg_rows(wx_ref), NN, preferred_element_type=f32)

    blk = pl.BlockSpec((s, RG_BLOCK), lambda j: (0, j))
    wblk = pl.BlockSpec((N_CHIPS, None, RG_BLOCK // N_CHIPS, RG_BLOCK), lambda j: (0, j, 0, 0))
    out = jax.ShapeDtypeStruct((s, width), f32)
    return pl.pallas_call(
        body, name="bd_fwd", grid=(nb,), out_shape=[out, out],
        in_specs=[blk, wblk, wblk], out_specs=[blk, blk],
        compiler_params=_params("parallel"),
    )(xb, wa, wx)


def _bd_bwd(xb, wa, wx, dra, dix):
    s, width = xb.shape
    nb = width // RG_BLOCK
    rq = RG_BLOCK // N_CHIPS

    def body(x_ref, wa_ref, wx_ref, dra_ref, dix_ref, dx_ref, dwa_ref, dwx_ref):
        x = x_ref[...]
        dra_v = dra_ref[...]
        dix_v = dix_ref[...]
        dx_ref[...] = (lax.dot_general(dra_v, _rg_rows(wa_ref), NT, preferred_element_type=f32)
                       + lax.dot_general(dix_v, _rg_rows(wx_ref), NT, preferred_element_type=f32))
        dwa = lax.dot_general(x, dra_v, TN, preferred_element_type=f32)
        dwx = lax.dot_general(x, dix_v, TN, preferred_element_type=f32)
        for q in range(N_CHIPS):
            dwa_ref[q] = dwa[q * rq:(q + 1) * rq, :].astype(dwa_ref.dtype)
            dwx_ref[q] = dwx[q * rq:(q + 1) * rq, :].astype(dwx_ref.dtype)

    blk = pl.BlockSpec((s, RG_BLOCK), lambda j: (0, j))
    wblk = pl.BlockSpec((N_CHIPS, None, rq, RG_BLOCK), lambda j: (0, j, 0, 0))
    wout = jax.ShapeDtypeStruct((N_CHIPS, nb, rq, RG_BLOCK), bf16)
    return pl.pallas_call(
        body, name="bd_bwd", grid=(nb,),
        out_shape=[jax.ShapeDtypeStruct((s, width), f32), wout, wout],
        in_specs=[blk, wblk, wblk, blk, blk], out_specs=[blk, wblk, wblk],
        compiler_params=_params("parallel"),
    )(xb, wa, wx, dra, dix)


def _scan_tiles(a_ref, u_ref, out_ref, n_tiles, reverse):
    cb = a_ref.shape[1]
    row = lax.broadcasted_iota(jnp.int32, (SUBLANES, cb), 0)

    def tile(i, carry):
        t = (n_tiles - 1 - i) if reverse else i
        rows = pl.ds(pl.multiple_of(t * SUBLANES, SUBLANES), SUBLANES)
        a = a_ref[rows, :]
        b = u_ref[rows, :]
        for d in (1, 2, 4):
            if reverse:
                valid = row < SUBLANES - d
                shift = SUBLANES - d
            else:
                valid = row >= d
                shift = d
            a_n = jnp.where(valid, pltpu.roll(a, shift, 0), 1.0)
            b_n = jnp.where(valid, pltpu.roll(b, shift, 0), 0.0)
            b = a * b_n + b
            a = a * a_n
        out = a * carry + b
        out_ref[rows, :] = out
        edge = t * SUBLANES + (0 if reverse else SUBLANES - 1)
        return jnp.broadcast_to(out_ref[pl.ds(edge, 1), :], (SUBLANES, cb))

    lax.fori_loop(0, n_tiles, tile, jnp.zeros((SUBLANES, cb), f32))


def _scan_fwd(a, u):
    s, width = a.shape

    def body(a_ref, u_ref, h_ref):
        _scan_tiles(a_ref, u_ref, h_ref, s // SUBLANES, False)

    blk = pl.BlockSpec((s, RG_BLOCK), lambda j: (0, j))
    return pl.pallas_call(
        body, name="scan_fwd", grid=(width // RG_BLOCK,), out_shape=jax.ShapeDtypeStruct((s, width), f32),
        in_specs=[blk, blk], out_specs=blk, compiler_params=_params("parallel"),
    )(a, u)


def _scan_bwd(a, h, dh):
    s, width = a.shape

    def body(a_ref, h_ref, dh_ref, du_ref, da_ref, a_next):
        a_next[...] = _shift_rows(a_ref[...], 1, False)
        _scan_tiles(a_next, dh_ref, du_ref, s // SUBLANES, True)
        da_ref[...] = du_ref[...] * _shift_rows(h_ref[...], 1, True)

    blk = pl.BlockSpec((s, RG_BLOCK), lambda j: (0, j))
    out = jax.ShapeDtypeStruct((s, width), f32)
    return pl.pallas_call(
        body, name="scan_bwd", grid=(width // RG_BLOCK,), out_shape=[out, out],
        in_specs=[blk, blk, blk], out_specs=[blk, blk],
        scratch_shapes=[pltpu.VMEM((s, RG_BLOCK), f32)],
        compiler_params=_params("parallel"),
    )(a, h, dh)


EW_TILE_BYTES = 2 << 20


def _ew_rows(rows, width):
    limit = max(16, EW_TILE_BYTES // (4 * width))
    t = (min(limit, rows) // 16) * 16
    while t >= 16:
        if rows % t == 0:
            return t
        t -= 16
    return rows


def _as2d(a):
    return a.reshape(-1, a.shape[-1])


def _ew(fn, ins, out_dtypes, name):
    shape = ins[0].shape
    ins2 = [_as2d(a) for a in ins]
    rows, width = ins2[0].shape
    res = _rowwise(fn, ins2, [], [(width, dt) for dt in out_dtypes], name, _ew_rows(rows, width))
    return [r.reshape(shape) for r in res]


def _f_adamw(w, g, m, v):
    m = ADAM_B1 * m + (1.0 - ADAM_B1) * g
    v = ADAM_B2 * v + (1.0 - ADAM_B2) * (g * g)
    m_hat = m / (1.0 - ADAM_B1 ** ADAM_STEP)
    v_hat = v / (1.0 - ADAM_B2 ** ADAM_STEP)
    delta = -ADAM_LR * (m_hat / (jnp.sqrt(v_hat) + ADAM_EPS) + ADAM_WD * w)
    return delta, m, v


ANY = pl.BlockSpec(memory_space=pl.ANY)


def _place():
    x, y, c = lax.axis_index("x"), lax.axis_index("y"), lax.axis_index("c")
    return x, y, c, [(1 - x, y), (x, 1 - y), (1 - x, 1 - y)]


def _rcopy(src, dst, send_sem, recv_sem, dev):
    return pltpu.make_async_remote_copy(src_ref=src, dst_ref=dst, send_sem=send_sem, recv_sem=recv_sem,
                                        device_id=dev, device_id_type=MESH)


def _half(ref, h, kind, lead=0):
    ax = lead + (1 if kind == "rg" else 0)
    n = ref.shape[ax] // 2
    idx = [slice(None)] * ref.ndim
    idx[ax] = pl.ds(h * n, n)
    return ref.at[tuple(idx)]


def _gather_over_ici(kinds):
    def make(out, send_sems, recv_sems):
        x, y, c, chips = _place()
        return [_rcopy(mine, mine, send_sems.at[3 * e + j], recv_sems.at[3 * e + j], (*chip, c))
                for e in range(len(kinds)) for j, chip in enumerate(chips)
                for mine in [_half(out[e].at[2 * x + y], c, kinds[e])]]

    return make


def _gather_hand_on_copies(kinds):
    def make(out, send_sems, recv_sems):
        x, y, c, chips = _place()
        return [_rcopy(part, part, send_sems.at[3 * e + j], recv_sems.at[3 * e + j], (x, y, 1 - c))
                for e in range(len(kinds)) for j, chip in enumerate(chips)
                for part in [_half(out[e].at[2 * chip[0] + chip[1]], c, kinds[e])]]

    return make


def _gather_hand_on(slots, kinds):
    n = len(slots)

    def copies(out, send_sems, recv_sems, which):
        x, y, c, chips = _place()
        h = c if which == "landed" else 1 - c
        return [_rcopy(part, part, send_sems.at[3 * e + j], recv_sems.at[3 * e + j], (x, y, 1 - c))
                for e in range(n) for j, chip in enumerate(chips)
                for part in [_half(out[e].at[2 * chip[0] + chip[1]], h, kinds[e])]]

    def start(ins, out, send_sems, recv_sems):
        for cp in copies(out, send_sems, recv_sems, "landed"):
            cp.start()

    def finish(ins, out, send_sems, recv_sems):
        for cp in copies(out, send_sems, recv_sems, "handed"):
            cp.wait_recv()
        for cp in copies(out, send_sems, recv_sems, "landed"):
            cp.wait_send()

    return _Comm(slots, [jax.ShapeDtypeStruct(s.shape, s.dtype) for s in slots], {e: e for e in range(n)}, 3 * n,
                 start, finish)


def _exchange_halves(grads, kinds):
    n = len(grads)

    def copies(g, theirs, send_sems, recv_sems):
        x, y, c, _ = _place()
        return [_rcopy(_half(g[e], 1 - c, kinds[e], 1), theirs[e], send_sems.at[e], recv_sems.at[e], (x, y, 1 - c))
                for e in range(n)]

    def start(*refs):
        for cp in copies(*refs):
            cp.start()

    def finish(*refs):
        for cp in copies(*refs):
            cp.wait()

    return _Comm(grads, [jax.ShapeDtypeStruct(_half_shape(s, k), s.dtype) for s, k in zip(grads, kinds)], {}, n,
                 start, finish)


def _half_shape(s, kind):
    shp = list(s.shape)
    shp[2 if kind == "rg" else 1] //= 2
    return tuple(shp)


def _exchange_halves_copies(grads, kinds):
    n = len(grads)

    def make(refs, send_sems, recv_sems):
        g, theirs = refs[:n], refs[n:]
        x, y, c, _ = _place()
        return [_rcopy(_half(g[e], 1 - c, kinds[e], 1), theirs[e], send_sems.at[e], recv_sems.at[e], (x, y, 1 - c))
                for e in range(n)]

    landing = [lax.empty(_half_shape(s, k), s.dtype) for s, k in zip(grads, kinds)]
    return make, list(grads) + landing, n


def _exchange_chips(parts):
    n = len(parts)

    def make(refs, send_sems, recv_sems):
        p, got = refs[:n], refs[n:]
        x, y, c, chips = _place()
        return [_rcopy(p[e].at[2 * chip[0] + chip[1]], got[e].at[j], send_sems.at[3 * e + j], recv_sems.at[3 * e + j],
                       (*chip, c))
                for e in range(n) for j, chip in enumerate(chips)]

    landing = [lax.empty((N_CHIPS - 1,) + s.shape[1:], s.dtype) for s in parts]
    return make, list(parts) + landing, 3 * n


def _share_halves(shards, kinds):
    n = len(shards)

    def copies(out, send_sems, recv_sems, h):
        x, y, c, _ = _place()
        return [_rcopy(part, part, send_sems.at[e], recv_sems.at[e], (x, y, 1 - c))
                for e in range(n) for part in [_half(out[e], c if h == "mine" else 1 - c, kinds[e])]]

    def start(ins, out, send_sems, recv_sems):
        for cp in copies(out, send_sems, recv_sems, "mine"):
            cp.start()

    def finish(ins, out, send_sems, recv_sems):
        for cp in copies(out, send_sems, recv_sems, "other"):
            cp.wait_recv()
        for cp in copies(out, send_sems, recv_sems, "mine"):
            cp.wait_send()

    return _Comm(shards, [jax.ShapeDtypeStruct(s.shape, s.dtype) for s in shards], {e: e for e in range(n)}, n,
                 start, finish)


def _tiles_call(fn, scalars, ins, outs, steps, name):
    n_in = len(ins)

    def body(s_ref, *refs):
        res = fn(*[r[...] for r in refs[:n_in]])
        for r, o in zip(res, refs[n_in:]):
            o[...] = r.astype(o.dtype)

    spec = lambda blk, idx: pl.BlockSpec(blk, lambda i, s, idx=idx: (idx(i, s), 0))
    return pl.pallas_call(
        body, name=name, out_shape=[o[0] for o in outs],
        grid_spec=pltpu.PrefetchScalarGridSpec(
            num_scalar_prefetch=1, grid=(steps,),
            in_specs=[spec(blk, idx) for _, blk, idx in ins], out_specs=[spec(blk, idx) for _, blk, idx in outs]),
        compiler_params=_params("arbitrary"),
    )(scalars, *[a for a, _, _ in ins])


def _groups(kind, shard_shape):
    if kind == "rg":
        return shard_shape[0], shard_shape[1], shard_shape[2]
    return 1, shard_shape[0], shard_shape[1]


def _half_tiles(kind, shard_shape):
    g, rg, width = _groups(kind, shard_shape)
    tr = _ew_rows(rg // 2, width)
    nt = (rg // 2) // tr
    in_whole = lambda i, c: (i // nt) * (rg // tr) + c * nt + i % nt
    return g, rg, width, tr, nt, in_whole


def _cast_into_slot(w, layer, kind, scalars):
    shard_shape = w.shape[1:]
    g, rg, width, tr, nt, _ = _half_tiles(kind, shard_shape)
    per_shard = g * rg // tr
    (buf,) = _tiles_call(
        lambda v: (v,), scalars, [(_as2d(w), (tr, width), lambda i, s: layer * per_shard + i)],
        [(jax.ShapeDtypeStruct((N_CHIPS * g * rg, width), bf16), (tr, width), lambda i, s: s[0] * per_shard + i)],
        per_shard, "cast_" + kind)
    return buf.reshape((N_CHIPS,) + shard_shape)


def _add_cores(grad, theirs, kind, scalars):
    g, rg, width, tr, nt, in_whole = _half_tiles(kind, grad.shape[1:])
    steps = N_CHIPS * g * nt
    (out,) = _tiles_call(
        lambda u, v: (u.astype(f32) + v.astype(f32),), scalars,
        [(_as2d(grad), (tr, width), lambda i, s: in_whole(i, s[1])), (_as2d(theirs), (tr, width), lambda i, s: i)],
        [(jax.ShapeDtypeStruct((steps * tr, width), bf16), (tr, width), lambda i, s: i)], steps, "rs_add_cores")
    return out.reshape(theirs.shape)


def _sum_chips(part, got, kind, scalars):
    shard_shape = list(part.shape[1:])
    shard_shape[1 if kind == "rg" else 0] *= 2
    g, rg, width, tr, nt, in_whole = _half_tiles(kind, shard_shape)
    steps = g * nt
    up = lambda t: t.astype(f32)
    g2 = _as2d(got)
    ins = [(_as2d(part), (tr, width), lambda i, s: s[0] * steps + i)]
    ins += [(g2, (tr, width), lambda i, s, j=j: j * steps + i) for j in range(N_CHIPS - 1)]
    (out,) = _tiles_call(
        lambda a, b, c, e: (((up(a) + up(b)) + up(c)) + up(e),), scalars, ins,
        [(jax.ShapeDtypeStruct((g * rg, width), f32), (tr, width), lambda i, s: in_whole(i, s[1]))], steps, "rs_add_chips")
    return out.reshape(shard_shape)


def _all_gather_rows(v, reduce):
    rows = v.shape[0]

    def body(v_ref, out_ref, *rest):
        if reduce:
            sum_ref, send_sems, recv_sems, local_sem = rest
        else:
            send_sems, recv_sems, local_sem = rest
        x, y, c, chips = _place()
        me, sibling = (x, y, c), (x, y, 1 - c)

        def blk(px, py, pc):
            return out_ref.at[pl.ds((4 * px + 2 * py + pc) * rows, rows), :]

        def copy(k, block, to, src=None):
            return _rcopy(blk(*block) if src is None else src, blk(*block), send_sems.at[k], recv_sems.at[k], to)

        mine = pltpu.make_async_copy(v_ref, blk(*me), local_sem)
        mine.start()
        first = [copy(0, me, sibling, src=v_ref)] + [copy(1 + j, me, (*chip, c), src=v_ref) for j, chip in enumerate(chips)]
        for cp in first:
            cp.start()
        passed = [copy(4 + j, (*chip, c), sibling) for j, chip in enumerate(chips)]
        for j, chip in enumerate(chips):
            copy(1 + j, (*chip, c), me).wait_recv()
            passed[j].start()
        copy(0, sibling, me).wait_recv()
        for j, chip in enumerate(chips):
            copy(4 + j, (*chip, 1 - c), me).wait_recv()
        for cp in first + passed:
            cp.wait_send()
        mine.wait()
        if reduce:
            acc = out_ref[pl.ds(0, rows), :]
            for d in range(1, 8):
                acc = acc + out_ref[pl.ds(d * rows, rows), :]
            sum_ref[...] = acc

    vm = pl.BlockSpec(memory_space=pltpu.VMEM)
    gathered = jax.ShapeDtypeStruct((8 * rows, LANES), f32)
    res = pl.pallas_call(
        body, name="all_reduce_small" if reduce else "all_gather_small",
        out_shape=[gathered, jax.ShapeDtypeStruct((rows, LANES), f32)] if reduce else [gathered],
        in_specs=[vm], out_specs=[vm, vm] if reduce else [vm],
        scratch_shapes=[pltpu.SemaphoreType.DMA((7,)), pltpu.SemaphoreType.DMA((7,)), pltpu.SemaphoreType.DMA],
    )(v)
    return res[1] if reduce else res[0]


def _pack(parts):
    flat = jnp.concatenate([p.reshape(-1).astype(f32) for p in parts])
    rows = -(-flat.shape[0] // (SUBLANES * LANES)) * SUBLANES
    return jnp.pad(flat, (0, rows * LANES - flat.shape[0])).reshape(rows, LANES)


def _unpack(packed, shapes):
    flat = packed.reshape(-1)
    out, off = [], 0
    for shp in shapes:
        size = 1
        for d in shp:
            size *= d
        out.append(flat[off:off + size].reshape(shp))
        off += size
    return out


WEIGHTS = ["mix_pre_g", "mix_post_g", "ffn_pre_g", "ffn_post_g", "ple_norm_g", "w_in_even", "w_out_even",
           "hg_lb_logits", "hg_norm_g", "w_in_odd", "conv_w", "conv_b", "rg_wa", "rg_ba", "rg_wx", "rg_bx",
           "rg_lambda", "w_out_odd", "w_gate_up", "w_down", "w_ple_up", "w_ple_gate"]
BIG = [("w_in_even", 0, "col"), ("w_out_even", 0, "row"), ("w_gate_up", 0, "col"), ("w_down", 0, "row"),
       ("w_ple_up", 0, "col"), ("w_ple_gate", 0, "row"), ("w_in_odd", 0, "col"), ("rg_wa", 0, "rg"),
       ("rg_wx", 0, "rg"), ("w_out_odd", 0, "row"), ("w_gate_up", 1, "col"), ("w_down", 1, "row"),
       ("w_ple_up", 1, "col"), ("w_ple_gate", 1, "row")]
GATHER_GROUPS = [("in0", [0]), ("out0", [1]), ("ffn0", [2, 3, 4, 5]), ("mix1", [6, 7, 8, 9]), ("ffn1", [10, 11, 12, 13])]
REDUCE_LAYER1, REDUCE_FFN0, REDUCE_MIX0 = [6, 7, 8, 9, 10, 11, 12, 13], [2, 3, 4, 5], [0, 1]
BIG_NAMES = ["w_in_even", "w_out_even", "w_gate_up", "w_down", "w_ple_up", "w_ple_gate", "w_in_odd", "rg_wa", "rg_wx",
             "w_out_odd"]
SMALL_SHARDED = ["conv_w", "conv_b", "rg_ba", "rg_bx", "rg_lambda"]
REPLICATED = ["mix_pre_g", "mix_post_g", "ffn_pre_g", "ffn_post_g", "ple_norm_g", "hg_lb_logits", "hg_norm_g"]
TR = 256
TR_WIDE = 128


def _ffn_ple_fwd(h, m, l, gains, w_gu, w_d, w_pu, w_pg, pb, next_pre, hook=None):
    d = h.shape[1]
    ff = w_d.shape[0]
    hb, nf = _rowwise(_f_post_pre, [h, m], [gains["mix_post_g"][l], gains["ffn_pre_g"][l]], [(d, f32), (d, bf16)],
                      f"post_mix{l}", TR)
    gu = _mm(nf, w_gu, "nn", f32, f"ffn_up{l}", b_shards=True)
    (act,) = _rowwise(_f_swiglu, [(gu, ff, 0), (gu, ff, 1)], [], [(ff, bf16)], f"swiglu{l}", TR_WIDE)
    f = _mm(act, w_d, "nn", f32, f"ffn_down{l}")
    started = None if hook is None else hook(f)
    hc, hcb = _rowwise(_f_post, [hb, f], [gains["ffn_post_g"][l]], [(d, f32), (d, bf16)], f"post_ffn{l}", TR)
    t = _mm(hcb, w_pg, "nn", f32, f"ple_gate{l}", after=started)
    e = _mm(pb, w_pu, "nn", f32, f"ple_up{l}", b_shards=True)
    if next_pre is None:
        (hd,) = _rowwise(_f_ple, [hc, t, e], [gains["ple_norm_g"][l]], [(d, f32)], f"ple{l}", TR)
        n_next = None
    else:
        hd, n_next = _rowwise(_f_ple_pre, [hc, t, e], [gains["ple_norm_g"][l], next_pre], [(d, f32), (d, bf16)],
                              f"ple{l}", TR)
    saved = dict(h=h, m=m, hb=hb, nf=nf, gu=gu, act=act, f=f, hc=hc, hcb=hcb, t=t, e=e)
    return hd, n_next, saved


def _ffn_ple_bwd(sv, l, gains, w_gu, w_d, w_pg, pb, next_pre, dhd, dn_next, after=None, hook=None):
    ff = w_d.shape[0]
    if next_pre is None:
        (dhc, dt, de), (dg_ple,) = _rowwise_vjp(_f_ple, [sv["hc"], sv["t"], sv["e"]], [gains["ple_norm_g"][l]], [dhd],
                                                [(0, f32), (1, bf16), (2, bf16)], f"ple_bwd{l}", TR)
        dg_next = None
    else:
        (dhc, dt, de), (dg_ple, dg_next) = _rowwise_vjp(
            _f_ple_pre, [sv["hc"], sv["t"], sv["e"]], [gains["ple_norm_g"][l], next_pre], [dhd, dn_next],
            [(0, f32), (1, bf16), (2, bf16)], f"ple_bwd{l}", TR)
    gw = {}
    gw["w_ple_gate"] = _mm(sv["hcb"], dt, "tn", bf16, f"ple_gate_wgrad{l}")
    gw["w_ple_up"] = _mm(pb, de, "tn", bf16, f"ple_up_wgrad{l}", out_shards=True)
    dhcb = _mm(dt, w_pg, "nt", f32, f"ple_gate_dgrad{l}", after=after)
    (dhb, df), (dg_fpost,) = _rowwise_vjp(_f_post, [sv["hb"], sv["f"]], [gains["ffn_post_g"][l]], [dhc, dhcb],
                                          [(0, f32), (1, bf16)], f"post_ffn_bwd{l}", TR)
    gw["w_down"] = _mm(sv["act"], df, "tn", bf16, f"ffn_down_wgrad{l}")
    dact = _mm(df, w_d, "nt", f32, f"ffn_down_dgrad{l}")
    started = None if hook is None else hook(dact)
    (dgu,), _ = _rowwise_vjp(_f_swiglu, [(sv["gu"], ff, 0), (sv["gu"], ff, 1)], [], [dact], [(0, bf16), (1, bf16)],
                             f"swiglu_bwd{l}", TR_WIDE, concat=True)
    gw["w_gate_up"] = _mm(sv["nf"], dgu, "tn", bf16, f"ffn_up_wgrad{l}", out_shards=True)
    dnf = _mm(dgu, w_gu, "nt", f32, f"ffn_up_dgrad{l}", b_shards=True, after=started)
    (dh, dm), (dg_post, dg_fpre) = _rowwise_vjp(
        _f_post_pre, [sv["h"], sv["m"]], [gains["mix_post_g"][l], gains["ffn_pre_g"][l]], [dhb, dnf],
        [(0, f32), (1, bf16)], f"post_mix_bwd{l}", TR)
    gg = dict(ple_norm_g=dg_ple, ffn_post_g=dg_fpost, mix_post_g=dg_post, ffn_pre_g=dg_fpre)
    return dh, dm, gw, gg, dg_next


def kernel(x, p, mix_pre_g, mix_post_g, ffn_pre_g, ffn_post_g, ple_norm_g, w_in_even, w_out_even, hg_lb_logits, hg_norm_g, w_in_odd, conv_w, conv_b, rg_wa, rg_ba, rg_wx, rg_bx, rg_lambda, w_out_odd, w_gate_up, w_down, w_ple_up, w_ple_gate, loss_target, m_mix_pre_g, m_mix_post_g, m_ffn_pre_g, m_ffn_post_g, m_ple_norm_g, m_w_in_even, m_w_out_even, m_hg_lb_logits, m_hg_norm_g, m_w_in_odd, m_conv_w, m_conv_b, m_rg_wa, m_rg_ba, m_rg_wx, m_rg_bx, m_rg_lambda, m_w_out_odd, m_w_gate_up, m_w_down, m_w_ple_up, m_w_ple_gate, v_mix_pre_g, v_mix_post_g, v_ffn_pre_g, v_ffn_post_g, v_ple_norm_g, v_w_in_even, v_w_out_even, v_hg_lb_logits, v_hg_norm_g, v_w_in_odd, v_conv_w, v_conv_b, v_rg_wa, v_rg_ba, v_rg_wx, v_rg_bx, v_rg_lambda, v_w_out_odd, v_w_gate_up, v_w_down, v_w_ple_up, v_w_ple_gate):
    given = dict(locals())
    w = {n: given[n] for n in WEIGHTS}
    mom = {n: given["m_" + n] for n in WEIGHTS}
    var = {n: given["v_" + n] for n in WEIGHTS}
    xi, yi = lax.axis_index("x"), lax.axis_index("y")
    my_q = 2 * xi + yi
    h0 = x[0]
    s, d = h0.shape
    n_layers = p.shape[0]
    gains = {n: [w[n][l:l + 1] for l in range(n_layers)]
             for n in ("mix_pre_g", "mix_post_g", "ffn_pre_g", "ffn_post_g", "ple_norm_g")}

    kinds = [k for _, _, k in BIG]
    place = jnp.stack([my_q, lax.axis_index("c")]).astype(jnp.int32)
    full, in_flight, order = {}, {}, place
    for tag, group in GATHER_GROUPS:
        kinds_g = [kinds[i] for i in group]
        slots = [_cast_into_slot(w[BIG[i][0]], BIG[i][1], BIG[i][2], place) for i in group]
        send_sems, recv_sems, slots, order = _copies_start(_gather_over_ici(kinds_g), slots, 3 * len(group),
                                                           "gather_start_" + tag, order)
        in_flight[tag] = (group, kinds_g, send_sems, recv_sems, slots)
    gather_started = order

    def arrive(tag, after):
        group, kinds_g, send_sems, recv_sems, slots = in_flight[tag]
        slots = _copies_wait(_gather_over_ici(kinds_g), send_sems, recv_sems, slots, "gather_wait_" + tag, [after])
        slots = _comm_call(_gather_hand_on(slots, kinds_g), "gather_hand_on_" + tag)
        full.update({BIG[i][:2]: b for i, b in zip(group, slots)})

    handing = {}

    def landed(tag, after):
        group, kinds_g, send_sems, recv_sems, slots = in_flight[tag]
        slots = _copies_wait(_gather_over_ici(kinds_g), send_sems, recv_sems, slots, "gather_wait_" + tag, after)
        make = _gather_hand_on_copies(kinds_g)
        send_sems, recv_sems, slots, started = _copies_start(make, slots, 3 * len(group), "gather_hand_on_start_" + tag,
                                                             place)
        handing[tag] = (group, make, send_sems, recv_sems, slots)
        return started

    def handed(tag, after):
        group, make, send_sems, recv_sems, slots = handing[tag]
        slots = _copies_wait(make, send_sems, recv_sems, slots, "gather_hand_on_wait_" + tag, after)
        full.update({BIG[i][:2]: b for i, b in zip(group, slots)})

    def rows_full(name, l):
        g = full[(name, l)]
        return g.reshape(g.shape[0] * g.shape[1], g.shape[2])

    small_shapes = [w[n].shape[1:] for n in SMALL_SHARDED]
    small_rows = _all_gather_rows(_pack([w[n][0] for n in SMALL_SHARDED]), False)
    per_chip = [_unpack(small_rows.reshape(8, -1)[2 * q], small_shapes) for q in range(N_CHIPS)]
    conv_w_f, conv_b_f, rg_ba_f, rg_bx_f, rg_lam_f = [
        jnp.concatenate([per_chip[q][i] for q in range(N_CHIPS)], axis=-1) for i in range(len(SMALL_SHARDED))]
    lru = conv_b_f.shape[0]
    conv_b_f, rg_ba_f, rg_bx_f, rg_lam_f = [v.reshape(1, lru) for v in (conv_b_f, rg_ba_f, rg_bx_f, rg_lam_f)]
    pb = [_ew(lambda v: (v,), [p[l, 0]], [bf16], f"cast_p{l}")[0] for l in range(n_layers)]

    (n0,) = _rowwise(_f_pre, [h0], [gains["mix_pre_g"][0]], [(d, bf16)], "pre_mix0", TR)
    arrive("in0", gather_started)
    proj0 = _mm(n0, full[("w_in_even", 0)], "nn", f32, "in_even", b_shards=True)
    a_out, _ = _sb_fwd(proj0)
    b_out, hg_states, _ = _hg_fwd(proj0, w["hg_lb_logits"], w["hg_norm_g"])
    arrive("out0", b_out)
    ffn0_handing = landed("ffn0", [a_out, b_out])
    mix0 = jnp.concatenate([a_out, b_out], axis=1).astype(bf16)
    m0 = _mm(mix0, rows_full("w_out_even", 0), "nn", f32, "out_even", after=ffn0_handing)
    handed("ffn0", [m0])
    h1, n1, sv0 = _ffn_ple_fwd(h0, m0, 0, gains, full[("w_gate_up", 0)], rows_full("w_down", 0), full[("w_ple_up", 0)],
                               rows_full("w_ple_gate", 0), pb[0], gains["mix_pre_g"][1],
                               hook=lambda f: landed("mix1", [f]))

    handed("mix1", [n1])
    proj1 = _mm(n1, full[("w_in_odd", 0)], "nn", f32, "in_odd", b_shards=True)
    xc, xcb = _conv_fwd(proj1, conv_w_f, conv_b_f)
    ra, ix = _bd_fwd(xcb, full[("rg_wa", 0)], full[("rg_wx", 0)])
    gate_in = [ra, ix, xc], [rg_ba_f, rg_bx_f, rg_lam_f]
    a_rec, u_rec = _rowwise(_f_rg_gate, *gate_in, [(lru, f32), (lru, f32)], "rg_gate", TR_WIDE, row_ids=True)
    hs = _scan_fwd(a_rec, u_rec)
    out_in = [(proj1, lru, 0), hs]
    (yb,) = _rowwise(_f_rg_out, out_in, [], [(lru, bf16)], "rg_out", TR_WIDE)
    m1 = _mm(yb, rows_full("w_out_odd", 0), "nn", f32, "out_odd", after=landed("ffn1", [hs]))
    handed("ffn1", [m1])
    h2, _, sv1 = _ffn_ple_fwd(h1, m1, 1, gains, full[("w_gate_up", 1)], rows_full("w_down", 1), full[("w_ple_up", 1)],
                              rows_full("w_ple_gate", 1), pb[1], None)
    dy, loss_local = _loss_head(h2, loss_target[0])
    loss = lax.psum(loss_local, ("x", "y", "c"))

    gbig, chip_part, got, reducing, core_swaps = {}, {}, {}, {}, {}

    def shard_major(group):
        gl = []
        for i in group:
            n_, l_, k_ = BIG[i]
            g_ = gbig[(n_, l_)]
            gl.append(g_.reshape(N_CHIPS, g_.shape[0] // N_CHIPS, g_.shape[1]) if k_ == "row" else g_)
        return gl

    def chips_start(group, tag, gl, theirs):
        chip_part.update({i: _add_cores(g_, t_, kinds[i], place) for i, g_, t_ in zip(group, gl, theirs)})
        make, arrays, n_sems = _exchange_chips([chip_part[i] for i in group])
        send_sems, recv_sems, arrays, started = _copies_start(make, arrays, n_sems, "rs_chip_start_" + tag, place)
        reducing[tag] = (group, make, send_sems, recv_sems, arrays)
        return started

    def core_sums(group, tag):
        gl = shard_major(group)
        theirs = _comm_call(_exchange_halves(gl, [kinds[i] for i in group]), "rs_core_exchange_" + tag)
        return chips_start(group, tag, gl, theirs)

    def core_sums_start(group, tag):
        make, arrays, n_sems = _exchange_halves_copies(shard_major(group), [kinds[i] for i in group])
        send_sems, recv_sems, arrays, started = _copies_start(make, arrays, n_sems, "rs_core_start_" + tag, place)
        core_swaps[tag] = (group, make, send_sems, recv_sems, arrays)
        return started

    def core_sums_finish(tag, after):
        group, make, send_sems, recv_sems, arrays = core_swaps[tag]
        arrays = _copies_wait(make, send_sems, recv_sems, arrays, "rs_core_wait_" + tag, after)
        return chips_start(group, tag, arrays[:len(group)], arrays[len(group):])

    def chips_arrive(tag, after):
        group, make, send_sems, recv_sems, arrays = reducing[tag]
        arrays = _copies_wait(make, send_sems, recv_sems, arrays, "rs_chip_wait_" + tag, after)
        got.update(zip(group, arrays[len(group):]))

    def reduced_shards(group, tag):
        sums = [_sum_chips(chip_part[i], got[i], kinds[i], place) for i in group]
        whole = _comm_call(_share_halves(sums, [kinds[i] for i in group]), "rs_share_halves_" + tag)
        return {BIG[i][:2]: r for i, r in zip(group, whole)}

    dh1, dm1, gw, gg1, _ = _ffn_ple_bwd(sv1, 1, gains, full[("w_gate_up", 1)], rows_full("w_down", 1),
                                        rows_full("w_ple_gate", 1), pb[1], None, dy, None)
    gbig.update({(n, 1): g for n, g in gw.items()})
    gbig[("w_out_odd", 0)] = _mm(yb, dm1, "tn", bf16, "out_odd_wgrad")
    dyo = _mm(dm1, rows_full("w_out_odd", 0), "nt", f32, "out_odd_dgrad")
    (dgate, dhs), _ = _rowwise_vjp(_f_rg_out, out_in, [], [dyo], [(0, bf16), (1, f32)], "rg_out_bwd", TR_WIDE)
    du, da = _scan_bwd(a_rec, hs, dhs)
    (dra, dix, dxc_direct), (g_ba, g_bx, g_lam) = _rowwise_vjp(
        _f_rg_gate, *gate_in, [da, du], [(0, bf16), (1, bf16), (2, f32)], "rg_gate_bwd", TR_WIDE, row_ids=True)
    dxc_gates, gbig[("rg_wa", 0)], gbig[("rg_wx", 0)] = _bd_bwd(xcb, full[("rg_wa", 0)], full[("rg_wx", 0)], dra, dix)
    dxb, g_conv_w, g_conv_b = _conv_bwd(proj1, conv_w_f, dxc_direct, dxc_gates)
    dproj1 = jnp.concatenate([dgate, dxb], axis=1)
    gbig[("w_in_odd", 0)] = _mm(n1, dproj1, "tn", bf16, "in_odd_wgrad", out_shards=True)
    dn1 = _mm(dproj1, full[("w_in_odd", 0)], "nt", f32, "in_odd_dgrad", b_shards=True)
    layer1_swapping = core_sums_start(REDUCE_LAYER1, "layer1")

    dh0, dm0, gw, gg0, g_pre1 = _ffn_ple_bwd(sv0, 0, gains, full[("w_gate_up", 0)], rows_full("w_down", 0),
                                             rows_full("w_ple_gate", 0), pb[0], gains["mix_pre_g"][1], dh1, dn1,
                                             after=layer1_swapping,
                                             hook=lambda dact: core_sums_finish("layer1", [dact]))
    gbig.update({(n, 0): g for n, g in gw.items()})
    ffn0_swapping = core_sums_start(REDUCE_FFN0, "ffn0")
    gbig[("w_out_even", 0)] = _mm(mix0, dm0, "tn", bf16, "out_even_wgrad")
    dmix = _mm(dm0, rows_full("w_out_even", 0), "nt", f32, "out_even_dgrad", after=ffn0_swapping)
    (dq, dk, dv), _ = _sb_bwd(proj0, a_out, dmix)
    ffn0_started = core_sums_finish("ffn0", [dq])
    dhg, g_logits, g_ng, _ = _hg_bwd(proj0, w["hg_lb_logits"], w["hg_norm_g"], hg_states, dmix, HEADS,
                                     after=ffn0_started)
    dproj0 = jnp.concatenate([dq, dk, dv, dhg], axis=1)
    gbig[("w_in_even", 0)] = _mm(n0, dproj0, "tn", bf16, "in_even_wgrad", out_shards=True)
    dn0 = _mm(dproj0, full[("w_in_even", 0)], "nt", f32, "in_even_dgrad", b_shards=True)
    (grad_x,), (g_pre0,) = _rowwise_vjp(_f_pre_id, [h0], [gains["mix_pre_g"][0]], [dh0, dn0], [(0, f32)], "pre_mix0_bwd", TR)
    mix0_started = core_sums(REDUCE_MIX0, "mix0")

    g_small = dict(
        mix_pre_g=jnp.concatenate([g_pre0, g_pre1], 0), mix_post_g=jnp.concatenate([gg0["mix_post_g"], gg1["mix_post_g"]], 0),
        ffn_pre_g=jnp.concatenate([gg0["ffn_pre_g"], gg1["ffn_pre_g"]], 0),
        ffn_post_g=jnp.concatenate([gg0["ffn_post_g"], gg1["ffn_post_g"]], 0),
        ple_norm_g=jnp.concatenate([gg0["ple_norm_g"], gg1["ple_norm_g"]], 0),
        hg_lb_logits=g_logits, hg_norm_g=jnp.sum(g_ng, axis=0, keepdims=True),
        conv_w=g_conv_w, conv_b=g_conv_b, rg_ba=g_ba, rg_bx=g_bx, rg_lambda=g_lam)
    order = REPLICATED + SMALL_SHARDED
    summed = dict(zip(order, _unpack(_all_reduce_rows(_pack([g_small[n] for n in order])),
                                     [g_small[n].shape for n in order])))
    grad_small = {n: summed[n].reshape(w[n].shape) for n in REPLICATED}
    for n in SMALL_SHARDED:
        width = w[n].shape[-1]
        if n in ("rg_ba", "rg_bx"):
            full_g = summed[n].reshape(1, -1, RG_BLOCK)
        else:
            full_g = summed[n].reshape((1,) + g_small[n].shape) if n == "conv_w" else summed[n]
        grad_small[n] = lax.dynamic_slice_in_dim(full_g, my_q * width, width, axis=full_g.ndim - 1).reshape(w[n].shape)

    grads = dict(grad_small)
    delta, new_m, new_v = {}, {}, {}

    def update(names, by_layer):
        for n in names:
            grads[n] = jnp.stack([by_layer[(n, l)] for l in range(w[n].shape[0])])
            delta[n], new_m[n], new_v[n] = _ew(_f_adamw, [w[n], grads[n], mom[n], var[n]], [f32, f32, f32], "adamw_" + n)

    chips_arrive("layer1", [mix0_started])
    chips_arrive("ffn0", [mix0_started])
    rest = [n for n in BIG_NAMES if all(BIG[i][0] != n for i in REDUCE_MIX0)]
    update(rest, {**reduced_shards(REDUCE_LAYER1, "layer1"), **reduced_shards(REDUCE_FFN0, "ffn0")})
    chips_arrive("mix0", [delta[n] for n in rest])
    update([n for n in BIG_NAMES if n not in rest], reduced_shards(REDUCE_MIX0, "mix0"))
    small = REPLICATED + SMALL_SHARDED
    packed = [_pack([src[n] for n in small]) for src in (w, grads, mom, var)]
    outs = _ew(_f_adamw, packed, [f32, f32, f32], "adamw_small")
    for dst, o in zip((delta, new_m, new_v), outs):
        dst.update(zip(small, _unpack(o, [w[n].shape for n in small])))
    return (loss, grad_x.reshape(x.shape), *[grads[n] for n in WEIGHTS], *[delta[n] for n in WEIGHTS],
            *[new_m[n] for n in WEIGHTS], *[new_v[n] for n in WEIGHTS])


def _all_reduce_rows(v):
    return _all_gather_rows(v, True)
```

```python
import functools

import jax
import jax.numpy as jnp
from jax import lax
from jax.experimental import pallas as pl
from jax.experimental.pallas import tpu as pltpu

f32 = jnp.float32
bf16 = jnp.bfloat16
HIGHEST = lax.Precision.HIGHEST
MESH = pl.DeviceIdType.MESH

LANES = 128
SUBLANES = 8
MXU_WIDTH = 256
VMEM_LIMIT_BYTES = 56 * 1024 * 1024

RMS_EPS = 1e-6
HEADS = 8
HEAD_DIM = 128
SB_BLOCK = 128
HG_SUB = 16
HG_CHUNK = 64
RG_BLOCK = 256
CONV_WIDTH = 4
RG_C = 8.0
ADAM_LR, ADAM_B1, ADAM_B2, ADAM_EPS, ADAM_WD, ADAM_STEP = 0.001, 0.9, 0.999, 1e-08, 0.01, 10
N_CHIPS = 4

NN = (((1,), (0,)), ((), ()))
NT = (((1,), (1,)), ((), ()))
TN = (((0,), (0,)), ((), ()))


def _params(*sem):
    return pltpu.CompilerParams(dimension_semantics=sem, vmem_limit_bytes=VMEM_LIMIT_BYTES)


class _Comm:
    def __init__(self, arrays, out_shapes, aliases, n_sems, start, finish):
        self.arrays, self.out_shapes, self.aliases, self.n_sems = list(arrays), list(out_shapes), dict(aliases), n_sems
        self.start, self.finish = start, finish


def _grid_call(body, name, steps, out_shape, in_specs, out_specs, scratch_shapes, args, comm=None):
    n_in, n_out, n_scr = len(in_specs), len(out_shape), len(scratch_shapes)
    if comm is None:
        res = pl.pallas_call(body, name=name, grid=(steps,), out_shape=list(out_shape), in_specs=list(in_specs),
                             out_specs=list(out_specs), scratch_shapes=list(scratch_shapes),
                             compiler_params=_params("parallel"))(*args)
        return list(res), []
    c_in, c_out = len(comm.arrays), len(comm.out_shapes)
    any_spec = pl.BlockSpec(memory_space=pl.ANY)

    def wrapped(*refs):
        ins, refs = refs[:n_in], refs[n_in:]
        cins, refs = refs[:c_in], refs[c_in:]
        outs, refs = refs[:n_out], refs[n_out:]
        couts, refs = refs[:c_out], refs[c_out:]
        scr, (send_sems, recv_sems) = refs[:n_scr], refs[n_scr:]

        @pl.when(pl.program_id(0) == 0)
        def _():
            comm.start(cins, couts, send_sems, recv_sems)

        body(*ins, *outs, *scr)

        @pl.when(pl.program_id(0) == steps - 1)
        def _():
            comm.finish(cins, couts, send_sems, recv_sems)

    res = pl.pallas_call(
        wrapped, name=name, grid=(steps,), out_shape=list(out_shape) + comm.out_shapes,
        in_specs=list(in_specs) + [any_spec] * c_in, out_specs=list(out_specs) + [any_spec] * c_out,
        scratch_shapes=list(scratch_shapes) + [pltpu.SemaphoreType.DMA((comm.n_sems,))] * 2,
        input_output_aliases={n_in + i: n_out + o for i, o in comm.aliases.items()},
        compiler_params=_params("arbitrary"),
    )(*args, *comm.arrays)
    return list(res[:n_out]), list(res[n_out:])


def _comm_call(comm, name):
    c_in, c_out = len(comm.arrays), len(comm.out_shapes)
    any_spec = pl.BlockSpec(memory_space=pl.ANY)

    def body(*refs):
        ins, outs, (send_sems, recv_sems) = refs[:c_in], refs[c_in:c_in + c_out], refs[c_in + c_out:]
        comm.start(ins, outs, send_sems, recv_sems)
        comm.finish(ins, outs, send_sems, recv_sems)

    return pl.pallas_call(
        body, name=name, out_shape=comm.out_shapes, in_specs=[any_spec] * c_in, out_specs=[any_spec] * c_out,
        input_output_aliases=comm.aliases, scratch_shapes=[pltpu.SemaphoreType.DMA((comm.n_sems,))] * 2,
    )(*comm.arrays)


def _copies_start(make_copies, arrays, n_sems, name, after):
    n = len(arrays)
    hbm = pl.BlockSpec(memory_space=pltpu.HBM)
    sem = pl.BlockSpec(memory_space=pltpu.SEMAPHORE)

    def body(*refs):
        thru = refs[n + 3:2 * n + 3]
        for cp in make_copies(thru, refs[n + 1], refs[n + 2]):
            cp.start()
        refs[2 * n + 3][...] = jnp.zeros_like(refs[2 * n + 3])

    res = pl.pallas_call(
        body, name=name,
        out_shape=[pltpu.SemaphoreType.DMA((n_sems,)), pltpu.SemaphoreType.DMA((n_sems,))]
        + [pltpu.HBM(a.shape, a.dtype) for a in arrays] + [jax.ShapeDtypeStruct((SUBLANES, LANES), f32)],
        in_specs=[hbm] * n + [pl.BlockSpec(memory_space=pl.ANY)],
        out_specs=[sem, sem] + [hbm] * n + [pl.BlockSpec(memory_space=pltpu.VMEM)],
        input_output_aliases={i: 2 + i for i in range(n)},
        compiler_params=pltpu.CompilerParams(has_side_effects=pltpu.SideEffectType.DATAFLOW_SIDE_EFFECTING),
    )(*[pltpu.with_memory_space_constraint(a, pltpu.HBM) for a in arrays], after)
    return res[0], res[1], list(res[2:2 + n]), res[2 + n]


def _copies_wait(make_copies, send_sems, recv_sems, arrays, name, after):
    n = len(arrays)
    hbm = pl.BlockSpec(memory_space=pltpu.HBM)
    sem = pl.BlockSpec(memory_space=pltpu.SEMAPHORE)

    def body(*refs):
        for cp in make_copies(refs[:n], refs[n], refs[n + 1]):
            cp.wait_send()
            cp.wait_recv()

    return list(pl.pallas_call(
        body, name=name, out_shape=[pltpu.HBM(a.shape, a.dtype) for a in arrays],
        in_specs=[hbm] * n + [sem, sem] + [pl.BlockSpec(memory_space=pl.ANY)] * len(after), out_specs=[hbm] * n,
        input_output_aliases={i: i for i in range(n)},
        compiler_params=pltpu.CompilerParams(has_side_effects=pltpu.SideEffectType.DATAFLOW_SIDE_EFFECTING),
    )(*arrays, send_sems, recv_sems, *after))


def _pick(dim, target, align=LANES):
    if dim <= target:
        return dim
    t = (target // align) * align
    while t >= align:
        if dim % t == 0:
            return t
        t -= align
    return dim


MM_FULL_K = 2816
MM_CHUNK_K = 1408
MM_VMEM_BUDGET = 44 * 1024 * 1024


def _mm_tiles(m, n_unit, k_unit, k_whole, out_bytes):
    tk = k_unit if (k_whole and k_unit <= MM_FULL_K) else _pick(k_unit, MM_CHUNK_K)
    tk = k_unit if tk < MXU_WIDTH else tk
    tm = m if m <= 2048 else _pick(m, 512)
    tn = _pick(n_unit, 2048 if (m > 2048 or tk < k_unit or not k_whole) else 1024)
    tn = n_unit if tn < MXU_WIDTH else tn
    need = lambda: 2 * (2 * tm * tk + 2 * tk * tn + out_bytes * tm * tn)
    while need() > MM_VMEM_BUDGET:
        if tn > 512 and (tn // 2) % LANES == 0 and n_unit % (tn // 2) == 0:
            tn //= 2
        elif tm > 256:
            tm //= 2
        else:
            break
    return tm, tn, tk


def _mm(a, b, mode, out_dtype, name, *, b_shards=False, out_shards=False, after=None):
    if mode == "nn":
        m, kc = a.shape
        n = b.shape[-1] * (N_CHIPS if b_shards else 1)
    elif mode == "nt":
        m, kc = a.shape
        n = b.shape[-2]
    else:
        kc, m = a.shape
        n = b.shape[-1]
    n_sh = n // N_CHIPS
    kc_sh = kc // N_CHIPS
    n_unit = n_sh if (out_shards or (b_shards and mode == "nn")) else n
    k_split = b_shards and mode == "nt"
    tm, tn, tk = _mm_tiles(m, n_unit, kc_sh if k_split else kc, not k_split, jnp.dtype(out_dtype).itemsize)
    nk = kc // tk
    assert nk == 1 or out_dtype == f32, "a chunked contraction accumulates in the f32 output block"
    dims = {"nn": NN, "nt": NT, "tn": TN}[mode]

    def body(a_ref, b_ref, *rest):
        o_ref = rest[-1]
        part = lax.dot_general(a_ref[...], b_ref[...], dims, preferred_element_type=f32)
        if nk == 1:
            o_ref[...] = part.astype(o_ref.dtype)
        else:
            k = pl.program_id(2)

            @pl.when(k == 0)
            def _():
                o_ref[...] = part

            @pl.when(k > 0)
            def _():
                o_ref[...] += part

    if mode == "tn":
        a_spec = pl.BlockSpec((tk, tm), lambda i, j, k: (k, i))
    else:
        a_spec = pl.BlockSpec((tm, tk), lambda i, j, k: (i, k))
    nb_sh = n_sh // tn
    if mode == "nn":
        if b_shards:
            b_spec = pl.BlockSpec((None, tk, tn), lambda i, j, k: (j // nb_sh, k, j % nb_sh))
        else:
            b_spec = pl.BlockSpec((tk, tn), lambda i, j, k: (k, j))
    elif mode == "nt":
        if b_shards:
            kb_sh = kc_sh // tk
            b_spec = pl.BlockSpec((None, tn, tk), lambda i, j, k: (k // kb_sh, j, k % kb_sh))
        else:
            b_spec = pl.BlockSpec((tn, tk), lambda i, j, k: (j, k))
    else:
        b_spec = pl.BlockSpec((tk, tn), lambda i, j, k: (k, j))
    if out_shards:
        out_shape = jax.ShapeDtypeStruct((N_CHIPS, m, n_sh), out_dtype)
        o_spec = pl.BlockSpec((None, tm, tn), lambda i, j, k: (j // nb_sh, i, j % nb_sh))
    else:
        out_shape = jax.ShapeDtypeStruct((m, n), out_dtype)
        o_spec = pl.BlockSpec((tm, tn), lambda i, j, k: (i, j))
    order = [] if after is None else [after]
    return pl.pallas_call(
        body, name=name, out_shape=out_shape, grid=(m // tm, n // tn, nk),
        in_specs=[a_spec, b_spec] + [pl.BlockSpec(memory_space=pl.ANY)] * len(order), out_specs=o_spec,
        compiler_params=_params("parallel", "parallel", "arbitrary"),
    )(a, b, *order)


def _row_spec(entry, tr):
    if isinstance(entry, tuple):
        arr, width, cb = entry[:3]
        rb = (entry[3] // tr) if len(entry) > 3 else 0
        return arr, pl.BlockSpec((tr, width), lambda i, cb=cb, rb=rb: (i + rb, cb))
    return entry, pl.BlockSpec((tr, entry.shape[1]), lambda i: (i, 0))


def _whole_spec(arr):
    return pl.BlockSpec(arr.shape, lambda i: (0,) * arr.ndim)


def _rowwise(fn, rows, bcast, outs, name, tr, row_ids=False, m=None):
    arrs, specs = zip(*[_row_spec(e, tr) for e in rows])
    m = arrs[0].shape[0] if m is None else m
    nr, nb = len(rows), len(bcast)

    def body(*refs):
        vals = [r[...] for r in refs[:nr + nb]]
        if row_ids:
            rid = pl.program_id(0) * tr + lax.broadcasted_iota(jnp.int32, (tr, 1), 0)
            res = fn(rid, *vals)
        else:
            res = fn(*vals)
        for r, o in zip(res, refs[nr + nb:]):
            o[...] = r.astype(o.dtype)

    return pl.pallas_call(
        body, name=name, grid=(m // tr,),
        out_shape=[jax.ShapeDtypeStruct((m, w), dt) for w, dt in outs],
        in_specs=list(specs) + [_whole_spec(b) for b in bcast],
        out_specs=[pl.BlockSpec((tr, w), lambda i: (i, 0)) for w, _ in outs],
        compiler_params=_params("parallel"),
    )(*arrs, *bcast)


def _rowwise_vjp(fn, rows, bcast, cts, row_grads, name, tr, row_ids=False, concat=False):
    arrs, specs = zip(*[_row_spec(e, tr) for e in rows])
    ct_arrs, ct_specs = zip(*[_row_spec(e, tr) for e in cts])
    m = arrs[0].shape[0]
    nr, nb, nc, ng = len(rows), len(bcast), len(cts), len(row_grads)
    widths = [specs[idx].block_shape[1] for idx, _ in row_grads]

    def body(*refs):
        vals = [r[...] for r in refs[:nr + nb]]
        ctv = [r[...] for r in refs[nr + nb:nr + nb + nc]]
        out_refs = refs[nr + nb + nc:]
        if row_ids:
            rid = pl.program_id(0) * tr + lax.broadcasted_iota(jnp.int32, (tr, 1), 0)
            f = functools.partial(fn, rid)
        else:
            f = fn
        res, vjp = jax.vjp(f, *vals)
        grads = vjp(tuple(c.astype(r.dtype) for c, r in zip(ctv, res)))
        if concat:
            off = 0
            for (idx, _), w in zip(row_grads, widths):
                out_refs[0][:, off:off + w] = grads[idx].astype(out_refs[0].dtype)
                off += w
            b_refs = out_refs[1:]
        else:
            for (idx, _), o in zip(row_grads, out_refs[:ng]):
                o[...] = grads[idx].astype(o.dtype)
            b_refs = out_refs[ng:]

        @pl.when(pl.program_id(0) == 0)
        def _():
            for o in b_refs:
                o[...] = jnp.zeros_like(o)

        for j, o in enumerate(b_refs):
            o[...] += grads[nr + j].astype(f32)

    if concat:
        dt = row_grads[0][1]
        g_shapes = [jax.ShapeDtypeStruct((m, sum(widths)), dt)]
        g_specs = [pl.BlockSpec((tr, sum(widths)), lambda i: (i, 0))]
    else:
        g_shapes = [jax.ShapeDtypeStruct((m, w), dt) for (_, dt), w in zip(row_grads, widths)]
        g_specs = [pl.BlockSpec((tr, w), lambda i: (i, 0)) for w in widths]
    res = pl.pallas_call(
        body, name=name, grid=(m // tr,),
        out_shape=g_shapes + [jax.ShapeDtypeStruct(b.shape, f32) for b in bcast],
        in_specs=list(specs) + [_whole_spec(b) for b in bcast] + list(ct_specs),
        out_specs=g_specs + [_whole_spec(b) for b in bcast],
        compiler_params=_params("arbitrary"),
    )(*arrs, *bcast, *ct_arrs)
    n_g = len(g_shapes)
    return list(res[:n_g]), list(res[n_g:])


def _rms(x, g):
    return x * lax.rsqrt(jnp.mean(x * x, axis=-1, keepdims=True) + RMS_EPS) * g


def _sigmoid(x):
    return 1.0 / (1.0 + jnp.exp(-x))


def _silu(x):
    return x * _sigmoid(x)


def _gelu_tanh(x):
    return 0.5 * x * (1.0 + jnp.tanh(0.7978845608028654 * (x + 0.044715 * (x * x * x))))


def _softplus(x):
    return jnp.maximum(x, 0.0) + jnp.log(1.0 + jnp.exp(-jnp.abs(x)))


def _expm1(x):
    small = jnp.abs(x) < 0.01
    series = x * (1.0 + x * (0.5 + x * (1.0 / 6.0 + x * (1.0 / 24.0))))
    return jnp.where(small, series, jnp.exp(jnp.where(small, 0.0, x)) - 1.0)


def _f_pre(h, g):
    return (_rms(h, g),)


def _f_pre_id(h, g):
    return h, _rms(h, g)


def _f_post_pre(h, m, g_post, g_pre):
    hb = h + _rms(m, g_post)
    return hb, _rms(hb, g_pre)


def _f_post(h, m, g_post):
    hc = h + _rms(m, g_post)
    return hc, hc


def _f_swiglu(g, u):
    return (_silu(g) * u,)


def _f_ple_pre(h, t, e, g_ple, g_pre):
    hd = h + _rms(_sigmoid(t) * e, g_ple)
    return hd, _rms(hd, g_pre)


def _f_ple(h, t, e, g_ple):
    return (h + _rms(_sigmoid(t) * e, g_ple),)


def _f_rg_gate(rid, ra, ix, xc, ba, bx, lam):
    r = _sigmoid(ra + ba)
    i = _sigmoid(ix + bx)
    log_a = -RG_C * r * _softplus(-lam)
    a = jnp.exp(log_a)
    mult = jnp.sqrt(-_expm1(2.0 * log_a))
    mult = jnp.where(rid == 0, 1.0, mult)
    return a, xc * i * mult


def _f_rg_out(gate, hs):
    return (_gelu_tanh(gate) * hs,)


def _loss_head(y, target, tr=256):
    m, d = y.shape

    def body(y_ref, t_ref, dy_ref, l_ref):
        err = y_ref[...] - t_ref[...]
        dy_ref[...] = err * (1.0 / d)

        @pl.when(pl.program_id(0) == 0)
        def _():
            l_ref[...] = jnp.zeros_like(l_ref)

        part = 0.5 * jnp.sum(jnp.mean(err * err, axis=-1, keepdims=True), axis=0, keepdims=True)
        l_ref[...] += jnp.broadcast_to(part, l_ref.shape)

    dy, l = pl.pallas_call(
        body, name="loss_head", grid=(m // tr,),
        out_shape=[jax.ShapeDtypeStruct((m, d), f32), jax.ShapeDtypeStruct((SUBLANES, LANES), f32)],
        in_specs=[pl.BlockSpec((tr, d), lambda i: (i, 0))] * 2,
        out_specs=[pl.BlockSpec((tr, d), lambda i: (i, 0)), pl.BlockSpec((SUBLANES, LANES), lambda i: (0, 0))],
        compiler_params=_params("arbitrary"),
    )(y, target)
    return dy, l[0, 0]


def _hilo_dot(a, u):
    hi = a.astype(bf16)
    lo = (a - hi.astype(f32)).astype(bf16)
    return (lax.dot_general(hi, u, NN, preferred_element_type=f32)
            + lax.dot_general(lo, u, NN, preferred_element_type=f32))


def _sb_scores(q_t, k_s, mask):
    z = lax.dot_general(q_t, k_s, NT, preferred_element_type=f32)
    sp = jnp.log(1.0 + jnp.exp(-jnp.abs(z)))
    a = -(jnp.maximum(z, 0.0) + sp)
    b = a + z
    if mask is not None:
        a = jnp.where(mask, a, 0.0)
    return a, b


SB_QROWS = 512
SB_DIAG = SB_QROWS // SB_BLOCK


def _sb_masks():
    row = lax.broadcasted_iota(jnp.int32, (SB_QROWS, SB_BLOCK), 0)
    col = lax.broadcasted_iota(jnp.int32, (SB_QROWS, SB_BLOCK), 1)
    tri = [d * SB_BLOCK + col < row for d in range(SB_DIAG)]
    r = lax.broadcasted_iota(jnp.int32, (SB_BLOCK, SB_BLOCK), 0)
    c = lax.broadcasted_iota(jnp.int32, (SB_BLOCK, SB_BLOCK), 1)
    u_excl = (r > c).astype(bf16)
    u_incl = (r >= c).astype(bf16)
    return tri, u_excl, u_incl


def _sb_walk(block, t, init):
    carry = init
    for d in reversed(range(SB_DIAG)):
        carry = block(t * SB_DIAG + d, carry, d)
    return lax.fori_loop(0, t * SB_DIAG, lambda i, c: block(t * SB_DIAG - 1 - i, c, None), carry)


HEADS_PER_STEP = 2


def _head_lanes(hd):
    return slice(hd * HEAD_DIM, (hd + 1) * HEAD_DIM)


def _head_cols(s):
    width = HEADS_PER_STEP * HEAD_DIM
    return lambda off: pl.BlockSpec((s, width), lambda h, off=off: (0, off // HEADS_PER_STEP + h))


def _sb_fwd(proj, comm=None):
    s = proj.shape[0]
    nblk = s // SB_BLOCK
    scale = HEAD_DIM ** -0.5

    def body(q_ref, k_ref, v_ref, o_ref, qs, ks, vs):
        qs[...] = (q_ref[...] * scale).astype(bf16)
        ks[...] = k_ref[...].astype(bf16)
        vs[...] = v_ref[...].astype(bf16)
        tri, u_excl, _ = _sb_masks()

        def block(q_t, s_idx, carry, d):
            rows = pl.ds(pl.multiple_of(s_idx * SB_BLOCK, SB_BLOCK), SB_BLOCK)
            mask = None if d is None else tri[d]
            out = []
            for hd, (ra, acc) in enumerate(carry):
                a, b = _sb_scores(q_t[hd], ks[rows, _head_lanes(hd)], mask)
                w = jnp.exp(b + _hilo_dot(a, u_excl) + ra)
                if mask is not None:
                    w = jnp.where(mask, w, 0.0)
                acc = acc + lax.dot_general(w.astype(bf16), vs[rows, _head_lanes(hd)], NN, preferred_element_type=f32)
                out.append((ra + jnp.sum(a, axis=1, keepdims=True), acc))
            return tuple(out)

        def query_tile(t, _):
            rows = pl.ds(pl.multiple_of(t * SB_QROWS, SB_QROWS), SB_QROWS)
            q_t = [qs[rows, _head_lanes(hd)] for hd in range(HEADS_PER_STEP)]
            init = ((jnp.zeros((SB_QROWS, 1), f32), jnp.zeros((SB_QROWS, HEAD_DIM), f32)),) * HEADS_PER_STEP
            carry = _sb_walk(functools.partial(block, q_t), t, init)
            for hd in range(HEADS_PER_STEP):
                o_ref[rows, _head_lanes(hd)] = carry[hd][1]
            return 0

        lax.fori_loop(0, s // SB_QROWS, query_tile, 0)

    col = _head_cols(s)
    (o,), sent = _grid_call(
        body, "sb_fwd", HEADS // HEADS_PER_STEP, [jax.ShapeDtypeStruct((s, HEADS * HEAD_DIM), f32)],
        [col(0), col(HEADS), col(2 * HEADS)], [col(0)], [pltpu.VMEM((s, HEADS_PER_STEP * HEAD_DIM), bf16)] * 3,
        (proj, proj, proj), comm)
    return o, sent


def _sb_bwd(proj, o, do, comm=None):
    s = proj.shape[0]
    nblk = s // SB_BLOCK
    scale = HEAD_DIM ** -0.5

    def body(q_ref, k_ref, v_ref, o_ref, do_ref, dq_ref, dk_ref, dv_ref, qs, ks, vs, dos, dk_acc, dv_acc):
        qs[...] = (q_ref[...] * scale).astype(bf16)
        ks[...] = k_ref[...].astype(bf16)
        vs[...] = v_ref[...].astype(bf16)
        dos[...] = do_ref[...].astype(bf16)
        dk_acc[...] = jnp.zeros_like(dk_acc)
        dv_acc[...] = jnp.zeros_like(dv_acc)
        tri, u_excl, u_incl = _sb_masks()

        def block(q_t, do_t, d_t, s_idx, carry, d):
            rows = pl.ds(pl.multiple_of(s_idx * SB_BLOCK, SB_BLOCK), SB_BLOCK)
            mask = None if d is None else tri[d]
            out = []
            for hd, (ra, rg, dq) in enumerate(carry):
                lanes = _head_lanes(hd)
                k_s = ks[rows, lanes]
                v_s = vs[rows, lanes]
                a, b = _sb_scores(q_t[hd], k_s, mask)
                w = jnp.exp(b + _hilo_dot(a, u_excl) + ra)
                if mask is not None:
                    w = jnp.where(mask, w, 0.0)
                dw = lax.dot_general(do_t[hd], v_s, NT, preferred_element_type=f32)
                wb = w.astype(bf16)
                g = wb.astype(f32) * dw
                da = d_t[hd] - (_hilo_dot(g, u_incl) + rg)
                dz = g * jnp.exp(a) - da * jnp.exp(b)
                if mask is not None:
                    dz = jnp.where(mask, dz, 0.0)
                dzb = dz.astype(bf16)
                dq = dq + lax.dot_general(dzb, k_s, NN, preferred_element_type=f32)
                dk_acc[rows, lanes] += lax.dot_general(dzb, q_t[hd], TN, preferred_element_type=f32)
                dv_acc[rows, lanes] += lax.dot_general(wb, do_t[hd], TN, preferred_element_type=f32)
                out.append((ra + jnp.sum(a, axis=1, keepdims=True), rg + jnp.sum(g, axis=1, keepdims=True), dq))
            return tuple(out)

        def query_tile(t, _):
            rows = pl.ds(pl.multiple_of(t * SB_QROWS, SB_QROWS), SB_QROWS)
            heads = range(HEADS_PER_STEP)
            q_t = [qs[rows, _head_lanes(hd)] for hd in heads]
            do_t = [dos[rows, _head_lanes(hd)] for hd in heads]
            d_t = [jnp.sum(do_t[hd].astype(f32) * o_ref[rows, _head_lanes(hd)], axis=1, keepdims=True) for hd in heads]
            zero = jnp.zeros((SB_QROWS, 1), f32)
            init = ((zero, zero, jnp.zeros((SB_QROWS, HEAD_DIM), f32)),) * HEADS_PER_STEP
            carry = _sb_walk(functools.partial(block, q_t, do_t, d_t), t, init)
            for hd in heads:
                dq_ref[rows, _head_lanes(hd)] = (carry[hd][2] * scale).astype(dq_ref.dtype)
            return 0

        lax.fori_loop(0, s // SB_QROWS, query_tile, 0)
        dk_ref[...] = dk_acc[...].astype(dk_ref.dtype)
        dv_ref[...] = dv_acc[...].astype(dv_ref.dtype)

    col = _head_cols(s)
    out = jax.ShapeDtypeStruct((s, HEADS * HEAD_DIM), bf16)
    width = HEADS_PER_STEP * HEAD_DIM
    return _grid_call(
        body, "sb_bwd", HEADS // HEADS_PER_STEP, [out, out, out], [col(0), col(HEADS), col(2 * HEADS), col(0), col(0)],
        [col(0), col(0), col(0)], [pltpu.VMEM((s, width), bf16)] * 4 + [pltpu.VMEM((s, width), f32)] * 2,
        (proj, proj, proj, o, do), comm)


def _tri_dot(x, lower):
    n = x.shape[0]
    r = lax.broadcasted_iota(jnp.int32, (n, n), 0)
    c = lax.broadcasted_iota(jnp.int32, (n, n), 1)
    tri = ((c <= r) if lower else (c >= r)).astype(bf16)
    hi = x.astype(bf16)
    rest = x - hi.astype(f32)
    mid = rest.astype(bf16)
    lo = (rest - mid.astype(f32)).astype(bf16)
    return sum(lax.dot_general(tri, part, NN, preferred_element_type=f32) for part in (hi, mid, lo))


@jax.custom_vjp
def _cumsum_rows(x):
    return _tri_dot(x, True)


_cumsum_rows.defvjp(lambda x: (_tri_dot(x, True), None), lambda _, ct: (_tri_dot(ct, False),))


@functools.partial(jax.custom_vjp, nondiff_argnums=(2,))
def _bf16_dot(a, b, dims):
    return lax.dot_general(a.astype(bf16), b.astype(bf16), dims, preferred_element_type=f32)


def _bf16_dot_bwd(dims, res, ct):
    a, b, ct = (v.astype(bf16) for v in (*res, ct))
    dot = lambda u, v, d: lax.dot_general(u, v, d, preferred_element_type=f32)
    if dims == NT:
        return dot(ct, b, NN), dot(ct, a, TN)
    return dot(b, ct, NT), dot(a, ct, NN)


_bf16_dot.defvjp(lambda a, b, dims: (_bf16_dot(a, b, dims), (a, b)), _bf16_dot_bwd)


def _hg_sub(state_t, hq, hf, hi, lb):
    n = HG_SUB
    f = lb + (1.0 - lb) * _sigmoid(hf)
    q = _silu(hq)
    k = 1.0 - f
    lf = jnp.log(f)
    cum = _cumsum_rows(lf)
    o = _bf16_dot(q * jnp.exp(cum), state_t, NT)
    causal = lax.broadcasted_iota(jnp.int32, (n, n, 1), 1) >= lax.broadcasted_iota(jnp.int32, (n, n, 1), 0)
    dec = jnp.exp(jnp.where(causal, cum[None] - cum[:, None, :], 0.0))
    score = jnp.sum(jnp.where(causal, q[None] * k[:, None, :] * dec, 0.0), axis=2, keepdims=True)
    o = o + jnp.sum(score * hi[:, None, :], axis=0)
    last = jnp.sum(lf, axis=0, keepdims=True)
    k_dec = k * jnp.exp(last - cum)
    new_state = state_t * jnp.exp(last) + _bf16_dot(hi, k_dec, TN)
    return new_state, o


def _hg_chunk(state_t, hq, hf, hi, hg, l0, l1, ng):
    lb = _sigmoid(l0 - l1)
    outs = []
    for j in range(len(hq)):
        state_t, o = _hg_sub(state_t, hq[j], hf[j], hi[j], lb)
        outs.append(_rms(o, ng) * _silu(hg[j]))
    return state_t, outs


def _hg_rows(c, j):
    return pl.ds(pl.multiple_of(c * HG_CHUNK + j * HG_SUB, HG_SUB), HG_SUB)


def _hg_pieces(ref, c, hd):
    return [ref[_hg_rows(c, j), _head_lanes(hd)] for j in range(HG_CHUNK // HG_SUB)]


def _hg_specs(s):
    lrow = pl.BlockSpec((1, HEADS_PER_STEP * HEAD_DIM), lambda h: (0, h))
    ng = pl.BlockSpec((1, HEAD_DIM), lambda h: (0, 0))
    return _head_cols(s), lrow, ng


def _hg_fwd(proj, logits, ng, comm=None):
    s = proj.shape[0]
    nch = s // HG_CHUNK
    n = HG_CHUNK // HG_SUB

    def body(hq_ref, hf_ref, hi_ref, hg_ref, l0_ref, l1_ref, ng_ref, o_ref, st_ref):
        def chunk(c, states):
            new = []
            for hd, state_t in enumerate(states):
                st_ref[hd, c] = state_t
                state_t, outs = _hg_chunk(
                    state_t, _hg_pieces(hq_ref, c, hd), _hg_pieces(hf_ref, c, hd), _hg_pieces(hi_ref, c, hd),
                    _hg_pieces(hg_ref, c, hd), l0_ref[:, _head_lanes(hd)], l1_ref[:, _head_lanes(hd)], ng_ref[...])
                for j in range(n):
                    o_ref[_hg_rows(c, j), _head_lanes(hd)] = outs[j]
                new.append(state_t)
            return tuple(new)

        lax.fori_loop(0, nch, chunk, (jnp.zeros((HEAD_DIM, HEAD_DIM), f32),) * HEADS_PER_STEP)

    col, lrow, ngs = _hg_specs(s)
    (o, states), sent = _grid_call(
        body, "hg_fwd", HEADS // HEADS_PER_STEP,
        [jax.ShapeDtypeStruct((s, HEADS * HEAD_DIM), f32), jax.ShapeDtypeStruct((HEADS, nch, HEAD_DIM, HEAD_DIM), f32)],
        [col(3 * HEADS), col(4 * HEADS), col(5 * HEADS), col(6 * HEADS), lrow, lrow, ngs],
        [col(0), pl.BlockSpec((HEADS_PER_STEP, nch, HEAD_DIM, HEAD_DIM), lambda h: (h, 0, 0, 0))], [],
        (proj, proj, proj, proj, logits[0:1], logits[1:2], ng), comm)
    return o, states, sent


def _hg_bwd(proj, logits, ng, states, do, do_off, after=None):
    s = proj.shape[0]
    nch = s // HG_CHUNK
    n = HG_CHUNK // HG_SUB

    def body(hq_ref, hf_ref, hi_ref, hg_ref, l0_ref, l1_ref, ng_ref, st_ref, do_ref, *rest):
        dq_ref, df_ref, di_ref, dg_ref, dl0_ref, dl1_ref, dng_ref = rest[-7:]
        dl0_ref[...] = jnp.zeros_like(dl0_ref)
        dl1_ref[...] = jnp.zeros_like(dl1_ref)
        dng_ref[...] = jnp.zeros_like(dng_ref)

        def chunk(i, dstates):
            c = nch - 1 - i
            new = []
            for hd, dstate in enumerate(dstates):
                lanes = _head_lanes(hd)
                args = (st_ref[hd, c], _hg_pieces(hq_ref, c, hd), _hg_pieces(hf_ref, c, hd), _hg_pieces(hi_ref, c, hd),
                        _hg_pieces(hg_ref, c, hd), l0_ref[:, lanes], l1_ref[:, lanes], ng_ref[...])
                _, vjp = jax.vjp(_hg_chunk, *args)
                dst, dq, df, di, dg, dl0, dl1, dng = vjp((dstate, _hg_pieces(do_ref, c, hd)))
                for j in range(n):
                    rows = _hg_rows(c, j)
                    dq_ref[rows, lanes] = dq[j].astype(dq_ref.dtype)
                    df_ref[rows, lanes] = df[j].astype(df_ref.dtype)
                    di_ref[rows, lanes] = di[j].astype(di_ref.dtype)
                    dg_ref[rows, lanes] = dg[j].astype(dg_ref.dtype)
                dl0_ref[:, lanes] += dl0
                dl1_ref[:, lanes] += dl1
                dng_ref[:, lanes] += dng
                new.append(dst)
            return tuple(new)

        lax.fori_loop(0, nch, chunk, (jnp.zeros((HEAD_DIM, HEAD_DIM), f32),) * HEADS_PER_STEP)

    col, lrow, ngs = _hg_specs(s)
    big = jax.ShapeDtypeStruct((s, HEADS * HEAD_DIM), bf16)
    vec = jax.ShapeDtypeStruct((1, HEADS * HEAD_DIM), f32)
    order = [] if after is None else [after]
    (dq, df, di, dg, dl0, dl1, dng), sent = _grid_call(
        body, "hg_bwd", HEADS // HEADS_PER_STEP, [big, big, big, big, vec, vec, vec],
        [col(3 * HEADS), col(4 * HEADS), col(5 * HEADS), col(6 * HEADS), lrow, lrow, ngs,
         pl.BlockSpec((HEADS_PER_STEP, nch, HEAD_DIM, HEAD_DIM), lambda h: (h, 0, 0, 0)), col(do_off)]
        + [pl.BlockSpec(memory_space=pl.ANY)] * len(order),
        [col(0), col(0), col(0), col(0), lrow, lrow, lrow], [],
        (proj, proj, proj, proj, logits[0:1], logits[1:2], ng, states, do, *order))
    return (jnp.concatenate([dq, df, di, dg], axis=1), jnp.concatenate([dl0, dl1], axis=0),
            dng.reshape(HEADS, HEAD_DIM), sent)


def _shift_rows(x, k, down):
    if k == 0:
        return x
    s = x.shape[0]
    row = lax.broadcasted_iota(jnp.int32, x.shape, 0)
    if down:
        return jnp.where(row < k, 0.0, pltpu.roll(x, k, 0))
    return jnp.where(row >= s - k, 0.0, pltpu.roll(x, s - k, 0))


def _conv_fwd(proj, w, b):
    s, w2 = proj.shape
    width = w2 // 2
    nb = width // RG_BLOCK

    def body(x_ref, w_ref, b_ref, y_ref, yb_ref):
        x = x_ref[...]
        y = jnp.broadcast_to(b_ref[...], x.shape)
        for k in range(CONV_WIDTH):
            y = y + _shift_rows(x, k, True) * w_ref[pl.ds(k, 1), :]
        y_ref[...] = y
        yb_ref[...] = y.astype(bf16)

    blk = pl.BlockSpec((s, RG_BLOCK), lambda j: (0, j))
    return pl.pallas_call(
        body, name="conv_fwd", grid=(nb,),
        out_shape=[jax.ShapeDtypeStruct((s, width), f32), jax.ShapeDtypeStruct((s, width), bf16)],
        in_specs=[pl.BlockSpec((s, RG_BLOCK), lambda j: (0, nb + j)),
                  pl.BlockSpec((CONV_WIDTH, RG_BLOCK), lambda j: (0, j)), pl.BlockSpec((1, RG_BLOCK), lambda j: (0, j))],
        out_specs=[blk, blk],
        compiler_params=_params("parallel"),
    )(proj, w, b)


def _conv_bwd(proj, w, dy_a, dy_b):
    s, w2 = proj.shape
    width = w2 // 2
    nb = width // RG_BLOCK

    def body(x_ref, w_ref, dya_ref, dyb_ref, dx_ref, dw_ref, db_ref):
        x = x_ref[...]
        dy_v = dya_ref[...] + dyb_ref[...]
        dx = jnp.zeros_like(x)
        for k in range(CONV_WIDTH):
            dx = dx + _shift_rows(dy_v, k, False) * w_ref[pl.ds(k, 1), :]
            dw_ref[pl.ds(k, 1), :] = jnp.sum(dy_v * _shift_rows(x, k, True), axis=0, keepdims=True)
        dx_ref[...] = dx.astype(dx_ref.dtype)
        db_ref[...] = jnp.sum(dy_v, axis=0, keepdims=True)

    blk = pl.BlockSpec((s, RG_BLOCK), lambda j: (0, j))
    wblk = pl.BlockSpec((CONV_WIDTH, RG_BLOCK), lambda j: (0, j))
    bblk = pl.BlockSpec((1, RG_BLOCK), lambda j: (0, j))
    return pl.pallas_call(
        body, name="conv_bwd", grid=(nb,),
        out_shape=[jax.ShapeDtypeStruct((s, width), bf16), jax.ShapeDtypeStruct((CONV_WIDTH, width), f32),
                   jax.ShapeDtypeStruct((1, width), f32)],
        in_specs=[pl.BlockSpec((s, RG_BLOCK), lambda j: (0, nb + j)), wblk, blk, blk],
        out_specs=[blk, wblk, bblk],
        compiler_params=_params("parallel"),
    )(proj, w, dy_a, dy_b)


def _rg_rows(w_ref):
    return jnp.concatenate([w_ref[q] for q in range(N_CHIPS)], axis=0)


def _bd_fwd(xb, wa, wx):
    s, width = xb.shape
    nb = width // RG_BLOCK

    def body(x_ref, wa_ref, wx_ref, ra_ref, ix_ref):
        x = x_ref[...]
        ra_ref[...] = lax.dot_general(x, _rg_rows(wa_ref), NN, preferred_element_type=f32)
        ix_ref[...] = lax.dot_general(x, _rg_rows(wx_ref), NN, preferred_element_type=f32)

    blk = pl.BlockSpec((s, RG_BLOCK), lambda j: (0, j))
    wblk = pl.BlockSpec((N_CHIPS, None, RG_BLOCK // N_CHIPS, RG_BLOCK), lambda j: (0, j, 0, 0))
    out = jax.ShapeDtypeStruct((s, width), f32)
    return pl.pallas_call(
        body, name="bd_fwd", grid=(nb,), out_shape=[out, out],
        in_specs=[blk, wblk, wblk], out_specs=[blk, blk],
        compiler_params=_params("parallel"),
    )(xb, wa, wx)


def _bd_bwd(xb, wa, wx, dra, dix):
    s, width = xb.shape
    nb = width // RG_BLOCK
    rq = RG_BLOCK // N_CHIPS

    def body(x_ref, wa_ref, wx_ref, dra_ref, dix_ref, dx_ref, dwa_ref, dwx_ref):
        x = x_ref[...]
        dra_v = dra_ref[...]
        dix_v = dix_ref[...]
        dx_ref[...] = (lax.dot_general(dra_v, _rg_rows(wa_ref), NT, preferred_element_type=f32)
                       + lax.dot_general(dix_v, _rg_rows(wx_ref), NT, preferred_element_type=f32))
        dwa = lax.dot_general(x, dra_v, TN, preferred_element_type=f32)
        dwx = lax.dot_general(x, dix_v, TN, preferred_element_type=f32)
        for q in range(N_CHIPS):
            dwa_ref[q] = dwa[q * rq:(q + 1) * rq, :].astype(dwa_ref.dtype)
            dwx_ref[q] = dwx[q * rq:(q + 1) * rq, :].astype(dwx_ref.dtype)

    blk = pl.BlockSpec((s, RG_BLOCK), lambda j: (0, j))
    wblk = pl.BlockSpec((N_CHIPS, None, rq, RG_BLOCK), lambda j: (0, j, 0, 0))
    wout = jax.ShapeDtypeStruct((N_CHIPS, nb, rq, RG_BLOCK), bf16)
    return pl.pallas_call(
        body, name="bd_bwd", grid=(nb,),
        out_shape=[jax.ShapeDtypeStruct((s, width), f32), wout, wout],
        in_specs=[blk, wblk, wblk, blk, blk], out_specs=[blk, wblk, wblk],
        compiler_params=_params("parallel"),
    )(xb, wa, wx, dra, dix)


def _scan_tiles(a_ref, u_ref, out_ref, n_tiles, reverse):
    cb = a_ref.shape[1]
    row = lax.broadcasted_iota(jnp.int32, (SUBLANES, cb), 0)

    def tile(i, carry):
        t = (n_tiles - 1 - i) if reverse else i
        rows = pl.ds(pl.multiple_of(t * SUBLANES, SUBLANES), SUBLANES)
        a = a_ref[rows, :]
        b = u_ref[rows, :]
        for d in (1, 2, 4):
            if reverse:
                valid = row < SUBLANES - d
                shift = SUBLANES - d
            else:
                valid = row >= d
                shift = d
            a_n = jnp.where(valid, pltpu.roll(a, shift, 0), 1.0)
            b_n = jnp.where(valid, pltpu.roll(b, shift, 0), 0.0)
            b = a * b_n + b
            a = a * a_n
        out = a * carry + b
        out_ref[rows, :] = out
        edge = t * SUBLANES + (0 if reverse else SUBLANES - 1)
        return jnp.broadcast_to(out_ref[pl.ds(edge, 1), :], (SUBLANES, cb))

    lax.fori_loop(0, n_tiles, tile, jnp.zeros((SUBLANES, cb), f32))


def _scan_fwd(a, u):
    s, width = a.shape

    def body(a_ref, u_ref, h_ref):
        _scan_tiles(a_ref, u_ref, h_ref, s // SUBLANES, False)

    blk = pl.BlockSpec((s, RG_BLOCK), lambda j: (0, j))
    return pl.pallas_call(
        body, name="scan_fwd", grid=(width // RG_BLOCK,), out_shape=jax.ShapeDtypeStruct((s, width), f32),
        in_specs=[blk, blk], out_specs=blk, compiler_params=_params("parallel"),
    )(a, u)


def _scan_bwd(a, h, dh):
    s, width = a.shape

    def body(a_ref, h_ref, dh_ref, du_ref, da_ref, a_next):
        a_next[...] = _shift_rows(a_ref[...], 1, False)
        _scan_tiles(a_next, dh_ref, du_ref, s // SUBLANES, True)
        da_ref[...] = du_ref[...] * _shift_rows(h_ref[...], 1, True)

    blk = pl.BlockSpec((s, RG_BLOCK), lambda j: (0, j))
    out = jax.ShapeDtypeStruct((s, width), f32)
    return pl.pallas_call(
        body, name="scan_bwd", grid=(width // RG_BLOCK,), out_shape=[out, out],
        in_specs=[blk, blk, blk], out_specs=[blk, blk],
        scratch_shapes=[pltpu.VMEM((s, RG_BLOCK), f32)],
        compiler_params=_params("parallel"),
    )(a, h, dh)


EW_TILE_BYTES = 2 << 20


def _ew_rows(rows, width):
    limit = max(16, EW_TILE_BYTES // (4 * width))
    t = (min(limit, rows) // 16) * 16
    while t >= 16:
        if rows % t == 0:
            return t
        t -= 16
    return rows


def _as2d(a):
    return a.reshape(-1, a.shape[-1])


def _ew(fn, ins, out_dtypes, name):
    shape = ins[0].shape
    ins2 = [_as2d(a) for a in ins]
    rows, width = ins2[0].shape
    res = _rowwise(fn, ins2, [], [(width, dt) for dt in out_dtypes], name, _ew_rows(rows, width))
    return [r.reshape(shape) for r in res]


def _f_adamw(w, g, m, v):
    m = ADAM_B1 * m + (1.0 - ADAM_B1) * g
    v = ADAM_B2 * v + (1.0 - ADAM_B2) * (g * g)
    m_hat = m / (1.0 - ADAM_B1 ** ADAM_STEP)
    v_hat = v / (1.0 - ADAM_B2 ** ADAM_STEP)
    delta = -ADAM_LR * (m_hat / (jnp.sqrt(v_hat) + ADAM_EPS) + ADAM_WD * w)
    return delta, m, v


ANY = pl.BlockSpec(memory_space=pl.ANY)


def _place():
    x, y, c = lax.axis_index("x"), lax.axis_index("y"), lax.axis_index("c")
    return x, y, c, [(1 - x, y), (x, 1 - y), (1 - x, 1 - y)]


def _rcopy(src, dst, send_sem, recv_sem, dev):
    return pltpu.make_async_remote_copy(src_ref=src, dst_ref=dst, send_sem=send_sem, recv_sem=recv_sem,
                                        device_id=dev, device_id_type=MESH)


def _half(ref, h, kind, lead=0):
    ax = lead + (1 if kind == "rg" else 0)
    n = ref.shape[ax] // 2
    idx = [slice(None)] * ref.ndim
    idx[ax] = pl.ds(h * n, n)
    return ref.at[tuple(idx)]


def _gather_over_ici(kinds):
    def make(out, send_sems, recv_sems):
        x, y, c, chips = _place()
        return [_rcopy(mine, mine, send_sems.at[3 * e + j], recv_sems.at[3 * e + j], (*chip, c))
                for e in range(len(kinds)) for j, chip in enumerate(chips)
                for mine in [_half(out[e].at[2 * x + y], c, kinds[e])]]

    return make


def _gather_hand_on_copies(kinds):
    def make(out, send_sems, recv_sems):
        x, y, c, chips = _place()
        return [_rcopy(part, part, send_sems.at[3 * e + j], recv_sems.at[3 * e + j], (x, y, 1 - c))
                for e in range(len(kinds)) for j, chip in enumerate(chips)
                for part in [_half(out[e].at[2 * chip[0] + chip[1]], c, kinds[e])]]

    return make


def _gather_hand_on(slots, kinds):
    n = len(slots)

    def copies(out, send_sems, recv_sems, which):
        x, y, c, chips = _place()
        h = c if which == "landed" else 1 - c
        return [_rcopy(part, part, send_sems.at[3 * e + j], recv_sems.at[3 * e + j], (x, y, 1 - c))
                for e in range(n) for j, chip in enumerate(chips)
                for part in [_half(out[e].at[2 * chip[0] + chip[1]], h, kinds[e])]]

    def start(ins, out, send_sems, recv_sems):
        for cp in copies(out, send_sems, recv_sems, "landed"):
            cp.start()

    def finish(ins, out, send_sems, recv_sems):
        for cp in copies(out, send_sems, recv_sems, "handed"):
            cp.wait_recv()
        for cp in copies(out, send_sems, recv_sems, "landed"):
            cp.wait_send()

    return _Comm(slots, [jax.ShapeDtypeStruct(s.shape, s.dtype) for s in slots], {e: e for e in range(n)}, 3 * n,
                 start, finish)


def _exchange_halves(grads, kinds):
    n = len(grads)

    def copies(g, theirs, send_sems, recv_sems):
        x, y, c, _ = _place()
        return [_rcopy(_half(g[e], 1 - c, kinds[e], 1), theirs[e], send_sems.at[e], recv_sems.at[e], (x, y, 1 - c))
                for e in range(n)]

    def start(*refs):
        for cp in copies(*refs):
            cp.start()

    def finish(*refs):
        for cp in copies(*refs):
            cp.wait()

    return _Comm(grads, [jax.ShapeDtypeStruct(_half_shape(s, k), s.dtype) for s, k in zip(grads, kinds)], {}, n,
                 start, finish)


def _half_shape(s, kind):
    shp = list(s.shape)
    shp[2 if kind == "rg" else 1] //= 2
    return tuple(shp)


def _exchange_halves_copies(grads, kinds):
    n = len(grads)

    def make(refs, send_sems, recv_sems):
        g, theirs = refs[:n], refs[n:]
        x, y, c, _ = _place()
        return [_rcopy(_half(g[e], 1 - c, kinds[e], 1), theirs[e], send_sems.at[e], recv_sems.at[e], (x, y, 1 - c))
                for e in range(n)]

    landing = [lax.empty(_half_shape(s, k), s.dtype) for s, k in zip(grads, kinds)]
    return make, list(grads) + landing, n


def _exchange_chips(parts):
    n = len(parts)

    def make(refs, send_sems, recv_sems):
        p, got = refs[:n], refs[n:]
        x, y, c, chips = _place()
        return [_rcopy(p[e].at[2 * chip[0] + chip[1]], got[e].at[j], send_sems.at[3 * e + j], recv_sems.at[3 * e + j],
                       (*chip, c))
                for e in range(n) for j, chip in enumerate(chips)]

    landing = [lax.empty((N_CHIPS - 1,) + s.shape[1:], s.dtype) for s in parts]
    return make, list(parts) + landing, 3 * n


def _share_halves(shards, kinds):
    n = len(shards)

    def copies(out, send_sems, recv_sems, h):
        x, y, c, _ = _place()
        return [_rcopy(part, part, send_sems.at[e], recv_sems.at[e], (x, y, 1 - c))
                for e in range(n) for part in [_half(out[e], c if h == "mine" else 1 - c, kinds[e])]]

    def start(ins, out, send_sems, recv_sems):
        for cp in copies(out, send_sems, recv_sems, "mine"):
            cp.start()

    def finish(ins, out, send_sems, recv_sems):
        for cp in copies(out, send_sems, recv_sems, "other"):
            cp.wait_recv()
        for cp in copies(out, send_sems, recv_sems, "mine"):
            cp.wait_send()

    return _Comm(shards, [jax.ShapeDtypeStruct(s.shape, s.dtype) for s in shards], {e: e for e in range(n)}, n,
                 start, finish)


def _tiles_call(fn, scalars, ins, outs, steps, name):
    n_in = len(ins)

    def body(s_ref, *refs):
        res = fn(*[r[...] for r in refs[:n_in]])
        for r, o in zip(res, refs[n_in:]):
            o[...] = r.astype(o.dtype)

    spec = lambda blk, idx: pl.BlockSpec(blk, lambda i, s, idx=idx: (idx(i, s), 0))
    return pl.pallas_call(
        body, name=name, out_shape=[o[0] for o in outs],
        grid_spec=pltpu.PrefetchScalarGridSpec(
            num_scalar_prefetch=1, grid=(steps,),
            in_specs=[spec(blk, idx) for _, blk, idx in ins], out_specs=[spec(blk, idx) for _, blk, idx in outs]),
        compiler_params=_params("arbitrary"),
    )(scalars, *[a for a, _, _ in ins])


def _groups(kind, shard_shape):
    if kind == "rg":
        return shard_shape[0], shard_shape[1], shard_shape[2]
    return 1, shard_shape[0], shard_shape[1]


def _half_tiles(kind, shard_shape):
    g, rg, width = _groups(kind, shard_shape)
    tr = _ew_rows(rg // 2, width)
    nt = (rg // 2) // tr
    in_whole = lambda i, c: (i // nt) * (rg // tr) + c * nt + i % nt
    return g, rg, width, tr, nt, in_whole


def _cast_into_slot(w, layer, kind, scalars):
    shard_shape = w.shape[1:]
    g, rg, width, tr, nt, _ = _half_tiles(kind, shard_shape)
    per_shard = g * rg // tr
    (buf,) = _tiles_call(
        lambda v: (v,), scalars, [(_as2d(w), (tr, width), lambda i, s: layer * per_shard + i)],
        [(jax.ShapeDtypeStruct((N_CHIPS * g * rg, width), bf16), (tr, width), lambda i, s: s[0] * per_shard + i)],
        per_shard, "cast_" + kind)
    return buf.reshape((N_CHIPS,) + shard_shape)


def _add_cores(grad, theirs, kind, scalars):
    g, rg, width, tr, nt, in_whole = _half_tiles(kind, grad.shape[1:])
    steps = N_CHIPS * g * nt
    (out,) = _tiles_call(
        lambda u, v: (u.astype(f32) + v.astype(f32),), scalars,
        [(_as2d(grad), (tr, width), lambda i, s: in_whole(i, s[1])), (_as2d(theirs), (tr, width), lambda i, s: i)],
        [(jax.ShapeDtypeStruct((steps * tr, width), bf16), (tr, width), lambda i, s: i)], steps, "rs_add_cores")
    return out.reshape(theirs.shape)


def _sum_chips(part, got, kind, scalars):
    shard_shape = list(part.shape[1:])
    shard_shape[1 if kind == "rg" else 0] *= 2
    g, rg, width, tr, nt, in_whole = _half_tiles(kind, shard_shape)
    steps = g * nt
    up = lambda t: t.astype(f32)
    g2 = _as2d(got)
    ins = [(_as2d(part), (tr, width), lambda i, s: s[0] * steps + i)]
    ins += [(g2, (tr, width), lambda i, s, j=j: j * steps + i) for j in range(N_CHIPS - 1)]
    (out,) = _tiles_call(
        lambda a, b, c, e: (((up(a) + up(b)) + up(c)) + up(e),), scalars, ins,
        [(jax.ShapeDtypeStruct((g * rg, width), f32), (tr, width), lambda i, s: in_whole(i, s[1]))], steps, "rs_add_chips")
    return out.reshape(shard_shape)


def _all_gather_rows(v, reduce):
    rows = v.shape[0]

    def body(v_ref, out_ref, *rest):
        if reduce:
            sum_ref, send_sems, recv_sems, local_sem = rest
        else:
            send_sems, recv_sems, local_sem = rest
        x, y, c, chips = _place()
        me, sibling = (x, y, c), (x, y, 1 - c)

        def blk(px, py, pc):
            return out_ref.at[pl.ds((4 * px + 2 * py + pc) * rows, rows), :]

        def copy(k, block, to, src=None):
            return _rcopy(blk(*block) if src is None else src, blk(*block), send_sems.at[k], recv_sems.at[k], to)

        mine = pltpu.make_async_copy(v_ref, blk(*me), local_sem)
        mine.start()
        first = [copy(0, me, sibling, src=v_ref)] + [copy(1 + j, me, (*chip, c), src=v_ref) for j, chip in enumerate(chips)]
        for cp in first:
            cp.start()
        passed = [copy(4 + j, (*chip, c), sibling) for j, chip in enumerate(chips)]
        for j, chip in enumerate(chips):
            copy(1 + j, (*chip, c), me).wait_recv()
            passed[j].start()
        copy(0, sibling, me).wait_recv()
        for j, chip in enumerate(chips):
            copy(4 + j, (*chip, 1 - c), me).wait_recv()
        for cp in first + passed:
            cp.wait_send()
        mine.wait()
        if reduce:
            acc = out_ref[pl.ds(0, rows), :]
            for d in range(1, 8):
                acc = acc + out_ref[pl.ds(d * rows, rows), :]
            sum_ref[...] = acc

    vm = pl.BlockSpec(memory_space=pltpu.VMEM)
    gathered = jax.ShapeDtypeStruct((8 * rows, LANES), f32)
    res = pl.pallas_call(
        body, name="all_reduce_small" if reduce else "all_gather_small",
        out_shape=[gathered, jax.ShapeDtypeStruct((rows, LANES), f32)] if reduce else [gathered],
        in_specs=[vm], out_specs=[vm, vm] if reduce else [vm],
        scratch_shapes=[pltpu.SemaphoreType.DMA((7,)), pltpu.SemaphoreType.DMA((7,)), pltpu.SemaphoreType.DMA],
    )(v)
    return res[1] if reduce else res[0]


def _pack(parts):
    flat = jnp.concatenate([p.reshape(-1).astype(f32) for p in parts])
    rows = -(-flat.shape[0] // (SUBLANES * LANES)) * SUBLANES
    return jnp.pad(flat, (0, rows * LANES - flat.shape[0])).reshape(rows, LANES)


def _unpack(packed, shapes):
    flat = packed.reshape(-1)
    out, off = [], 0
    for shp in shapes:
        size = 1
        for d in shp:
            size *= d
        out.append(flat[off:off + size].reshape(shp))
        off += size
    return out


WEIGHTS = ["mix_pre_g", "mix_post_g", "ffn_pre_g", "ffn_post_g", "ple_norm_g", "w_in_even", "w_out_even",
           "hg_lb_logits", "hg_norm_g", "w_in_odd", "conv_w", "conv_b", "rg_wa", "rg_ba", "rg_wx", "rg_bx",
           "rg_lambda", "w_out_odd", "w_gate_up", "w_down", "w_ple_up", "w_ple_gate"]
BIG = [("w_in_even", 0, "col"), ("w_out_even", 0, "row"), ("w_gate_up", 0, "col"), ("w_down", 0, "row"),
       ("w_ple_up", 0, "col"), ("w_ple_gate", 0, "row"), ("w_in_odd", 0, "col"), ("rg_wa", 0, "rg"),
       ("rg_wx", 0, "rg"), ("w_out_odd", 0, "row"), ("w_gate_up", 1, "col"), ("w_down", 1, "row"),
       ("w_ple_up", 1, "col"), ("w_ple_gate", 1, "row")]
GATHER_GROUPS = [("in0", [0]), ("out0", [1]), ("ffn0", [2, 3, 4, 5]), ("mix1", [6, 7, 8, 9]), ("ffn1", [10, 11, 12, 13])]
REDUCE_LAYER1, REDUCE_FFN0, REDUCE_MIX0 = [6, 7, 8, 9, 10, 11, 12, 13], [2, 3, 4, 5], [0, 1]
BIG_NAMES = ["w_in_even", "w_out_even", "w_gate_up", "w_down", "w_ple_up", "w_ple_gate", "w_in_odd", "rg_wa", "rg_wx",
             "w_out_odd"]
SMALL_SHARDED = ["conv_w", "conv_b", "rg_ba", "rg_bx", "rg_lambda"]
REPLICATED = ["mix_pre_g", "mix_post_g", "ffn_pre_g", "ffn_post_g", "ple_norm_g", "hg_lb_logits", "hg_norm_g"]
TR = 256
TR_WIDE = 128


def _ffn_ple_fwd(h, m, l, gains, w_gu, w_d, w_pu, w_pg, pb, next_pre, hook=None):
    d = h.shape[1]
    ff = w_d.shape[0]
    hb, nf = _rowwise(_f_post_pre, [h, m], [gains["mix_post_g"][l], gains["ffn_pre_g"][l]], [(d, f32), (d, bf16)],
                      f"post_mix{l}", TR)
    gu = _mm(nf, w_gu, "nn", f32, f"ffn_up{l}", b_shards=True)
    (act,) = _rowwise(_f_swiglu, [(gu, ff, 0), (gu, ff, 1)], [], [(ff, bf16)], f"swiglu{l}", TR_WIDE)
    f = _mm(act, w_d, "nn", f32, f"ffn_down{l}")
    started = None if hook is None else hook(f)
    hc, hcb = _rowwise(_f_post, [hb, f], [gains["ffn_post_g"][l]], [(d, f32), (d, bf16)], f"post_ffn{l}", TR)
    t = _mm(hcb, w_pg, "nn", f32, f"ple_gate{l}", after=started)
    e = _mm(pb, w_pu, "nn", f32, f"ple_up{l}", b_shards=True)
    if next_pre is None:
        (hd,) = _rowwise(_f_ple, [hc, t, e], [gains["ple_norm_g"][l]], [(d, f32)], f"ple{l}", TR)
        n_next = None
    else:
        hd, n_next = _rowwise(_f_ple_pre, [hc, t, e], [gains["ple_norm_g"][l], next_pre], [(d, f32), (d, bf16)],
                              f"ple{l}", TR)
    saved = dict(h=h, m=m, hb=hb, nf=nf, gu=gu, act=act, f=f, hc=hc, hcb=hcb, t=t, e=e)
    return hd, n_next, saved


def _ffn_ple_bwd(sv, l, gains, w_gu, w_d, w_pg, pb, next_pre, dhd, dn_next, after=None, hook=None):
    ff = w_d.shape[0]
    if next_pre is None:
        (dhc, dt, de), (dg_ple,) = _rowwise_vjp(_f_ple, [sv["hc"], sv["t"], sv["e"]], [gains["ple_norm_g"][l]], [dhd],
                                                [(0, f32), (1, bf16), (2, bf16)], f"ple_bwd{l}", TR)
        dg_next = None
    else:
        (dhc, dt, de), (dg_ple, dg_next) = _rowwise_vjp(
            _f_ple_pre, [sv["hc"], sv["t"], sv["e"]], [gains["ple_norm_g"][l], next_pre], [dhd, dn_next],
            [(0, f32), (1, bf16), (2, bf16)], f"ple_bwd{l}", TR)
    gw = {}
    gw["w_ple_gate"] = _mm(sv["hcb"], dt, "tn", bf16, f"ple_gate_wgrad{l}")
    gw["w_ple_up"] = _mm(pb, de, "tn", bf16, f"ple_up_wgrad{l}", out_shards=True)
    dhcb = _mm(dt, w_pg, "nt", f32, f"ple_gate_dgrad{l}", after=after)
    (dhb, df), (dg_fpost,) = _rowwise_vjp(_f_post, [sv["hb"], sv["f"]], [gains["ffn_post_g"][l]], [dhc, dhcb],
                                          [(0, f32), (1, bf16)], f"post_ffn_bwd{l}", TR)
    gw["w_down"] = _mm(sv["act"], df, "tn", bf16, f"ffn_down_wgrad{l}")
    dact = _mm(df, w_d, "nt", f32, f"ffn_down_dgrad{l}")
    started = None if hook is None else hook(dact)
    (dgu,), _ = _rowwise_vjp(_f_swiglu, [(sv["gu"], ff, 0), (sv["gu"], ff, 1)], [], [dact], [(0, bf16), (1, bf16)],
                             f"swiglu_bwd{l}", TR_WIDE, concat=True)
    gw["w_gate_up"] = _mm(sv["nf"], dgu, "tn", bf16, f"ffn_up_wgrad{l}", out_shards=True)
    dnf = _mm(dgu, w_gu, "nt", f32, f"ffn_up_dgrad{l}", b_shards=True, after=started)
    (dh, dm), (dg_post, dg_fpre) = _rowwise_vjp(
        _f_post_pre, [sv["h"], sv["m"]], [gains["mix_post_g"][l], gains["ffn_pre_g"][l]], [dhb, dnf],
        [(0, f32), (1, bf16)], f"post_mix_bwd{l}", TR)
    gg = dict(ple_norm_g=dg_ple, ffn_post_g=dg_fpost, mix_post_g=dg_post, ffn_pre_g=dg_fpre)
    return dh, dm, gw, gg, dg_next


def kernel(x, p, mix_pre_g, mix_post_g, ffn_pre_g, ffn_post_g, ple_norm_g, w_in_even, w_out_even, hg_lb_logits, hg_norm_g, w_in_odd, conv_w, conv_b, rg_wa, rg_ba, rg_wx, rg_bx, rg_lambda, w_out_odd, w_gate_up, w_down, w_ple_up, w_ple_gate, loss_target, m_mix_pre_g, m_mix_post_g, m_ffn_pre_g, m_ffn_post_g, m_ple_norm_g, m_w_in_even, m_w_out_even, m_hg_lb_logits, m_hg_norm_g, m_w_in_odd, m_conv_w, m_conv_b, m_rg_wa, m_rg_ba, m_rg_wx, m_rg_bx, m_rg_lambda, m_w_out_odd, m_w_gate_up, m_w_down, m_w_ple_up, m_w_ple_gate, v_mix_pre_g, v_mix_post_g, v_ffn_pre_g, v_ffn_post_g, v_ple_norm_g, v_w_in_even, v_w_out_even, v_hg_lb_logits, v_hg_norm_g, v_w_in_odd, v_conv_w, v_conv_b, v_rg_wa, v_rg_ba, v_rg_wx, v_rg_bx, v_rg_lambda, v_w_out_odd, v_w_gate_up, v_w_down, v_w_ple_up, v_w_ple_gate):
    given = dict(locals())
    w = {n: given[n] for n in WEIGHTS}
    mom = {n: given["m_" + n] for n in WEIGHTS}
    var = {n: given["v_" + n] for n in WEIGHTS}
    xi, yi = lax.axis_index("x"), lax.axis_index("y")
    my_q = 2 * xi + yi
    h0 = x[0]
    s, d = h0.shape
    n_layers = p.shape[0]
    gains = {n: [w[n][l:l + 1] for l in range(n_layers)]
             for n in ("mix_pre_g", "mix_post_g", "ffn_pre_g", "ffn_post_g", "ple_norm_g")}

    kinds = [k for _, _, k in BIG]
    place = jnp.stack([my_q, lax.axis_index("c")]).astype(jnp.int32)
    small_rows = _all_gather_rows(_pack([w[n][0] for n in SMALL_SHARDED]), False)
    full, in_flight, order = {}, {}, small_rows
    for tag, group in GATHER_GROUPS:
        kinds_g = [kinds[i] for i in group]
        slots = [_cast_into_slot(w[BIG[i][0]], BIG[i][1], BIG[i][2], place) for i in group]
        send_sems, recv_sems, slots, order = _copies_start(_gather_over_ici(kinds_g), slots, 3 * len(group),
                                                           "gather_start_" + tag, order)
        in_flight[tag] = (group, kinds_g, send_sems, recv_sems, slots)
    gather_started = order

    def arrive(tag, after):
        group, kinds_g, send_sems, recv_sems, slots = in_flight[tag]
        slots = _copies_wait(_gather_over_ici(kinds_g), send_sems, recv_sems, slots, "gather_wait_" + tag, [after])
        slots = _comm_call(_gather_hand_on(slots, kinds_g), "gather_hand_on_" + tag)
        full.update({BIG[i][:2]: b for i, b in zip(group, slots)})

    handing = {}

    def landed(tag, after):
        group, kinds_g, send_sems, recv_sems, slots = in_flight[tag]
        slots = _copies_wait(_gather_over_ici(kinds_g), send_sems, recv_sems, slots, "gather_wait_" + tag, after)
        make = _gather_hand_on_copies(kinds_g)
        send_sems, recv_sems, slots, started = _copies_start(make, slots, 3 * len(group), "gather_hand_on_start_" + tag,
                                                             place)
        handing[tag] = (group, make, send_sems, recv_sems, slots)
        return started

    def handed(tag, after):
        group, make, send_sems, recv_sems, slots = handing[tag]
        slots = _copies_wait(make, send_sems, recv_sems, slots, "gather_hand_on_wait_" + tag, after)
        full.update({BIG[i][:2]: b for i, b in zip(group, slots)})

    def rows_full(name, l):
        g = full[(name, l)]
        return g.reshape(g.shape[0] * g.shape[1], g.shape[2])

    small_shapes = [w[n].shape[1:] for n in SMALL_SHARDED]
    per_chip =[_unpack(small_rows.reshape(8, -1)[2 * q], small_shapes) for q in range(N_CHIPS)]
    conv_w_f, conv_b_f, rg_ba_f, rg_bx_f, rg_lam_f = [
        jnp.concatenate([per_chip[q][i] for q in range(N_CHIPS)], axis=-1) for i in range(len(SMALL_SHARDED))]
    lru = conv_b_f.shape[0]
    conv_b_f, rg_ba_f, rg_bx_f, rg_lam_f = [v.reshape(1, lru) for v in (conv_b_f, rg_ba_f, rg_bx_f, rg_lam_f)]
    pb = [_ew(lambda v: (v,), [p[l, 0]], [bf16], f"cast_p{l}")[0] for l in range(n_layers)]

    (n0,) = _rowwise(_f_pre, [h0], [gains["mix_pre_g"][0]], [(d, bf16)], "pre_mix0", TR)
    arrive("in0", gather_started)
    proj0 = _mm(n0, full[("w_in_even", 0)], "nn", f32, "in_even", b_shards=True)
    a_out, _ = _sb_fwd(proj0)
    b_out, hg_states, _ = _hg_fwd(proj0, w["hg_lb_logits"], w["hg_norm_g"])
    arrive("out0", b_out)
    ffn0_handing = landed("ffn0", [a_out, b_out])
    mix0 = jnp.concatenate([a_out, b_out], axis=1).astype(bf16)
    m0 = _mm(mix0, rows_full("w_out_even", 0), "nn", f32, "out_even", after=ffn0_handing)
    handed("ffn0", [m0])
    h1, n1, sv0 = _ffn_ple_fwd(h0, m0, 0, gains, full[("w_gate_up", 0)], rows_full("w_down", 0), full[("w_ple_up", 0)],
                               rows_full("w_ple_gate", 0), pb[0], gains["mix_pre_g"][1],
                               hook=lambda f: landed("mix1", [f]))

    handed("mix1", [n1])
    proj1 = _mm(n1, full[("w_in_odd", 0)], "nn", f32, "in_odd", b_shards=True)
    xc, xcb = _conv_fwd(proj1, conv_w_f, conv_b_f)
    ra, ix = _bd_fwd(xcb, full[("rg_wa", 0)], full[("rg_wx", 0)])
    gate_in = [ra, ix, xc], [rg_ba_f, rg_bx_f, rg_lam_f]
    a_rec, u_rec = _rowwise(_f_rg_gate, *gate_in, [(lru, f32), (lru, f32)], "rg_gate", TR_WIDE, row_ids=True)
    hs = _scan_fwd(a_rec, u_rec)
    out_in = [(proj1, lru, 0), hs]
    (yb,) = _rowwise(_f_rg_out, out_in, [], [(lru, bf16)], "rg_out", TR_WIDE)
    m1 = _mm(yb, rows_full("w_out_odd", 0), "nn", f32, "out_odd", after=landed("ffn1", [hs]))
    handed("ffn1", [m1])
    h2, _, sv1 = _ffn_ple_fwd(h1, m1, 1, gains, full[("w_gate_up", 1)], rows_full("w_down", 1), full[("w_ple_up", 1)],
                              rows_full("w_ple_gate", 1), pb[1], None)
    dy, loss_local = _loss_head(h2, loss_target[0])
    loss = lax.psum(loss_local, ("x", "y", "c"))

    gbig, chip_part, got, reducing, core_swaps = {}, {}, {}, {}, {}

    def shard_major(group):
        gl = []
        for i in group:
            n_, l_, k_ = BIG[i]
            g_ = gbig[(n_, l_)]
            gl.append(g_.reshape(N_CHIPS, g_.shape[0] // N_CHIPS, g_.shape[1]) if k_ == "row" else g_)
        return gl

    def chips_start(group, tag, gl, theirs):
        chip_part.update({i: _add_cores(g_, t_, kinds[i], place) for i, g_, t_ in zip(group, gl, theirs)})
        make, arrays, n_sems = _exchange_chips([chip_part[i] for i in group])
        send_sems, recv_sems, arrays, started = _copies_start(make, arrays, n_sems, "rs_chip_start_" + tag, place)
        reducing[tag] = (group, make, send_sems, recv_sems, arrays)
        return started

    def core_sums(group, tag):
        gl = shard_major(group)
        theirs = _comm_call(_exchange_halves(gl, [kinds[i] for i in group]), "rs_core_exchange_" + tag)
        return chips_start(group, tag, gl, theirs)

    def core_sums_start(group, tag):
        make, arrays, n_sems = _exchange_halves_copies(shard_major(group), [kinds[i] for i in group])
        send_sems, recv_sems, arrays, started = _copies_start(make, arrays, n_sems, "rs_core_start_" + tag, place)
        core_swaps[tag] = (group, make, send_sems, recv_sems, arrays)
        return started

    def core_sums_finish(tag, after):
        group, make, send_sems, recv_sems, arrays = core_swaps[tag]
        arrays = _copies_wait(make, send_sems, recv_sems, arrays, "rs_core_wait_" + tag, after)
        return chips_start(group, tag, arrays[:len(group)], arrays[len(group):])

    def chips_arrive(tag, after):
        group, make, send_sems, recv_sems, arrays = reducing[tag]
        arrays = _copies_wait(make, send_sems, recv_sems, arrays, "rs_chip_wait_" + tag, after)
        got.update(zip(group, arrays[len(group):]))

    def reduced_shards(group, tag):
        sums = [_sum_chips(chip_part[i], got[i], kinds[i], place) for i in group]
        whole = _comm_call(_share_halves(sums, [kinds[i] for i in group]), "rs_share_halves_" + tag)
        return {BIG[i][:2]: r for i, r in zip(group, whole)}

    dh1, dm1, gw, gg1, _ = _ffn_ple_bwd(sv1, 1, gains, full[("w_gate_up", 1)], rows_full("w_down", 1),
                                        rows_full("w_ple_gate", 1), pb[1], None, dy, None)
    gbig.update({(n, 1): g for n, g in gw.items()})
    gbig[("w_out_odd", 0)] = _mm(yb, dm1, "tn", bf16, "out_odd_wgrad")
    dyo = _mm(dm1, rows_full("w_out_odd", 0), "nt", f32, "out_odd_dgrad")
    (dgate, dhs), _ = _rowwise_vjp(_f_rg_out, out_in, [], [dyo], [(0, bf16), (1, f32)], "rg_out_bwd", TR_WIDE)
    du, da = _scan_bwd(a_rec, hs, dhs)
    (dra, dix, dxc_direct), (g_ba, g_bx, g_lam) = _rowwise_vjp(
        _f_rg_gate, *gate_in, [da, du], [(0, bf16), (1, bf16), (2, f32)], "rg_gate_bwd", TR_WIDE, row_ids=True)
    dxc_gates, gbig[("rg_wa", 0)], gbig[("rg_wx", 0)] = _bd_bwd(xcb, full[("rg_wa", 0)], full[("rg_wx", 0)], dra, dix)
    dxb, g_conv_w, g_conv_b = _conv_bwd(proj1, conv_w_f, dxc_direct, dxc_gates)
    dproj1 = jnp.concatenate([dgate, dxb], axis=1)
    gbig[("w_in_odd", 0)] = _mm(n1, dproj1, "tn", bf16, "in_odd_wgrad", out_shards=True)
    dn1 = _mm(dproj1, full[("w_in_odd", 0)], "nt", f32, "in_odd_dgrad", b_shards=True)
    layer1_swapping = core_sums_start(REDUCE_LAYER1, "layer1")

    dh0, dm0, gw, gg0, g_pre1 = _ffn_ple_bwd(sv0, 0, gains, full[("w_gate_up", 0)], rows_full("w_down", 0),
                                             rows_full("w_ple_gate", 0), pb[0], gains["mix_pre_g"][1], dh1, dn1,
                                             after=layer1_swapping,
                                             hook=lambda dact: core_sums_finish("layer1", [dact]))
    gbig.update({(n, 0): g for n, g in gw.items()})
    ffn0_swapping = core_sums_start(REDUCE_FFN0, "ffn0")
    gbig[("w_out_even", 0)] = _mm(mix0, dm0, "tn", bf16, "out_even_wgrad")
    dmix = _mm(dm0, rows_full("w_out_even", 0), "nt", f32, "out_even_dgrad", after=ffn0_swapping)
    (dq, dk, dv), _ = _sb_bwd(proj0, a_out, dmix)
    ffn0_started = core_sums_finish("ffn0", [dq])
    dhg, g_logits, g_ng, _ = _hg_bwd(proj0, w["hg_lb_logits"], w["hg_norm_g"], hg_states, dmix, HEADS,
                                     after=ffn0_started)
    dproj0 = jnp.concatenate([dq, dk, dv, dhg], axis=1)
    gbig[("w_in_even", 0)] = _mm(n0, dproj0, "tn", bf16, "in_even_wgrad", out_shards=True)
    dn0 = _mm(dproj0, full[("w_in_even", 0)], "nt", f32, "in_even_dgrad", b_shards=True)
    (grad_x,), (g_pre0,) = _rowwise_vjp(_f_pre_id, [h0], [gains["mix_pre_g"][0]], [dh0, dn0], [(0, f32)], "pre_mix0_bwd", TR)
    mix0_started = core_sums(REDUCE_MIX0, "mix0")

    g_small = dict(
        mix_pre_g=jnp.concatenate([g_pre0, g_pre1], 0), mix_post_g=jnp.concatenate([gg0["mix_post_g"], gg1["mix_post_g"]], 0),
        ffn_pre_g=jnp.concatenate([gg0["ffn_pre_g"], gg1["ffn_pre_g"]], 0),
        ffn_post_g=jnp.concatenate([gg0["ffn_post_g"], gg1["ffn_post_g"]], 0),
        ple_norm_g=jnp.concatenate([gg0["ple_norm_g"], gg1["ple_norm_g"]], 0),
        hg_lb_logits=g_logits, hg_norm_g=jnp.sum(g_ng, axis=0, keepdims=True),
        conv_w=g_conv_w, conv_b=g_conv_b, rg_ba=g_ba, rg_bx=g_bx, rg_lambda=g_lam)
    order = REPLICATED + SMALL_SHARDED
    summed = dict(zip(order, _unpack(_all_reduce_rows(_pack([g_small[n] for n in order])),
                                     [g_small[n].shape for n in order])))
    grad_small = {n: summed[n].reshape(w[n].shape) for n in REPLICATED}
    for n in SMALL_SHARDED:
        width = w[n].shape[-1]
        if n in ("rg_ba", "rg_bx"):
            full_g = summed[n].reshape(1, -1, RG_BLOCK)
        else:
            full_g = summed[n].reshape((1,) + g_small[n].shape) if n == "conv_w" else summed[n]
        grad_small[n] = lax.dynamic_slice_in_dim(full_g, my_q * width, width, axis=full_g.ndim - 1).reshape(w[n].shape)

    grads = dict(grad_small)
    delta, new_m, new_v = {}, {}, {}

    def update(names, by_layer):
        for n in names:
            grads[n] = jnp.stack([by_layer[(n, l)] for l in range(w[n].shape[0])])
            delta[n], new_m[n], new_v[n] = _ew(_f_adamw, [w[n], grads[n], mom[n], var[n]], [f32, f32, f32], "adamw_" + n)

    chips_arrive("layer1", [mix0_started])
    chips_arrive("ffn0", [mix0_started])
    rest = [n for n in BIG_NAMES if all(BIG[i][0] != n for i in REDUCE_MIX0)]
    update(rest, {**reduced_shards(REDUCE_LAYER1, "layer1"), **reduced_shards(REDUCE_FFN0, "ffn0")})
    chips_arrive("mix0", [delta[n] for n in rest])
    update([n for n in BIG_NAMES if n not in rest], reduced_shards(REDUCE_MIX0, "mix0"))
    small = REPLICATED + SMALL_SHARDED
    packed = [_pack([src[n] for n in small]) for src in (w, grads, mom, var)]
    outs = _ew(_f_adamw, packed, [f32, f32, f32], "adamw_small")
    for dst, o in zip((delta, new_m, new_v), outs):
        dst.update(zip(small, _unpack(o, [w[n].shape for n in small])))
    return (loss, grad_x.reshape(x.shape), *[grads[n] for n in WEIGHTS], *[delta[n] for n in WEIGHTS],
            *[new_m[n] for n in WEIGHTS], *[new_v[n] for n in WEIGHTS])


def _all_reduce_rows(v):
    return _all_gather_rows(v, True)
```
